```python
import jax, jax.numpy as jnp
from jax import lax
import numpy as np

D_MODEL = 1024
BATCH = 16
SEQ = 2048
DEPTH = 1

MEM_LEN = 256
D_MIX = 2 * D_MODEL
EPS = 1e-6

ATT_HEADS = 8
ATT_KV_HEADS = 2
ATT_HEAD_DIM = 64
ATT_Q_W = ATT_HEADS * ATT_HEAD_DIM
ATT_KV_W = ATT_KV_HEADS * ATT_HEAD_DIM
WINDOW = 128
ATT_BLOCK = 128
ROPE_THETA = 500000.0
ROPE_DIM = ATT_HEAD_DIM // 4

SSD_HEADS = 16
SSD_HEAD_DIM = 64
SSD_WIDTH = SSD_HEADS * SSD_HEAD_DIM
SSD_GROUPS = 2
SSD_STATE = 128
SSD_CONV = 5
SSD_CHUNK = 128
SSD_XBC_W = SSD_WIDTH + 2 * SSD_GROUPS * SSD_STATE
SSD_DT_W = 2 * SSD_HEADS

XATT_HEADS = 4
XATT_HEAD_DIM = 128
XATT_W = XATT_HEADS * XATT_HEAD_DIM

D_FF = 4 * D_MODEL

IN_SPLITS = [ATT_Q_W, ATT_KV_W, ATT_KV_W, SSD_WIDTH, SSD_XBC_W, SSD_DT_W, XATT_W]
D_IN = ATT_Q_W + 2 * ATT_KV_W + SSD_WIDTH + SSD_XBC_W + SSD_DT_W + XATT_W

kernel_name = "hybrid_swa_ssd_memxattn_block"


def rmsnorm(t, w):
    t32 = t.astype(jnp.float32)
    t32 = t32 * lax.rsqrt(jnp.mean(t32 * t32, axis=-1, keepdims=True) + EPS)
    return t32.astype(t.dtype) * w


def partial_rope(t, pos):
    half = ROPE_DIM // 2
    inv = ROPE_THETA ** (-jnp.arange(0, ROPE_DIM, 2, dtype=jnp.float32) / ROPE_DIM)
    ang = pos.astype(jnp.float32)[:, None] * inv[None, :]
    cos = jnp.cos(ang)[None, :, None, :]
    sin = jnp.sin(ang)[None, :, None, :]
    t1, t2, tp = t[..., :half], t[..., half:ROPE_DIM], t[..., ROPE_DIM:]
    rot = jnp.concatenate([t1 * cos - t2 * sin, t2 * cos + t1 * sin], axis=-1)
    return jnp.concatenate([rot.astype(t.dtype), tp], axis=-1)


def windowed_gqa(q, k, v, sink):
    b, L = q.shape[:2]
    nb = L // ATT_BLOCK
    R = ATT_HEADS // ATT_KV_HEADS
    d = ATT_HEAD_DIM
    qb = q.reshape(b, nb, ATT_BLOCK, ATT_KV_HEADS, R, d)
    pad = ((0, 0), (ATT_BLOCK, ATT_BLOCK), (0, 0), (0, 0))
    kp = jnp.pad(k, pad).reshape(b, nb + 2, ATT_BLOCK, ATT_KV_HEADS, d)
    vp = jnp.pad(v, pad).reshape(b, nb + 2, ATT_BLOCK, ATT_KV_HEADS, d)
    kb = jnp.concatenate([kp[:, :-2], kp[:, 1:-1], kp[:, 2:]], axis=2)
    vb = jnp.concatenate([vp[:, :-2], vp[:, 1:-1], vp[:, 2:]], axis=2)
    s = jnp.einsum('bnqgrd,bnkgd->bngrqk', qb, kb).astype(jnp.float32) * (d ** -0.5)
    blk = jnp.arange(nb)[:, None]
    qpos = blk * ATT_BLOCK + jnp.arange(ATT_BLOCK)[None, :]
    kpos = (blk - 1) * ATT_BLOCK + jnp.arange(3 * ATT_BLOCK)[None, :]
    rel = kpos[:, None, :] - qpos[:, :, None]
    valid = (jnp.abs(rel) <= WINDOW) & (kpos[:, None, :] >= 0) & (kpos[:, None, :] < L)
    s = jnp.where(valid[None, :, None, None], s, -1e30)
    sink_col = jnp.broadcast_to(
        sink.astype(jnp.float32).reshape(1, 1, ATT_KV_HEADS, R, 1, 1), s.shape[:-1] + (1,))
    p = jax.nn.softmax(jnp.concatenate([s, sink_col], axis=-1), axis=-1)[..., :-1]
    o = jnp.einsum('bngrqk,bnkgd->bnqgrd', p.astype(v.dtype), vb)
    return o.reshape(b, L, ATT_Q_W)


def centred_depthwise_conv(u, w, bias):
    ch = u.shape[-1]
    out = lax.conv_general_dilated(
        u, w[:, None, :].astype(u.dtype), window_strides=(1,),
        padding=[(SSD_CONV // 2, SSD_CONV // 2)],
        dimension_numbers=('NWC', 'WIO', 'NWC'), feature_group_count=ch)
    return out + bias


def segsum_exp(a_cs):
    Q = a_cs.shape[-1]
    mask = jnp.tril(jnp.ones((Q, Q), dtype=bool))
    diff = a_cs[..., :, None] - a_cs[..., None, :]
    return jnp.where(mask, jnp.exp(jnp.where(mask, diff, 0.0)), 0.0)


def ssd_chunked(xdt, dtA, B, C):
    b, L, H, P = xdt.shape
    G, N = B.shape[-2:]
    R = H // G
    Q = SSD_CHUNK
    nc = L // Q
    xc = xdt.reshape(b, nc, Q, G, R, P)
    Bc = B.reshape(b, nc, Q, G, N)
    Cc = C.reshape(b, nc, Q, G, N)
    a = dtA.astype(jnp.float32).reshape(b, nc, Q, G, R).transpose(0, 3, 4, 1, 2)
    a_cs = jnp.cumsum(a, axis=-1)
    Lm = segsum_exp(a_cs)
    cb = jnp.einsum('bclgn,bcsgn->bgcls', Cc, Bc)
    y_diag = jnp.einsum('bgcls,bgrcls,bcsgrp->bclgrp', cb, Lm, xc)
    decay_states = jnp.exp(a_cs[..., -1:] - a_cs)
    states = jnp.einsum('bcsgn,bgrcs,bcsgrp->bcgrpn', Bc, decay_states, xc)
    chunk_decay = jnp.exp(a_cs[..., -1])

    def step(h, inp):
        dec, st = inp
        return dec[..., None, None] * h + st, h

    init = jnp.zeros((b, G, R, P, N), dtype=states.dtype)
    _, prev = lax.scan(step, init, (jnp.moveaxis(chunk_decay, -1, 0), jnp.moveaxis(states, 1, 0)))
    prev = jnp.moveaxis(prev, 0, 1)
    y_off = jnp.einsum('bclgn,bgrcl,bcgrpn->bclgrp', Cc, jnp.exp(a_cs), prev)
    return (y_diag + y_off).reshape(b, L, H, P).astype(xdt.dtype)


def bidirectional_ssd(z, xbc, dt_raw, conv_w, conv_b, dt_bias_f, dt_bias_b,
                      a_log_f, a_log_b, ssd_d, ssd_norm_w):
    b, L = z.shape[:2]
    xbc = jax.nn.silu(centred_depthwise_conv(xbc, conv_w, conv_b))
    xs = xbc[..., :SSD_WIDTH].reshape(b, L, SSD_HEADS, SSD_HEAD_DIM)
    Bm = xbc[..., SSD_WIDTH:SSD_WIDTH + SSD_GROUPS * SSD_STATE].reshape(b, L, SSD_GROUPS, SSD_STATE)
    Cm = xbc[..., SSD_WIDTH + SSD_GROUPS * SSD_STATE:].reshape(b, L, SSD_GROUPS, SSD_STATE)
    dt_f = jax.nn.softplus(dt_raw[..., :SSD_HEADS].astype(jnp.float32) + dt_bias_f.astype(jnp.float32))
    dt_b = jax.nn.softplus(dt_raw[..., SSD_HEADS:].astype(jnp.float32) + dt_bias_b.astype(jnp.float32))
    A_f = -jnp.exp(a_log_f.astype(jnp.float32))
    A_b = -jnp.exp(a_log_b.astype(jnp.float32))
    y_f = ssd_chunked((xs * dt_f[..., None]).astype(xs.dtype), dt_f * A_f, Bm, Cm)
    flip = lambda t: jnp.flip(t, axis=1)
    y_b = flip(ssd_chunked(flip((xs * dt_b[..., None]).astype(xs.dtype)), flip(dt_b * A_b),
                           flip(Bm), flip(Cm)))
    y = y_f + y_b + ssd_d[:, None] * xs
    y = y.reshape(b, L, SSD_WIDTH) * jax.nn.silu(z)
    yg = rmsnorm(y.reshape(b, L, SSD_GROUPS, SSD_WIDTH // SSD_GROUPS),
                 ssd_norm_w.reshape(SSD_GROUPS, SSD_WIDTH // SSD_GROUPS))
    return yg.reshape(b, L, SSD_WIDTH)


def memory_cross_attention(qx, mem, mem_norm_w, w_mem_kv, xq_norm_w, xk_norm_w):
    b, L = qx.shape[:2]
    M = mem.shape[1]
    kv = rmsnorm(mem, mem_norm_w) @ w_mem_kv
    km = rmsnorm(kv[..., :XATT_W].reshape(b, M, XATT_HEADS, XATT_HEAD_DIM), xk_norm_w)
    vm = kv[..., XATT_W:].reshape(b, M, XATT_HEADS, XATT_HEAD_DIM)
    q = rmsnorm(qx.reshape(b, L, XATT_HEADS, XATT_HEAD_DIM), xq_norm_w)
    s = jnp.einsum('blhd,bmhd->bhlm', q, km).astype(jnp.float32) * (XATT_HEAD_DIM ** -0.5)
    p = jax.nn.softmax(s, axis=-1)
    o = jnp.einsum('bhlm,bmhd->blhd', p.astype(vm.dtype), vm)
    return o.reshape(b, L, XATT_W)


def setup_inputs(seed: int = 0) -> dict:
    key = jax.random.key(seed)
    ks = jax.random.split(key, 24)
    nrm = lambda k, shape, fan_in: jax.random.normal(k, shape, jnp.float32) * (fan_in ** -0.5)
    gain = lambda k, n: 1.0 + 0.02 * jax.random.normal(k, (DEPTH, n), jnp.float32)
    dt0 = jnp.exp(jax.random.uniform(ks[10], (DEPTH, 2, SSD_HEADS), jnp.float32,
                                     minval=np.log(1e-3), maxval=np.log(1e-1)))
    dt_bias = dt0 + jnp.log(-jnp.expm1(-dt0))
    a_log = jnp.log(jax.random.uniform(ks[11], (DEPTH, 2, SSD_HEADS), jnp.float32,
                                       minval=1.0, maxval=16.0))
    return {
        "x": jax.random.normal(ks[0], (BATCH, SEQ, D_MODEL), jnp.float32),
        "mem": jax.random.normal(ks[1], (BATCH, MEM_LEN, D_MODEL), jnp.float32),
        "norm_mix_w": gain(ks[2], D_MODEL),
        "w_in": nrm(ks[3], (DEPTH, D_MODEL, D_IN), D_MODEL),
        "q_norm_w": gain(ks[4], ATT_HEAD_DIM),
        "k_norm_w": gain(ks[5], ATT_HEAD_DIM),
        "attn_sink": 0.5 * jax.random.normal(ks[6], (DEPTH, ATT_HEADS), jnp.float32),
        "conv_w": nrm(ks[7], (DEPTH, SSD_CONV, SSD_XBC_W), SSD_CONV),
        "conv_b": 0.02 * jax.random.normal(ks[8], (DEPTH, SSD_XBC_W), jnp.float32),
        "dt_bias_f": dt_bias[:, 0],
        "dt_bias_b": dt_bias[:, 1],
        "a_log_f": a_log[:, 0],
        "a_log_b": a_log[:, 1],
        "ssd_d": 1.0 + 0.1 * jax.random.normal(ks[9], (DEPTH, SSD_HEADS), jnp.float32),
        "ssd_norm_w": gain(ks[12], SSD_WIDTH),
        "mem_norm_w": gain(ks[13], D_MODEL),
        "w_mem_kv": nrm(ks[14], (DEPTH, D_MODEL, 2 * XATT_W), D_MODEL),
        "xq_norm_w": gain(ks[15], XATT_HEAD_DIM),
        "xk_norm_w": gain(ks[16], XATT_HEAD_DIM),
        "w_out": nrm(ks[17], (DEPTH, D_MIX, D_MODEL), D_MIX),
        "norm_mlp_w": gain(ks[18], D_MODEL),
        "w_mlp_up": nrm(ks[19], (DEPTH, D_MODEL, D_FF), D_MODEL),
        "w_mlp_down": nrm(ks[20], (DEPTH, D_FF, D_MODEL), D_FF),
    }


def reference(x, mem, norm_mix_w, w_in, q_norm_w, k_norm_w, attn_sink, conv_w, conv_b,
              dt_bias_f, dt_bias_b, a_log_f, a_log_b, ssd_d, ssd_norm_w, mem_norm_w,
              w_mem_kv, xq_norm_w, xk_norm_w, w_out, norm_mlp_w, w_mlp_up, w_mlp_down):
    b, L, _ = x.shape
    pos = jnp.arange(L, dtype=jnp.int32)
    split_idx = [int(c) for c in np.cumsum(IN_SPLITS)[:-1]]
    for i in range(DEPTH):
        h = rmsnorm(x, norm_mix_w[i])
        proj = h @ w_in[i]
        q, k, v, z, xbc, dt_raw, qx = jnp.split(proj, split_idx, axis=-1)
        q = partial_rope(rmsnorm(q.reshape(b, L, ATT_HEADS, ATT_HEAD_DIM), q_norm_w[i]), pos)
        k = partial_rope(rmsnorm(k.reshape(b, L, ATT_KV_HEADS, ATT_HEAD_DIM), k_norm_w[i]), pos)
        v = v.reshape(b, L, ATT_KV_HEADS, ATT_HEAD_DIM)
        attn_out = windowed_gqa(q, k, v, attn_sink[i])
        ssd_out = bidirectional_ssd(z, xbc, dt_raw, conv_w[i], conv_b[i], dt_bias_f[i],
                                    dt_bias_b[i], a_log_f[i], a_log_b[i], ssd_d[i],
                                    ssd_norm_w[i])
        xatt_out = memory_cross_attention(qx, mem, mem_norm_w[i], w_mem_kv[i],
                                          xq_norm_w[i], xk_norm_w[i])
        mix = jnp.concatenate([attn_out, ssd_out, xatt_out], axis=-1) @ w_out[i]
        x = x + mix
        u = rmsnorm(x, norm_mlp_w[i]) @ w_mlp_up[i]
        x = x + jnp.square(jax.nn.relu(u)) @ w_mlp_down[i]
    return x
```

```python
import functools

import numpy as np
import jax
import jax.numpy as jnp
from jax import lax
from jax.experimental import pallas as pl
from jax.experimental.pallas import tpu as pltpu

D_MODEL = 1024
EPS = 1e-6

ATT_HEADS = 8
ATT_KV_HEADS = 2
ATT_HEAD_DIM = 64
ATT_Q_W = ATT_HEADS * ATT_HEAD_DIM
ATT_KV_W = ATT_KV_HEADS * ATT_HEAD_DIM
WINDOW = 128
ATT_BLOCK = 128
ROPE_THETA = 500000.0
ROPE_DIM = ATT_HEAD_DIM // 4

SSD_HEADS = 16
SSD_HEAD_DIM = 64
SSD_WIDTH = SSD_HEADS * SSD_HEAD_DIM
SSD_GROUPS = 2
SSD_STATE = 128
SSD_CONV = 5
SSD_CHUNK = 128
SSD_XBC_W = SSD_WIDTH + 2 * SSD_GROUPS * SSD_STATE
SSD_DT_W = 2 * SSD_HEADS

XATT_HEADS = 4
XATT_HEAD_DIM = 128
XATT_W = XATT_HEADS * XATT_HEAD_DIM

D_FF = 4 * D_MODEL
D_MIX = 2 * D_MODEL

LANES = 128
BF16_SUBLANES = 16
VMEM_LIMIT = 56 * 1024 * 1024

KDUP_W = 2 * ATT_KV_W
DT_PAD_W = LANES
C_Q = 0
C_K = C_Q + ATT_Q_W
C_V = C_K + KDUP_W
C_Z = C_V + KDUP_W
C_XBC = C_Z + SSD_WIDTH
C_QX = C_XBC + SSD_XBC_W
C_DT = C_QX + XATT_W
D_IN_P = C_DT + DT_PAD_W

NEG_BIG = -1e30

bf16 = jnp.bfloat16
f32 = jnp.float32


def _dot(a, b):
    return jnp.dot(a, b, preferred_element_type=f32)


def _dot_nt(a, b):
    return lax.dot_general(a, b, (((1,), (1,)), ((), ())), preferred_element_type=f32)


def _split3(x):
    x1 = x.astype(bf16)
    r1 = x - x1.astype(f32)
    x2 = r1.astype(bf16)
    r2 = r1 - x2.astype(f32)
    return x1, x2, r2.astype(bf16)


def _dot3(x, m01):
    x1, x2, x3 = _split3(x)
    return _dot(x1, m01) + _dot(x2, m01) + _dot(x3, m01)


def _iota(shape, dim):
    return lax.broadcasted_iota(jnp.int32, shape, dim)


def _segment_ones(width, seg):
    r = _iota((width, width), 0) // seg
    c = _iota((width, width), 1) // seg
    return jnp.where(r == c, 1.0, 0.0).astype(bf16)


def _head_mean_sq(t, seg):
    sq = t * t
    hi = sq.astype(bf16)
    lo = (sq - hi.astype(f32)).astype(bf16)
    ones = _segment_ones(t.shape[1], seg)
    return (_dot(hi, ones) + _dot(lo, ones)) * (1.0 / seg)


def _rope(t, cos, sin_lo, sin_hi):
    half = ROPE_DIM // 2
    return (t * cos + pltpu.roll(t, LANES - half, 1) * sin_lo
            + pltpu.roll(t, half, 1) * sin_hi)


def _in_proj_kernel(x_ref, nw_ref, w_ref, qnw_ref, knw_ref, xqnw_ref,
                    cos_ref, sinlo_ref, sinhi_ref,
                    q_ref, k_ref, v_ref, z_ref, xbc_ref, qx_ref, dt_ref):
    x = x_ref[...]
    ms = jnp.mean(x * x, axis=-1, keepdims=True)
    h = ((x * lax.rsqrt(ms + EPS)) * nw_ref[...]).astype(bf16)

    cos = cos_ref[...]
    sin_lo = sinlo_ref[...]
    sin_hi = sinhi_ref[...]

    pq = _dot(h, w_ref[:, C_Q:C_Q + ATT_Q_W])
    qn = pq * lax.rsqrt(_head_mean_sq(pq, ATT_HEAD_DIM) + EPS) * qnw_ref[...]
    for c in range(ATT_Q_W // LANES):
        sl = slice(c * LANES, (c + 1) * LANES)
        q_ref[:, sl] = (_rope(qn[:, sl], cos, sin_lo, sin_hi)
                        * (ATT_HEAD_DIM ** -0.5)).astype(bf16)

    pk = _dot(h, w_ref[:, C_K:C_K + KDUP_W])
    kn = pk * lax.rsqrt(_head_mean_sq(pk, ATT_HEAD_DIM) + EPS) * knw_ref[...]
    for c in range(KDUP_W // LANES):
        sl = slice(c * LANES, (c + 1) * LANES)
        k_ref[:, sl] = _rope(kn[:, sl], cos, sin_lo, sin_hi).astype(bf16)

    v_ref[...] = _dot(h, w_ref[:, C_V:C_V + KDUP_W]).astype(bf16)
    z_ref[...] = _dot(h, w_ref[:, C_Z:C_Z + SSD_WIDTH]).astype(bf16)
    xbc_ref[...] = _dot(h, w_ref[:, C_XBC:C_XBC + SSD_XBC_W]).astype(bf16)

    pqx = _dot(h, w_ref[:, C_QX:C_QX + XATT_W])
    for c in range(XATT_HEADS):
        sl = slice(c * XATT_HEAD_DIM, (c + 1) * XATT_HEAD_DIM)
        t = pqx[:, sl]
        tn = t * lax.rsqrt(jnp.mean(t * t, axis=-1, keepdims=True) + EPS) * xqnw_ref[...]
        qx_ref[:, sl] = (tn * (XATT_HEAD_DIM ** -0.5)).astype(bf16)

    dt_ref[...] = _dot(h, w_ref[:, C_DT:C_DT + DT_PAD_W])


def _in_proj(x2, norm_w, w_in_p, qnw, knw, xqnw, cos_t, sinlo_t, sinhi_t, seq, tm):
    tokens = x2.shape[0]
    pos_blocks = seq // tm
    row = lambda i: (i, 0)
    const = lambda i: (0, 0)
    pos = lambda i: (i % pos_blocks, 0)
    out_widths = [ATT_Q_W, KDUP_W, KDUP_W, SSD_WIDTH, SSD_XBC_W, XATT_W, DT_PAD_W]
    out_dtypes = [bf16, bf16, bf16, bf16, bf16, bf16, f32]
    return pl.pallas_call(
        _in_proj_kernel,
        grid=(tokens // tm,),
        in_specs=[
            pl.BlockSpec((tm, D_MODEL), row),
            pl.BlockSpec((1, D_MODEL), const),
            pl.BlockSpec((D_MODEL, D_IN_P), const, pipeline_mode=pl.Buffered(1)),
            pl.BlockSpec((1, ATT_Q_W), const),
            pl.BlockSpec((1, KDUP_W), const),
            pl.BlockSpec((1, XATT_HEAD_DIM), const),
            pl.BlockSpec((tm, LANES), pos),
            pl.BlockSpec((tm, LANES), pos),
            pl.BlockSpec((tm, LANES), pos),
        ],
        out_specs=[pl.BlockSpec((tm, w), row) for w in out_widths],
        out_shape=[jax.ShapeDtypeStruct((tokens, w), d) for w, d in zip(out_widths, out_dtypes)],
        compiler_params=pltpu.CompilerParams(
            dimension_semantics=("arbitrary",), vmem_limit_bytes=VMEM_LIMIT),
        name="in_proj",
    )(x2, norm_w, w_in_p, qnw, knw, xqnw, cos_t, sinlo_t, sinhi_t)


def _attn_kernel(sink_ref, q_ref, kp_ref, kc_ref, kn_ref, vp_ref, vc_ref, vn_ref, o_ref):
    n = pl.program_id(1)
    nb = pl.num_programs(1)
    blk = ATT_BLOCK
    heads_per_kv = ATT_HEADS // ATT_KV_HEADS

    row = _iota((blk, 3 * blk), 0)
    col = _iota((blk, 3 * blk), 1)
    valid = (col >= row) & (col <= row + 2 * WINDOW)
    valid = valid & ((col >= blk) | (n > 0)) & ((col < 2 * blk) | (n < nb - 1))
    valid4 = jnp.concatenate([valid] * heads_per_kv, axis=0)
    low_lanes = _iota((blk, LANES), 1) < ATT_HEAD_DIM

    for g in range(ATT_KV_HEADS):
        gs = slice(g * LANES, (g + 1) * LANES)
        kcat = jnp.concatenate([kp_ref[:, gs], kc_ref[:, gs], kn_ref[:, gs]], axis=0)
        vcat = jnp.concatenate([vp_ref[:, gs], vc_ref[:, gs], vn_ref[:, gs]], axis=0)
        qs = []
        sinks = []
        for j in range(heads_per_kv):
            hd = g * heads_per_kv + j
            pair = q_ref[:, (hd // 2) * LANES:(hd // 2 + 1) * LANES]
            keep = low_lanes if hd % 2 == 0 else jnp.logical_not(low_lanes)
            qs.append(jnp.where(keep, pair, jnp.zeros_like(pair)))
            sinks.append(jnp.full((blk, 1), sink_ref[hd], f32))
        qg = jnp.concatenate(qs, axis=0)
        sink = jnp.concatenate(sinks, axis=0)
        s = _dot_nt(qg, kcat)
        s = jnp.where(valid4, s, NEG_BIG)
        m = jnp.maximum(jnp.max(s, axis=-1, keepdims=True), sink)
        p = jnp.exp(s - m)
        den = jnp.sum(p, axis=-1, keepdims=True) + jnp.exp(sink - m)
        o = _dot(p.astype(bf16), vcat) / den
        for jp in range(heads_per_kv // 2):
            even = o[(2 * jp) * blk:(2 * jp + 1) * blk]
            odd = o[(2 * jp + 1) * blk:(2 * jp + 2) * blk]
            pr = g * (heads_per_kv // 2) + jp
            o_ref[:, pr * LANES:(pr + 1) * LANES] = jnp.where(low_lanes, even, odd).astype(bf16)


def _attention(sink, q, kdup, vdup, batch, seq):
    nb = seq // ATT_BLOCK
    blk = ATT_BLOCK
    cur = lambda b, n: (b * nb + n, 0)
    prev = lambda b, n: (b * nb + jnp.maximum(n - 1, 0), 0)
    nxt = lambda b, n: (b * nb + jnp.minimum(n + 1, nb - 1), 0)
    return pl.pallas_call(
        _attn_kernel,
        grid=(batch, nb),
        in_specs=[
            pl.BlockSpec(memory_space=pltpu.SMEM),
            pl.BlockSpec((blk, ATT_Q_W), cur),
            pl.BlockSpec((blk, KDUP_W), prev),
            pl.BlockSpec((blk, KDUP_W), cur),
            pl.BlockSpec((blk, KDUP_W), nxt),
            pl.BlockSpec((blk, KDUP_W), prev),
            pl.BlockSpec((blk, KDUP_W), cur),
            pl.BlockSpec((blk, KDUP_W), nxt),
        ],
        out_specs=pl.BlockSpec((blk, ATT_Q_W), cur),
        out_shape=jax.ShapeDtypeStruct((batch * seq, ATT_Q_W), bf16),
        compiler_params=pltpu.CompilerParams(
            dimension_semantics=("arbitrary", "arbitrary"), vmem_limit_bytes=VMEM_LIMIT),
        name="attention",
    )(sink, q, kdup, kdup, kdup, vdup, vdup, vdup)


CONV_HALO = BF16_SUBLANES
C_B = SSD_WIDTH
C_C = SSD_WIDTH + SSD_GROUPS * SSD_STATE


def _ssd_kernel(d_ref, xm_ref, xp_ref, xn_ref, dt_ref, z_ref, cw_ref, cb_ref,
                dtb_ref, arow_ref, nw_ref, y_ref,
                win_s, xc_s, rows_s, cols_s, dtT_s, hf_s, sb_s, hf_run, hb_run):
    ph = pl.program_id(1)
    c = pl.program_id(2)
    nc = pl.num_programs(2)
    Q = SSD_CHUNK
    heads_per_group = SSD_HEADS // SSD_GROUPS
    low_lanes = _iota((Q, LANES), 1) < SSD_HEAD_DIM
    low_row = _iota((1, LANES), 1) < SSD_HEAD_DIM
    li = _iota((Q, Q), 0)
    si = _iota((Q, Q), 1)

    @pl.when(ph == 0)
    def _phase0():
        @pl.when(c == 0)
        def _():
            hf_run[...] = jnp.zeros_like(hf_run)

        zero_halo = jnp.zeros((CONV_HALO, SSD_XBC_W), f32)
        win_s[0:CONV_HALO, :] = jnp.where(c > 0, xp_ref[...].astype(f32), zero_halo)
        win_s[CONV_HALO:CONV_HALO + Q, :] = xm_ref[...].astype(f32)
        win_s[CONV_HALO + Q:, :] = jnp.where(c < nc - 1, xn_ref[...].astype(f32), zero_halo)
        acc = jnp.broadcast_to(cb_ref[...], (Q, SSD_XBC_W))
        for k in range(SSD_CONV):
            off = CONV_HALO - SSD_CONV // 2 + k
            acc = acc + win_s[off:off + Q, :] * cw_ref[k:k + 1, :]
        xc = acc / (1.0 + jnp.exp(-acc))
        xc_s[pl.ds(pl.multiple_of(c * Q, Q), Q), :] = xc.astype(bf16)

        dt = dt_ref[...] + dtb_ref[...]
        dt = jnp.maximum(dt, 0.0) + jnp.log1p(jnp.exp(-jnp.abs(dt)))
        a = dt * arow_ref[...]
        dtT = dt.T
        aT = a.T
        incl_le = jnp.where(li <= si, 1.0, 0.0).astype(bf16)
        incl_ge = jnp.where(li >= si, 1.0, 0.0).astype(bf16)
        pre = _dot3(aT[0:SSD_HEADS], incl_le)
        suf = _dot3(aT[SSD_HEADS:2 * SSD_HEADS], incl_ge)
        rows = jnp.concatenate(
            [pre, suf, jnp.zeros((Q - 2 * SSD_HEADS, Q), f32)], axis=0)
        rows_s[c] = rows
        cols_s[c] = rows.T
        dtT_s[c] = dtT

        pre_end = jnp.broadcast_to(pre[:, Q - 1:Q], (SSD_HEADS, Q))
        suf_end = jnp.broadcast_to(suf[:, 0:1], (SSD_HEADS, Q))
        wf = jnp.exp(pre_end - pre) * dtT[0:SSD_HEADS]
        wb = jnp.exp(suf_end - suf) * dtT[SSD_HEADS:2 * SSD_HEADS]
        decf = jnp.exp(pre_end)

        for g in range(SSD_GROUPS):
            bT = xc[:, C_B + g * SSD_STATE:C_B + (g + 1) * SSD_STATE].T
            for m in range(heads_per_group // 2):
                h0 = g * heads_per_group + 2 * m
                h1 = h0 + 1
                lhs = jnp.concatenate(
                    [bT * wf[h0:h0 + 1], bT * wb[h0:h0 + 1],
                     bT * wf[h1:h1 + 1], bT * wb[h1:h1 + 1]], axis=0).astype(bf16)
                cs = slice(h0 * SSD_HEAD_DIM, h0 * SSD_HEAD_DIM + LANES)
                res = _dot(lhs, xc[:, cs].astype(bf16))
                s_f = jnp.where(low_lanes, res[0:Q], res[2 * Q:3 * Q])
                s_b = jnp.where(low_lanes, res[Q:2 * Q], res[3 * Q:4 * Q])
                dec = jnp.where(low_row, decf[h0:h0 + 1], decf[h1:h1 + 1])
                prev = hf_run[:, cs]
                hf_s[c, :, cs] = prev.astype(bf16)
                hf_run[:, cs] = dec * prev + s_f
                sb_s[c, :, cs] = s_b.astype(bf16)

    @pl.when(ph == 1)
    def _phase1():
        @pl.when(c == 0)
        def _():
            hb_run[...] = jnp.zeros_like(hb_run)

        cc = nc - 1 - c
        t0 = pl.multiple_of(cc * Q, Q)
        rows = rows_s[cc]
        cols = cols_s[cc]
        dtT = dtT_s[cc]
        decb = jnp.exp(jnp.broadcast_to(rows[SSD_HEADS:2 * SSD_HEADS, 0:1], (SSD_HEADS, Q)))
        lower = li >= si
        upper = si >= li
        eye = li == si

        for g in range(SSD_GROUPS):
            bg = xc_s[pl.ds(t0, Q), C_B + g * SSD_STATE:C_B + (g + 1) * SSD_STATE]
            cg = xc_s[pl.ds(t0, Q), C_C + g * SSD_STATE:C_C + (g + 1) * SSD_STATE]
            cbm = _dot_nt(cg, bg)
            cgf = cg.astype(f32)
            for m in range(heads_per_group // 2):
                h0 = g * heads_per_group + 2 * m
                parts = []
                for hd in (h0, h0 + 1):
                    hb = SSD_HEADS + hd
                    colf = jnp.broadcast_to(cols[:, hd:hd + 1], (Q, Q))
                    colb = jnp.broadcast_to(cols[:, hb:hb + 1], (Q, Q))
                    gf = jnp.exp(jnp.where(lower, colf - rows[hd:hd + 1], NEG_BIG)) * dtT[hd:hd + 1]
                    gb = jnp.exp(jnp.where(upper, colb - rows[hb:hb + 1], NEG_BIG)) * dtT[hb:hb + 1]
                    gm = cbm * (gf + gb) + jnp.where(eye, d_ref[hd], 0.0)
                    parts.append(jnp.concatenate(
                        [gm, cgf * jnp.exp(colf), cgf * jnp.exp(colb)], axis=1).astype(bf16))
                lhs = jnp.concatenate(parts, axis=0)
                cs = slice(h0 * SSD_HEAD_DIM, h0 * SSD_HEAD_DIM + LANES)
                hb_prev = hb_run[:, cs]
                rhs = jnp.concatenate(
                    [xc_s[pl.ds(t0, Q), cs], hf_s[cc, :, cs], hb_prev.astype(bf16)], axis=0)
                res = _dot(lhs, rhs)
                win_s[0:Q, cs] = jnp.where(low_lanes, res[0:Q], res[Q:2 * Q])
                dec = jnp.where(low_row, decb[h0:h0 + 1], decb[h0 + 1:h0 + 2])
                hb_run[:, cs] = dec * hb_prev + sb_s[cc, :, cs].astype(f32)

        zf = z_ref[...].astype(f32)
        y = win_s[0:Q, 0:SSD_WIDTH] * (zf / (1.0 + jnp.exp(-zf)))
        gw = SSD_WIDTH // SSD_GROUPS
        for g in range(SSD_GROUPS):
            yg = y[:, g * gw:(g + 1) * gw]
            ms = jnp.mean(yg * yg, axis=-1, keepdims=True)
            y_ref[:, g * gw:(g + 1) * gw] = (
                yg * lax.rsqrt(ms + EPS) * nw_ref[:, g * gw:(g + 1) * gw]).astype(bf16)


def _ssd(ssd_d, xbc, dt_raw, z, conv_w, conv_b, dt_bias_row, a_row, norm_w, batch, seq):
    Q = SSD_CHUNK
    nc = seq // Q
    hpc = Q // CONV_HALO
    nhb = seq // CONV_HALO
    last = nc - 1

    def xm_map(b, ph, c):
        return (b * nc + c * (1 - ph) + last * ph, 0)

    def xp_map(b, ph, c):
        ce = c * (1 - ph) + last * ph
        return (b * nhb + jnp.maximum(ce * hpc - 1, 0), 0)

    def xn_map(b, ph, c):
        ce = c * (1 - ph) + last * ph
        return (b * nhb + jnp.minimum((ce + 1) * hpc, nhb - 1), 0)

    def back_map(b, ph, c):
        return (b * nc + last - c * ph, 0)

    const = lambda b, ph, c: (0, 0)
    return pl.pallas_call(
        _ssd_kernel,
        grid=(batch, 2, nc),
        in_specs=[
            pl.BlockSpec(memory_space=pltpu.SMEM),
            pl.BlockSpec((Q, SSD_XBC_W), xm_map),
            pl.BlockSpec((CONV_HALO, SSD_XBC_W), xp_map),
            pl.BlockSpec((CONV_HALO, SSD_XBC_W), xn_map),
            pl.BlockSpec((Q, DT_PAD_W), xm_map),
            pl.BlockSpec((Q, SSD_WIDTH), back_map),
            pl.BlockSpec((8, SSD_XBC_W), const),
            pl.BlockSpec((1, SSD_XBC_W), const),
            pl.BlockSpec((1, DT_PAD_W), const),
            pl.BlockSpec((1, DT_PAD_W), const),
            pl.BlockSpec((1, SSD_WIDTH), const),
        ],
        out_specs=pl.BlockSpec((Q, SSD_WIDTH), back_map),
        out_shape=jax.ShapeDtypeStruct((batch * seq, SSD_WIDTH), bf16),
        scratch_shapes=[
            pltpu.VMEM((Q + 2 * CONV_HALO, SSD_XBC_W), f32),
            pltpu.VMEM((seq, SSD_XBC_W), bf16),
            pltpu.VMEM((nc, Q, Q), f32),
            pltpu.VMEM((nc, Q, Q), f32),
            pltpu.VMEM((nc, Q, Q), f32),
            pltpu.VMEM((nc, SSD_STATE, SSD_WIDTH), bf16),
            pltpu.VMEM((nc, SSD_STATE, SSD_WIDTH), bf16),
            pltpu.VMEM((SSD_STATE, SSD_WIDTH), f32),
            pltpu.VMEM((SSD_STATE, SSD_WIDTH), f32),
        ],
        compiler_params=pltpu.CompilerParams(
            dimension_semantics=("arbitrary", "arbitrary", "arbitrary"),
            vmem_limit_bytes=VMEM_LIMIT),
        name="ssd",
    )(ssd_d, xbc, xbc, xbc, dt_raw, z, conv_w, conv_b, dt_bias_row, a_row, norm_w)


def _mem_kv_kernel(mem_ref, nw_ref, w_ref, knw_ref, k_ref, v_ref):
    x = mem_ref[...]
    ms = jnp.mean(x * x, axis=-1, keepdims=True)
    h = ((x * lax.rsqrt(ms + EPS)) * nw_ref[...]).astype(bf16)
    kv = _dot(h, w_ref[...])
    for c in range(XATT_HEADS):
        sl = slice(c * XATT_HEAD_DIM, (c + 1) * XATT_HEAD_DIM)
        t = kv[:, sl]
        k_ref[:, sl] = (t * lax.rsqrt(jnp.mean(t * t, axis=-1, keepdims=True) + EPS)
                        * knw_ref[...]).astype(bf16)
    v_ref[...] = kv[:, XATT_W:].astype(bf16)


def _mem_kv(mem2, mem_norm_w, w_kv, xk_norm_w, batch, mem_len):
    row = lambda b: (b, 0)
    const = lambda b: (0, 0)
    return pl.pallas_call(
        _mem_kv_kernel,
        grid=(batch,),
        in_specs=[
            pl.BlockSpec((mem_len, D_MODEL), row),
            pl.BlockSpec((1, D_MODEL), const),
            pl.BlockSpec((D_MODEL, 2 * XATT_W), const),
            pl.BlockSpec((1, XATT_HEAD_DIM), const),
        ],
        out_specs=[pl.BlockSpec((mem_len, XATT_W), row)] * 2,
        out_shape=[jax.ShapeDtypeStruct((batch * mem_len, XATT_W), bf16)] * 2,
        compiler_params=pltpu.CompilerParams(
            dimension_semantics=("arbitrary",), vmem_limit_bytes=VMEM_LIMIT),
        name="mem_kv",
    )(mem2, mem_norm_w, w_kv, xk_norm_w)


def _xattn_kernel(q_ref, k_ref, v_ref, o_ref):
    for hd in range(XATT_HEADS):
        sl = slice(hd * XATT_HEAD_DIM, (hd + 1) * XATT_HEAD_DIM)
        s = _dot_nt(q_ref[:, sl], k_ref[:, sl])
        m = jnp.max(s, axis=-1, keepdims=True)
        p = jnp.exp(s - m)
        den = jnp.sum(p, axis=-1, keepdims=True)
        o_ref[:, sl] = (_dot(p.astype(bf16), v_ref[:, sl]) / den).astype(bf16)


def _xattn(qx, km, vm, batch, seq, mem_len, tq):
    nq = seq // tq
    qmap = lambda b, i: (b * nq + i, 0)
    mmap = lambda b, i: (b, 0)
    return pl.pallas_call(
        _xattn_kernel,
        grid=(batch, nq),
        in_specs=[
            pl.BlockSpec((tq, XATT_W), qmap),
            pl.BlockSpec((mem_len, XATT_W), mmap),
            pl.BlockSpec((mem_len, XATT_W), mmap),
        ],
        out_specs=pl.BlockSpec((tq, XATT_W), qmap),
        out_shape=jax.ShapeDtypeStruct((batch * seq, XATT_W), bf16),
        compiler_params=pltpu.CompilerParams(
            dimension_semantics=("arbitrary", "arbitrary"), vmem_limit_bytes=VMEM_LIMIT),
        name="xattn",
    )(qx, km, vm)


FF_CHUNK = 1024


def _out_mlp_kernel(x_ref, a_ref, s_ref, c_ref, wo_ref, nw_ref, wu_ref, wd_ref, o_ref):
    mix = (_dot(a_ref[...], wo_ref[0:ATT_Q_W, :])
           + _dot(s_ref[...], wo_ref[ATT_Q_W:ATT_Q_W + SSD_WIDTH, :])
           + _dot(c_ref[...], wo_ref[ATT_Q_W + SSD_WIDTH:, :]))
    x1 = x_ref[...] + mix
    ms = jnp.mean(x1 * x1, axis=-1, keepdims=True)
    h = ((x1 * lax.rsqrt(ms + EPS)) * nw_ref[...]).astype(bf16)
    acc = x1
    for j in range(D_FF // FF_CHUNK):
        u = _dot(h, wu_ref[:, j * FF_CHUNK:(j + 1) * FF_CHUNK])
        r = jnp.maximum(u, 0.0)
        acc = acc + _dot((r * r).astype(bf16), wd_ref[j * FF_CHUNK:(j + 1) * FF_CHUNK, :])
    o_ref[...] = acc


def _out_mlp(x2, attn, ssd, xatt, w_out, norm_w, w_up, w_down, tm):
    tokens = x2.shape[0]
    row = lambda i: (i, 0)
    const = lambda i: (0, 0)
    resident = functools.partial(pl.BlockSpec, index_map=const, pipeline_mode=pl.Buffered(1))
    return pl.pallas_call(
        _out_mlp_kernel,
        grid=(tokens // tm,),
        in_specs=[
            pl.BlockSpec((tm, D_MODEL), row),
            pl.BlockSpec((tm, ATT_Q_W), row),
            pl.BlockSpec((tm, SSD_WIDTH), row),
            pl.BlockSpec((tm, XATT_W), row),
            resident((D_MIX, D_MODEL)),
            pl.BlockSpec((1, D_MODEL), const),
            resident((D_MODEL, D_FF)),
            resident((D_FF, D_MODEL)),
        ],
        out_specs=pl.BlockSpec((tm, D_MODEL), row),
        out_shape=jax.ShapeDtypeStruct((tokens, D_MODEL), f32),
        compiler_params=pltpu.CompilerParams(
            dimension_semantics=("arbitrary",), vmem_limit_bytes=VMEM_LIMIT),
        name="out_mlp",
    )(x2, attn, ssd, xatt, w_out, norm_w, w_up, w_down)


def _rope_tables(seq):
    half = ROPE_DIM // 2
    inv = ROPE_THETA ** (-jnp.arange(0, ROPE_DIM, 2, dtype=f32) / ROPE_DIM)
    ang = jnp.arange(seq, dtype=f32)[:, None] * inv[None, :]
    cos = jnp.cos(ang)
    sin = jnp.sin(ang)
    ones = jnp.ones((seq, ATT_HEAD_DIM - ROPE_DIM), f32)
    zeros_h = jnp.zeros((seq, half), f32)
    zeros_p = jnp.zeros((seq, ATT_HEAD_DIM - ROPE_DIM), f32)
    cos_head = jnp.concatenate([cos, cos, ones], axis=1)
    lo_head = jnp.concatenate([-sin, zeros_h, zeros_p], axis=1)
    hi_head = jnp.concatenate([zeros_h, sin, zeros_p], axis=1)
    rep = LANES // ATT_HEAD_DIM
    return (jnp.tile(cos_head, (1, rep)), jnp.tile(lo_head, (1, rep)),
            jnp.tile(hi_head, (1, rep)))


def _layer(x, mem, norm_mix_w, w_in, q_norm_w, k_norm_w, attn_sink, conv_w, conv_b,
           dt_bias_f, dt_bias_b, a_log_f, a_log_b, ssd_d, ssd_norm_w, mem_norm_w,
           w_mem_kv, xq_norm_w, xk_norm_w, w_out, norm_mlp_w, w_mlp_up, w_mlp_down, tables):
    batch, seq, _ = x.shape
    mem_len = mem.shape[1]
    tokens = batch * seq
    tm = 512

    s = np.cumsum([ATT_Q_W, ATT_KV_W, ATT_KV_W, SSD_WIDTH, SSD_XBC_W, SSD_DT_W, XATT_W])
    w_q, w_k, w_v = w_in[:, :s[0]], w_in[:, s[0]:s[1]], w_in[:, s[1]:s[2]]
    w_z, w_xbc, w_dt, w_qx = (w_in[:, s[2]:s[3]], w_in[:, s[3]:s[4]],
                              w_in[:, s[4]:s[5]], w_in[:, s[5]:s[6]])
    hd = ATT_HEAD_DIM
    dup = lambda w: jnp.concatenate(
        [w[:, g * hd:(g + 1) * hd] for g in range(ATT_KV_HEADS) for _ in range(2)], axis=1)
    w_in_p = jnp.concatenate(
        [w_q, dup(w_k), dup(w_v), w_z, w_xbc, w_qx, w_dt,
         jnp.zeros((D_MODEL, DT_PAD_W - SSD_DT_W), w_in.dtype)], axis=1).astype(bf16)

    x2 = x.reshape(tokens, D_MODEL)
    cos_t, sinlo_t, sinhi_t = tables
    q, kdup, vdup, z, xbc, qx, dt_raw = _in_proj(
        x2, norm_mix_w.reshape(1, D_MODEL), w_in_p,
        jnp.tile(q_norm_w, ATT_HEADS).reshape(1, ATT_Q_W),
        jnp.tile(k_norm_w, KDUP_W // hd).reshape(1, KDUP_W),
        xq_norm_w.reshape(1, XATT_HEAD_DIM), cos_t, sinlo_t, sinhi_t, seq, tm)

    attn = _attention(attn_sink.astype(f32), q, kdup, vdup, batch, seq)

    pad = jnp.zeros((DT_PAD_W - SSD_DT_W,), f32)
    dt_bias_row = jnp.concatenate([dt_bias_f, dt_bias_b, pad]).reshape(1, DT_PAD_W)
    a_row = jnp.concatenate([-jnp.exp(a_log_f), -jnp.exp(a_log_b), pad]).reshape(1, DT_PAD_W)
    conv_w8 = jnp.concatenate([conv_w, jnp.zeros((8 - SSD_CONV, SSD_XBC_W), f32)], axis=0)
    ssd = _ssd(ssd_d.astype(f32), xbc, dt_raw, z, conv_w8, conv_b.reshape(1, SSD_XBC_W),
               dt_bias_row, a_row, ssd_norm_w.reshape(1, SSD_WIDTH), batch, seq)

    km, vm = _mem_kv(mem.reshape(batch * mem_len, D_MODEL), mem_norm_w.reshape(1, D_MODEL),
                     w_mem_kv.astype(bf16), xk_norm_w.reshape(1, XATT_HEAD_DIM), batch, mem_len)
    xatt = _xattn(qx, km, vm, batch, seq, mem_len, tm)

    out = _out_mlp(x2, attn, ssd, xatt, w_out.astype(bf16), norm_mlp_w.reshape(1, D_MODEL),
                   w_mlp_up.astype(bf16), w_mlp_down.astype(bf16), tm)
    return out.reshape(batch, seq, D_MODEL)


def kernel(x, mem, norm_mix_w, w_in, q_norm_w, k_norm_w, attn_sink, conv_w, conv_b,
           dt_bias_f, dt_bias_b, a_log_f, a_log_b, ssd_d, ssd_norm_w, mem_norm_w,
           w_mem_kv, xq_norm_w, xk_norm_w, w_out, norm_mlp_w, w_mlp_up, w_mlp_down):
    depth = w_in.shape[0]
    tables = _rope_tables(x.shape[1])
    for i in range(depth):
        x = _layer(x, mem, norm_mix_w[i], w_in[i], q_norm_w[i], k_norm_w[i], attn_sink[i],
                   conv_w[i], conv_b[i], dt_bias_f[i], dt_bias_b[i], a_log_f[i], a_log_b[i],
                   ssd_d[i], ssd_norm_w[i], mem_norm_w[i], w_mem_kv[i], xq_norm_w[i],
                   xk_norm_w[i], w_out[i], norm_mlp_w[i], w_mlp_up[i], w_mlp_down[i], tables)
    return x
```

```python
import functools

import numpy as np
import jax
import jax.numpy as jnp
from jax import lax
from jax.experimental import pallas as pl
from jax.experimental.pallas import tpu as pltpu

D_MODEL = 1024
EPS = 1e-6

ATT_HEADS = 8
ATT_KV_HEADS = 2
ATT_HEAD_DIM = 64
ATT_Q_W = ATT_HEADS * ATT_HEAD_DIM
ATT_KV_W = ATT_KV_HEADS * ATT_HEAD_DIM
WINDOW = 128
ATT_BLOCK = 128
ROPE_THETA = 500000.0
ROPE_DIM = ATT_HEAD_DIM // 4

SSD_HEADS = 16
SSD_HEAD_DIM = 64
SSD_WIDTH = SSD_HEADS * SSD_HEAD_DIM
SSD_GROUPS = 2
SSD_STATE = 128
SSD_CONV = 5
SSD_CHUNK = 128
SSD_XBC_W = SSD_WIDTH + 2 * SSD_GROUPS * SSD_STATE
SSD_DT_W = 2 * SSD_HEADS

XATT_HEADS = 4
XATT_HEAD_DIM = 128
XATT_W = XATT_HEADS * XATT_HEAD_DIM

D_FF = 4 * D_MODEL
D_MIX = 2 * D_MODEL

LANES = 128
BF16_SUBLANES = 16
VMEM_LIMIT = 56 * 1024 * 1024

KDUP_W = 2 * ATT_KV_W
DT_PAD_W = LANES
C_Q = 0
C_K = C_Q + ATT_Q_W
C_V = C_K + KDUP_W
C_Z = C_V + KDUP_W
C_XBC = C_Z + SSD_WIDTH
C_QX = C_XBC + SSD_XBC_W
C_DT = C_QX + XATT_W
D_IN_P = C_DT + DT_PAD_W

CONV_HALO = BF16_SUBLANES
C_B = SSD_WIDTH
C_C = SSD_WIDTH + SSD_GROUPS * SSD_STATE

NEG_BIG = -1e30

bf16 = jnp.bfloat16
f32 = jnp.float32


def _dot(a, b):
    return jnp.dot(a, b, preferred_element_type=f32)


def _dot_nt(a, b):
    return lax.dot_general(a, b, (((1,), (1,)), ((), ())), preferred_element_type=f32)


def _split3(x):
    x1 = x.astype(bf16)
    r1 = x - x1.astype(f32)
    x2 = r1.astype(bf16)
    r2 = r1 - x2.astype(f32)
    return x1, x2, r2.astype(bf16)


def _dot3(x, m01):
    x1, x2, x3 = _split3(x)
    return _dot(x1, m01) + _dot(x2, m01) + _dot(x3, m01)


def _iota(shape, dim):
    return lax.broadcasted_iota(jnp.int32, shape, dim)


def _segment_ones(width, seg):
    r = _iota((width, width), 0) // seg
    c = _iota((width, width), 1) // seg
    return jnp.where(r == c, 1.0, 0.0).astype(bf16)


def _head_mean_sq(t, seg):
    sq = t * t
    hi = sq.astype(bf16)
    lo = (sq - hi.astype(f32)).astype(bf16)
    ones = _segment_ones(t.shape[1], seg)
    return (_dot(hi, ones) + _dot(lo, ones)) * (1.0 / seg)


def _rope(t, cos, sin_lo, sin_hi):
    half = ROPE_DIM // 2
    return (t * cos + pltpu.roll(t, LANES - half, 1) * sin_lo
            + pltpu.roll(t, half, 1) * sin_hi)


def _rms_rows(x, w):
    ms = jnp.mean(x * x, axis=-1, keepdims=True)
    return (x * lax.rsqrt(ms + EPS)) * w


def _in_proj_body(pos_blocks, x_ref, xp_ref, xn_ref, nw_ref, w_ref, qnw_ref, knw_ref,
                  xqnw_ref, cos_ref, sinlo_ref, sinhi_ref, cw_ref, cb_ref, dtb_ref, arow_ref,
                  q_ref, k_ref, v_ref, z_ref, xc_ref, qx_ref, rows_ref, cols_ref, dtT_ref,
                  conv_s):
    i = pl.program_id(0)
    tm = x_ref.shape[0]
    si = i % pos_blocks
    nw = nw_ref[...]
    h = _rms_rows(x_ref[...], nw).astype(bf16)

    cos = cos_ref[...]
    sin_lo = sinlo_ref[...]
    sin_hi = sinhi_ref[...]

    pq = _dot(h, w_ref[:, C_Q:C_Q + ATT_Q_W])
    qn = pq * lax.rsqrt(_head_mean_sq(pq, ATT_HEAD_DIM) + EPS) * qnw_ref[...]
    for c in range(ATT_Q_W // LANES):
        sl = slice(c * LANES, (c + 1) * LANES)
        q_ref[:, sl] = (_rope(qn[:, sl], cos, sin_lo, sin_hi)
                        * (ATT_HEAD_DIM ** -0.5)).astype(bf16)

    pk = _dot(h, w_ref[:, C_K:C_K + KDUP_W])
    kn = pk * lax.rsqrt(_head_mean_sq(pk, ATT_HEAD_DIM) + EPS) * knw_ref[...]
    for c in range(KDUP_W // LANES):
        sl = slice(c * LANES, (c + 1) * LANES)
        k_ref[:, sl] = _rope(kn[:, sl], cos, sin_lo, sin_hi).astype(bf16)

    v_ref[...] = _dot(h, w_ref[:, C_V:C_V + KDUP_W]).T.astype(bf16)
    z_ref[...] = _dot(h, w_ref[:, C_Z:C_Z + SSD_WIDTH]).astype(bf16)

    pqx = _dot(h, w_ref[:, C_QX:C_QX + XATT_W])
    for c in range(XATT_HEADS):
        sl = slice(c * XATT_HEAD_DIM, (c + 1) * XATT_HEAD_DIM)
        t = pqx[:, sl]
        tn = t * lax.rsqrt(jnp.mean(t * t, axis=-1, keepdims=True) + EPS) * xqnw_ref[...]
        qx_ref[:, sl] = (tn * (XATT_HEAD_DIM ** -0.5)).astype(bf16)

    hp = _rms_rows(xp_ref[...], nw)
    hn = _rms_rows(xn_ref[...], nw)
    hp = jnp.where(si > 0, hp, jnp.zeros_like(hp)).astype(bf16)
    hn = jnp.where(si < pos_blocks - 1, hn, jnp.zeros_like(hn)).astype(bf16)
    h_ext = jnp.concatenate([hp, h, hn], axis=0)
    conv_s[...] = _dot(h_ext, w_ref[:, C_XBC:C_XBC + SSD_XBC_W])
    acc = jnp.broadcast_to(cb_ref[...], (tm, SSD_XBC_W))
    for k in range(SSD_CONV):
        off = CONV_HALO - SSD_CONV // 2 + k
        acc = acc + conv_s[off:off + tm, :] * cw_ref[k:k + 1, :]
    xc_ref[...] = (acc / (1.0 + jnp.exp(-acc))).astype(bf16)

    Q = SSD_CHUNK
    dt = _dot(h, w_ref[:, C_DT:C_DT + DT_PAD_W]) + dtb_ref[...]
    dt = jnp.maximum(dt, 0.0) + jnp.log1p(jnp.exp(-jnp.abs(dt)))
    a = dt * arow_ref[...]
    li = _iota((Q, Q), 0)
    lj = _iota((Q, Q), 1)
    incl_le = jnp.where(li <= lj, 1.0, 0.0).astype(bf16)
    incl_ge = jnp.where(li >= lj, 1.0, 0.0).astype(bf16)
    pad_rows = jnp.zeros((Q - 2 * SSD_HEADS, Q), f32)
    for ch in range(tm // Q):
        r = slice(ch * Q, (ch + 1) * Q)
        aT = a[r].T
        pre = _dot3(aT[0:SSD_HEADS], incl_le)
        suf = _dot3(aT[SSD_HEADS:2 * SSD_HEADS], incl_ge)
        rows = jnp.concatenate([pre, suf, pad_rows], axis=0)
        rows_ref[r, :] = rows
        cols_ref[r, :] = rows.T
        dtT_ref[r, :] = dt[r].T


def _in_proj(x2, norm_w, w_in_p, qnw, knw, xqnw, cos_t, sinlo_t, sinhi_t,
             conv_w8, conv_b, dt_bias_row, a_row, seq, tm):
    tokens = x2.shape[0]
    pos_blocks = seq // tm
    hpt = tm // CONV_HALO
    n_halo = tokens // CONV_HALO
    row = lambda i: (i, 0)
    const = lambda i: (0, 0)
    pos = lambda i: (i % pos_blocks, 0)
    prev = lambda i: (jnp.maximum(i * hpt - 1, 0), 0)
    nxt = lambda i: (jnp.minimum((i + 1) * hpt, n_halo - 1), 0)
    out_widths = [ATT_Q_W, KDUP_W, KDUP_W, SSD_WIDTH, SSD_XBC_W, XATT_W, LANES, LANES, LANES]
    out_dtypes = [bf16, bf16, bf16, bf16, bf16, bf16, f32, f32, f32]
    out_specs = [pl.BlockSpec((tm, w), row) for w in out_widths]
    out_shape = [jax.ShapeDtypeStruct((tokens, w), d) for w, d in zip(out_widths, out_dtypes)]
    out_specs[2] = pl.BlockSpec((KDUP_W, tm), lambda i: (i // pos_blocks, i % pos_blocks))
    out_shape[2] = jax.ShapeDtypeStruct((tokens // seq * KDUP_W, seq), bf16)
    return pl.pallas_call(
        functools.partial(_in_proj_body, pos_blocks),
        grid=(tokens // tm,),
        in_specs=[
            pl.BlockSpec((tm, D_MODEL), row),
            pl.BlockSpec((CONV_HALO, D_MODEL), prev),
            pl.BlockSpec((CONV_HALO, D_MODEL), nxt),
            pl.BlockSpec((1, D_MODEL), const),
            pl.BlockSpec((D_MODEL, D_IN_P), const, pipeline_mode=pl.Buffered(1)),
            pl.BlockSpec((1, ATT_Q_W), const),
            pl.BlockSpec((1, KDUP_W), const),
            pl.BlockSpec((1, XATT_HEAD_DIM), const),
            pl.BlockSpec((tm, LANES), pos),
            pl.BlockSpec((tm, LANES), pos),
            pl.BlockSpec((tm, LANES), pos),
            pl.BlockSpec((8, SSD_XBC_W), const),
            pl.BlockSpec((1, SSD_XBC_W), const),
            pl.BlockSpec((1, DT_PAD_W), const),
            pl.BlockSpec((1, DT_PAD_W), const),
        ],
        out_specs=out_specs,
        out_shape=out_shape,
        scratch_shapes=[pltpu.VMEM((tm + 2 * CONV_HALO, SSD_XBC_W), f32)],
        compiler_params=pltpu.CompilerParams(
            dimension_semantics=("arbitrary",), vmem_limit_bytes=VMEM_LIMIT),
        name="in_proj",
    )(x2, x2, x2, norm_w, w_in_p, qnw, knw, xqnw, cos_t, sinlo_t, sinhi_t,
      conv_w8, conv_b, dt_bias_row, a_row)


def _attn_kernel(sink_ref, q_ref, kp_ref, kc_ref, kn_ref, vp_ref, vc_ref, vn_ref, o_ref):
    n = pl.program_id(1)
    nb = pl.num_programs(1)
    blk = ATT_BLOCK
    heads_per_kv = ATT_HEADS // ATT_KV_HEADS
    cols = heads_per_kv * blk

    key = _iota((3 * blk, cols), 0)
    qi = _iota((3 * blk, cols), 1) & (blk - 1)
    valid = (key >= qi) & (key <= qi + 2 * WINDOW)
    valid = valid & ((key >= blk) | (n > 0)) & ((key < 2 * blk) | (n < nb - 1))
    low_lanes = _iota((blk, LANES), 1) < ATT_HEAD_DIM
    head_of_col = _iota((1, cols), 1) // blk

    for g in range(ATT_KV_HEADS):
        gs = slice(g * LANES, (g + 1) * LANES)
        kcat = jnp.concatenate([kp_ref[:, gs], kc_ref[:, gs], kn_ref[:, gs]], axis=0)
        vt = jnp.concatenate([vp_ref[gs, :], vc_ref[gs, :], vn_ref[gs, :]], axis=1)
        qs = []
        sink = jnp.zeros((1, cols), f32)
        for j in range(heads_per_kv):
            hd = g * heads_per_kv + j
            pair = q_ref[:, (hd // 2) * LANES:(hd // 2 + 1) * LANES]
            keep = low_lanes if hd % 2 == 0 else jnp.logical_not(low_lanes)
            qs.append(jnp.where(keep, pair, jnp.zeros_like(pair)))
            sink = jnp.where(head_of_col == j, sink_ref[hd], sink)
        qg = jnp.concatenate(qs, axis=0)
        st = _dot_nt(kcat, qg)
        st = jnp.where(valid, st, NEG_BIG)
        m = jnp.maximum(jnp.max(st, axis=0, keepdims=True), sink)
        p = jnp.exp(st - m)
        den = jnp.sum(p, axis=0, keepdims=True) + jnp.exp(sink - m)
        ot = _dot(vt, p.astype(bf16)) / den
        half = LANES // 2
        for jp in range(heads_per_kv // 2):
            even = ot[0:half, (2 * jp) * blk:(2 * jp + 1) * blk]
            odd = ot[half:, (2 * jp + 1) * blk:(2 * jp + 2) * blk]
            pr = g * (heads_per_kv // 2) + jp
            o_ref[:, pr * LANES:(pr + 1) * LANES] = (
                jnp.concatenate([even, odd], axis=0).T.astype(bf16))


def _attention(sink, q, kdup, vt, batch, seq):
    nb = seq // ATT_BLOCK
    blk = ATT_BLOCK
    cur = lambda b, n: (b * nb + n, 0)
    prev = lambda b, n: (b * nb + jnp.maximum(n - 1, 0), 0)
    nxt = lambda b, n: (b * nb + jnp.minimum(n + 1, nb - 1), 0)
    vcur = lambda b, n: (b, n)
    vprev = lambda b, n: (b, jnp.maximum(n - 1, 0))
    vnxt = lambda b, n: (b, jnp.minimum(n + 1, nb - 1))
    return pl.pallas_call(
        _attn_kernel,
        grid=(batch, nb),
        in_specs=[
            pl.BlockSpec(memory_space=pltpu.SMEM),
            pl.BlockSpec((blk, ATT_Q_W), cur),
            pl.BlockSpec((blk, KDUP_W), prev),
            pl.BlockSpec((blk, KDUP_W), cur),
            pl.BlockSpec((blk, KDUP_W), nxt),
            pl.BlockSpec((KDUP_W, blk), vprev),
            pl.BlockSpec((KDUP_W, blk), vcur),
            pl.BlockSpec((KDUP_W, blk), vnxt),
        ],
        out_specs=pl.BlockSpec((blk, ATT_Q_W), cur),
        out_shape=jax.ShapeDtypeStruct((batch * seq, ATT_Q_W), bf16),
        compiler_params=pltpu.CompilerParams(
            dimension_semantics=("arbitrary", "arbitrary"), vmem_limit_bytes=VMEM_LIMIT),
        name="attention",
    )(sink, q, kdup, kdup, kdup, vt, vt, vt)


def _ssd_kernel(d_ref, xc_ref, rows_ref, cols_ref, dtT_ref, z_ref, nw_ref, y_ref,
                hf_s, sb_s, hf_run, hb_run, y_s):
    ph = pl.program_id(1)
    c = pl.program_id(2)
    nc = pl.num_programs(2)
    Q = SSD_CHUNK
    heads_per_group = SSD_HEADS // SSD_GROUPS
    low_lanes = _iota((Q, LANES), 1) < SSD_HEAD_DIM
    low_row = _iota((1, LANES), 1) < SSD_HEAD_DIM
    li = _iota((Q, Q), 0)
    si = _iota((Q, Q), 1)
    rows = rows_ref[...]
    dtT = dtT_ref[...]
    pre = rows[0:SSD_HEADS]
    suf = rows[SSD_HEADS:2 * SSD_HEADS]

    @pl.when(ph == 0)
    def _phase0():
        @pl.when(c == 0)
        def _():
            hf_run[...] = jnp.zeros_like(hf_run)

        pre_end = jnp.broadcast_to(pre[:, Q - 1:Q], (SSD_HEADS, Q))
        suf_end = jnp.broadcast_to(suf[:, 0:1], (SSD_HEADS, Q))
        wf = jnp.exp(pre_end - pre) * dtT[0:SSD_HEADS]
        wb = jnp.exp(suf_end - suf) * dtT[SSD_HEADS:2 * SSD_HEADS]
        decf = jnp.exp(pre_end)

        for g in range(SSD_GROUPS):
            bT = xc_ref[:, C_B + g * SSD_STATE:C_B + (g + 1) * SSD_STATE].astype(f32).T
            for m in range(heads_per_group // 2):
                h0 = g * heads_per_group + 2 * m
                h1 = h0 + 1
                lhs = jnp.concatenate(
                    [bT * wf[h0:h0 + 1], bT * wb[h0:h0 + 1],
                     bT * wf[h1:h1 + 1], bT * wb[h1:h1 + 1]], axis=0).astype(bf16)
                cs = slice(h0 * SSD_HEAD_DIM, h0 * SSD_HEAD_DIM + LANES)
                res = _dot(lhs, xc_ref[:, cs])
                s_f = jnp.where(low_lanes, res[0:Q], res[2 * Q:3 * Q])
                s_b = jnp.where(low_lanes, res[Q:2 * Q], res[3 * Q:4 * Q])
                dec = jnp.where(low_row, decf[h0:h0 + 1], decf[h1:h1 + 1])
                prev = hf_run[:, cs]
                hf_s[c, :, cs] = prev.astype(bf16)
                hf_run[:, cs] = dec * prev + s_f
                sb_s[c, :, cs] = s_b.astype(bf16)

    @pl.when(ph == 1)
    def _phase1():
        @pl.when(c == 0)
        def _():
            hb_run[...] = jnp.zeros_like(hb_run)

        cc = nc - 1 - c
        cols = cols_ref[...]
        decb = jnp.exp(jnp.broadcast_to(suf[:, 0:1], (SSD_HEADS, Q)))
        lower = li >= si
        upper = si >= li
        eye = li == si

        for g in range(SSD_GROUPS):
            bg = xc_ref[:, C_B + g * SSD_STATE:C_B + (g + 1) * SSD_STATE]
            cg = xc_ref[:, C_C + g * SSD_STATE:C_C + (g + 1) * SSD_STATE]
            cbm = _dot_nt(cg, bg)
            cgf = cg.astype(f32)
            for m in range(heads_per_group // 2):
                h0 = g * heads_per_group + 2 * m
                parts = []
                for hd in (h0, h0 + 1):
                    hb = SSD_HEADS + hd
                    colf = jnp.broadcast_to(cols[:, hd:hd + 1], (Q, Q))
                    colb = jnp.broadcast_to(cols[:, hb:hb + 1], (Q, Q))
                    gf = jnp.exp(jnp.where(lower, colf - rows[hd:hd + 1], NEG_BIG)) * dtT[hd:hd + 1]
                    gb = jnp.exp(jnp.where(upper, colb - rows[hb:hb + 1], NEG_BIG)) * dtT[hb:hb + 1]
                    gm = cbm * (gf + gb) + jnp.where(eye, d_ref[hd], 0.0)
                    parts.append(jnp.concatenate(
                        [gm, cgf * jnp.exp(colf), cgf * jnp.exp(colb)], axis=1).astype(bf16))
                lhs = jnp.concatenate(parts, axis=0)
                cs = slice(h0 * SSD_HEAD_DIM, h0 * SSD_HEAD_DIM + LANES)
                hb_prev = hb_run[:, cs]
                rhs = jnp.concatenate(
                    [xc_ref[:, cs], hf_s[cc, :, cs], hb_prev.astype(bf16)], axis=0)
                res = _dot(lhs, rhs)
                y_s[:, cs] = jnp.where(low_lanes, res[0:Q], res[Q:2 * Q])
                dec = jnp.where(low_row, decb[h0:h0 + 1], decb[h0 + 1:h0 + 2])
                hb_run[:, cs] = dec * hb_prev + sb_s[cc, :, cs].astype(f32)

        zf = z_ref[...].astype(f32)
        y = y_s[...] * (zf / (1.0 + jnp.exp(-zf)))
        gw = SSD_WIDTH // SSD_GROUPS
        for g in range(SSD_GROUPS):
            yg = y[:, g * gw:(g + 1) * gw]
            ms = jnp.mean(yg * yg, axis=-1, keepdims=True)
            y_ref[:, g * gw:(g + 1) * gw] = (
                yg * lax.rsqrt(ms + EPS) * nw_ref[:, g * gw:(g + 1) * gw]).astype(bf16)


def _ssd(ssd_d, xc, rows, cols, dtT, z, norm_w, batch, seq):
    Q = SSD_CHUNK
    nc = seq // Q
    last = nc - 1

    def both_map(b, ph, c):
        return (b * nc + c * (1 - ph) + (last - c) * ph, 0)

    def back_map(b, ph, c):
        return (b * nc + last - c * ph, 0)

    const = lambda b, ph, c: (0, 0)
    return pl.pallas_call(
        _ssd_kernel,
        grid=(batch, 2, nc),
        in_specs=[
            pl.BlockSpec(memory_space=pltpu.SMEM),
            pl.BlockSpec((Q, SSD_XBC_W), both_map),
            pl.BlockSpec((Q, LANES), both_map),
            pl.BlockSpec((Q, LANES), back_map),
            pl.BlockSpec((Q, LANES), both_map),
            pl.BlockSpec((Q, SSD_WIDTH), back_map),
            pl.BlockSpec((1, SSD_WIDTH), const),
        ],
        out_specs=pl.BlockSpec((Q, SSD_WIDTH), back_map),
        out_shape=jax.ShapeDtypeStruct((batch * seq, SSD_WIDTH), bf16),
        scratch_shapes=[
            pltpu.VMEM((nc, SSD_STATE, SSD_WIDTH), bf16),
            pltpu.VMEM((nc, SSD_STATE, SSD_WIDTH), bf16),
            pltpu.VMEM((SSD_STATE, SSD_WIDTH), f32),
            pltpu.VMEM((SSD_STATE, SSD_WIDTH), f32),
            pltpu.VMEM((Q, SSD_WIDTH), f32),
        ],
        compiler_params=pltpu.CompilerParams(
            dimension_semantics=("arbitrary", "arbitrary", "arbitrary"),
            vmem_limit_bytes=VMEM_LIMIT),
        name="ssd",
    )(ssd_d, xc, rows, cols, dtT, z, norm_w)


def _mem_kv_kernel(mem_ref, nw_ref, w_ref, knw_ref, k_ref, v_ref):
    h = _rms_rows(mem_ref[...], nw_ref[...]).astype(bf16)
    kv = _dot(h, w_ref[...])
    for c in range(XATT_HEADS):
        sl = slice(c * XATT_HEAD_DIM, (c + 1) * XATT_HEAD_DIM)
        t = kv[:, sl]
        k_ref[:, sl] = (t * lax.rsqrt(jnp.mean(t * t, axis=-1, keepdims=True) + EPS)
                        * knw_ref[...]).astype(bf16)
    v_ref[...] = kv[:, XATT_W:].astype(bf16)


def _mem_kv(mem2, mem_norm_w, w_kv, xk_norm_w, batch, mem_len):
    row = lambda b: (b, 0)
    const = lambda b: (0, 0)
    return pl.pallas_call(
        _mem_kv_kernel,
        grid=(batch,),
        in_specs=[
            pl.BlockSpec((mem_len, D_MODEL), row),
            pl.BlockSpec((1, D_MODEL), const),
            pl.BlockSpec((D_MODEL, 2 * XATT_W), const),
            pl.BlockSpec((1, XATT_HEAD_DIM), const),
        ],
        out_specs=[pl.BlockSpec((mem_len, XATT_W), row)] * 2,
        out_shape=[jax.ShapeDtypeStruct((batch * mem_len, XATT_W), bf16)] * 2,
        compiler_params=pltpu.CompilerParams(
            dimension_semantics=("arbitrary",), vmem_limit_bytes=VMEM_LIMIT),
        name="mem_kv",
    )(mem2, mem_norm_w, w_kv, xk_norm_w)


def _xattn_kernel(q_ref, k_ref, v_ref, o_ref):
    for hd in range(XATT_HEADS):
        sl = slice(hd * XATT_HEAD_DIM, (hd + 1) * XATT_HEAD_DIM)
        s = _dot_nt(q_ref[:, sl], k_ref[:, sl])
        m = jnp.max(s, axis=-1, keepdims=True)
        p = jnp.exp(s - m)
        den = jnp.sum(p, axis=-1, keepdims=True)
        o_ref[:, sl] = (_dot(p.astype(bf16), v_ref[:, sl]) / den).astype(bf16)


def _xattn(qx, km, vm, batch, seq, mem_len, tq):
    nq = seq // tq
    qmap = lambda b, i: (b * nq + i, 0)
    mmap = lambda b, i: (b, 0)
    return pl.pallas_call(
        _xattn_kernel,
        grid=(batch, nq),
        in_specs=[
            pl.BlockSpec((tq, XATT_W), qmap),
            pl.BlockSpec((mem_len, XATT_W), mmap),
            pl.BlockSpec((mem_len, XATT_W), mmap),
        ],
        out_specs=pl.BlockSpec((tq, XATT_W), qmap),
        out_shape=jax.ShapeDtypeStruct((batch * seq, XATT_W), bf16),
        compiler_params=pltpu.CompilerParams(
            dimension_semantics=("arbitrary", "arbitrary"), vmem_limit_bytes=VMEM_LIMIT),
        name="xattn",
    )(qx, km, vm)


FF_CHUNK = 1024


def _out_mlp_kernel(x_ref, a_ref, s_ref, c_ref, wo_ref, nw_ref, wu_ref, wd_ref, o_ref):
    mix = (_dot(a_ref[...], wo_ref[0:ATT_Q_W, :])
           + _dot(s_ref[...], wo_ref[ATT_Q_W:ATT_Q_W + SSD_WIDTH, :])
           + _dot(c_ref[...], wo_ref[ATT_Q_W + SSD_WIDTH:, :]))
    x1 = x_ref[...] + mix
    h = _rms_rows(x1, nw_ref[...]).astype(bf16)
    acc = x1
    for j in range(D_FF // FF_CHUNK):
        u = _dot(h, wu_ref[:, j * FF_CHUNK:(j + 1) * FF_CHUNK])
        r = jnp.maximum(u, 0.0)
        acc = acc + _dot((r * r).astype(bf16), wd_ref[j * FF_CHUNK:(j + 1) * FF_CHUNK, :])
    o_ref[...] = acc


def _out_mlp(x2, attn, ssd, xatt, w_out, norm_w, w_up, w_down, tm):
    tokens = x2.shape[0]
    row = lambda i: (i, 0)
    const = lambda i: (0, 0)
    resident = functools.partial(pl.BlockSpec, index_map=const, pipeline_mode=pl.Buffered(1))
    return pl.pallas_call(
        _out_mlp_kernel,
        grid=(tokens // tm,),
        in_specs=[
            pl.BlockSpec((tm, D_MODEL), row),
            pl.BlockSpec((tm, ATT_Q_W), row),
            pl.BlockSpec((tm, SSD_WIDTH), row),
            pl.BlockSpec((tm, XATT_W), row),
            resident((D_MIX, D_MODEL)),
            pl.BlockSpec((1, D_MODEL), const),
            resident((D_MODEL, D_FF)),
            resident((D_FF, D_MODEL)),
        ],
        out_specs=pl.BlockSpec((tm, D_MODEL), row),
        out_shape=jax.ShapeDtypeStruct((tokens, D_MODEL), f32),
        compiler_params=pltpu.CompilerParams(
            dimension_semantics=("arbitrary",), vmem_limit_bytes=VMEM_LIMIT),
        name="out_mlp",
    )(x2, attn, ssd, xatt, w_out, norm_w, w_up, w_down)


def _rope_tables(seq):
    half = ROPE_DIM // 2
    inv = ROPE_THETA ** (-jnp.arange(0, ROPE_DIM, 2, dtype=f32) / ROPE_DIM)
    ang = jnp.arange(seq, dtype=f32)[:, None] * inv[None, :]
    cos = jnp.cos(ang)
    sin = jnp.sin(ang)
    ones = jnp.ones((seq, ATT_HEAD_DIM - ROPE_DIM), f32)
    zeros_h = jnp.zeros((seq, half), f32)
    zeros_p = jnp.zeros((seq, ATT_HEAD_DIM - ROPE_DIM), f32)
    cos_head = jnp.concatenate([cos, cos, ones], axis=1)
    lo_head = jnp.concatenate([-sin, zeros_h, zeros_p], axis=1)
    hi_head = jnp.concatenate([zeros_h, sin, zeros_p], axis=1)
    rep = LANES // ATT_HEAD_DIM
    return (jnp.tile(cos_head, (1, rep)), jnp.tile(lo_head, (1, rep)),
            jnp.tile(hi_head, (1, rep)))


def _layer(x, mem, norm_mix_w, w_in, q_norm_w, k_norm_w, attn_sink, conv_w, conv_b,
           dt_bias_f, dt_bias_b, a_log_f, a_log_b, ssd_d, ssd_norm_w, mem_norm_w,
           w_mem_kv, xq_norm_w, xk_norm_w, w_out, norm_mlp_w, w_mlp_up, w_mlp_down, tables):
    batch, seq, _ = x.shape
    mem_len = mem.shape[1]
    tokens = batch * seq
    tm = 512

    s = np.cumsum([ATT_Q_W, ATT_KV_W, ATT_KV_W, SSD_WIDTH, SSD_XBC_W, SSD_DT_W, XATT_W])
    w_q, w_k, w_v = w_in[:, :s[0]], w_in[:, s[0]:s[1]], w_in[:, s[1]:s[2]]
    w_z, w_xbc, w_dt, w_qx = (w_in[:, s[2]:s[3]], w_in[:, s[3]:s[4]],
                              w_in[:, s[4]:s[5]], w_in[:, s[5]:s[6]])
    hd = ATT_HEAD_DIM
    dup = lambda w: jnp.concatenate(
        [w[:, g * hd:(g + 1) * hd] for g in range(ATT_KV_HEADS) for _ in range(2)], axis=1)
    w_in_p = jnp.concatenate(
        [w_q, dup(w_k), dup(w_v), w_z, w_xbc, w_qx, w_dt,
         jnp.zeros((D_MODEL, DT_PAD_W - SSD_DT_W), w_in.dtype)], axis=1).astype(bf16)

    pad = jnp.zeros((DT_PAD_W - SSD_DT_W,), f32)
    dt_bias_row = jnp.concatenate([dt_bias_f, dt_bias_b, pad]).reshape(1, DT_PAD_W)
    a_row = jnp.concatenate([-jnp.exp(a_log_f), -jnp.exp(a_log_b), pad]).reshape(1, DT_PAD_W)
    conv_w8 = jnp.concatenate([conv_w, jnp.zeros((8 - SSD_CONV, SSD_XBC_W), f32)], axis=0)

    x2 = x.reshape(tokens, D_MODEL)
    cos_t, sinlo_t, sinhi_t = tables
    q, kdup, vt, z, xc, qx, rows, cols, dtT = _in_proj(
        x2, norm_mix_w.reshape(1, D_MODEL), w_in_p,
        jnp.tile(q_norm_w, ATT_HEADS).reshape(1, ATT_Q_W),
        jnp.tile(k_norm_w, KDUP_W // hd).reshape(1, KDUP_W),
        xq_norm_w.reshape(1, XATT_HEAD_DIM), cos_t, sinlo_t, sinhi_t,
        conv_w8, conv_b.reshape(1, SSD_XBC_W), dt_bias_row, a_row, seq, tm)

    attn = _attention(attn_sink.astype(f32), q, kdup, vt, batch, seq)
    ssd = _ssd(ssd_d.astype(f32), xc, rows, cols, dtT, z,
               ssd_norm_w.reshape(1, SSD_WIDTH), batch, seq)

    km, vm = _mem_kv(mem.reshape(batch * mem_len, D_MODEL), mem_norm_w.reshape(1, D_MODEL),
                     w_mem_kv.astype(bf16), xk_norm_w.reshape(1, XATT_HEAD_DIM), batch, mem_len)
    xatt = _xattn(qx, km, vm, batch, seq, mem_len, tm)

    out = _out_mlp(x2, attn, ssd, xatt, w_out.astype(bf16), norm_mlp_w.reshape(1, D_MODEL),
                   w_mlp_up.astype(bf16), w_mlp_down.astype(bf16), tm)
    return out.reshape(batch, seq, D_MODEL)


def kernel(x, mem, norm_mix_w, w_in, q_norm_w, k_norm_w, attn_sink, conv_w, conv_b,
           dt_bias_f, dt_bias_b, a_log_f, a_log_b, ssd_d, ssd_norm_w, mem_norm_w,
           w_mem_kv, xq_norm_w, xk_norm_w, w_out, norm_mlp_w, w_mlp_up, w_mlp_down):
    depth = w_in.shape[0]
    tables = _rope_tables(x.shape[1])
    for i in range(depth):
        x = _layer(x, mem, norm_mix_w[i], w_in[i], q_norm_w[i], k_norm_w[i], attn_sink[i],
                   conv_w[i], conv_b[i], dt_bias_f[i], dt_bias_b[i], a_log_f[i], a_log_b[i],
                   ssd_d[i], ssd_norm_w[i], mem_norm_w[i], w_mem_kv[i], xq_norm_w[i],
                   xk_norm_w[i], w_out[i], norm_mlp_w[i], w_mlp_up[i], w_mlp_down[i], tables)
    return x
```

```python
import functools

import numpy as np
import jax
import jax.numpy as jnp
from jax import lax
from jax.experimental import pallas as pl
from jax.experimental.pallas import tpu as pltpu

D_MODEL = 1024
EPS = 1e-6

ATT_HEADS = 8
ATT_KV_HEADS = 2
ATT_HEAD_DIM = 64
ATT_Q_W = ATT_HEADS * ATT_HEAD_DIM
ATT_KV_W = ATT_KV_HEADS * ATT_HEAD_DIM
WINDOW = 128
ATT_BLOCK = 128
ATT_BLOCKS_PER_STEP = 2
ROPE_THETA = 500000.0
ROPE_DIM = ATT_HEAD_DIM // 4

SSD_HEADS = 16
SSD_HEAD_DIM = 64
SSD_WIDTH = SSD_HEADS * SSD_HEAD_DIM
SSD_GROUPS = 2
SSD_STATE = 128
SSD_CONV = 5
SSD_CHUNK = 128
SSD_XBC_W = SSD_WIDTH + 2 * SSD_GROUPS * SSD_STATE
SSD_DT_W = 2 * SSD_HEADS

XATT_HEADS = 4
XATT_HEAD_DIM = 128
XATT_W = XATT_HEADS * XATT_HEAD_DIM

D_FF = 4 * D_MODEL
D_MIX = 2 * D_MODEL

LANES = 128
BF16_SUBLANES = 16
VMEM_LIMIT = 56 * 1024 * 1024

KDUP_W = 2 * ATT_KV_W
DT_PAD_W = LANES
C_Q = 0
C_K = C_Q + ATT_Q_W
C_V = C_K + KDUP_W
C_Z = C_V + KDUP_W
C_XBC = C_Z + SSD_WIDTH
C_QX = C_XBC + SSD_XBC_W
C_DT = C_QX + XATT_W
D_IN_P = C_DT + DT_PAD_W

CONV_HALO = BF16_SUBLANES
C_B = SSD_WIDTH
C_C = SSD_WIDTH + SSD_GROUPS * SSD_STATE

NEG_BIG = -1e30
LOG2E = 1.4426950408889634

bf16 = jnp.bfloat16
f32 = jnp.float32


def _dot(a, b):
    return jnp.dot(a, b, preferred_element_type=f32)


def _dot_nt(a, b):
    return lax.dot_general(a, b, (((1,), (1,)), ((), ())), preferred_element_type=f32)


def _split3(x):
    x1 = x.astype(bf16)
    r1 = x - x1.astype(f32)
    x2 = r1.astype(bf16)
    r2 = r1 - x2.astype(f32)
    return x1, x2, r2.astype(bf16)


def _dot3(x, m01):
    x1, x2, x3 = _split3(x)
    return _dot(x1, m01) + _dot(x2, m01) + _dot(x3, m01)


def _iota(shape, dim):
    return lax.broadcasted_iota(jnp.int32, shape, dim)


def _segment_ones(width, seg):
    r = _iota((width, width), 0) // seg
    c = _iota((width, width), 1) // seg
    return jnp.where(r == c, 1.0, 0.0).astype(bf16)


def _head_mean_sq(t, seg):
    ones = _segment_ones(t.shape[1], seg)
    return _dot((t * t).astype(bf16), ones) * (1.0 / seg)


def _rope(t, cos, sin_lo, sin_hi):
    half = ROPE_DIM // 2
    return (t * cos + pltpu.roll(t, LANES - half, 1) * sin_lo
            + pltpu.roll(t, half, 1) * sin_hi)


def _rms_rows(x, w):
    ms = jnp.mean(x * x, axis=-1, keepdims=True)
    return (x * lax.rsqrt(ms + EPS)) * w


def _in_proj_body(pos_blocks, x_ref, xp_ref, xn_ref, nw_ref, w_ref, qnw_ref, knw_ref,
                  xqnw_ref, cos_ref, sinlo_ref, sinhi_ref, cw_ref, cb_ref, dtb_ref, arow_ref,
                  q_ref, k_ref, v_ref, z_ref, xc_ref, qx_ref,
                  rows_ref, rowsl_ref, cols_ref, dtT_ref, conv_s):
    i = pl.program_id(0)
    tm = x_ref.shape[0]
    si = i % pos_blocks
    nw = nw_ref[...]
    h = _rms_rows(x_ref[...], nw).astype(bf16)

    cos = cos_ref[...]
    sin_lo = sinlo_ref[...]
    sin_hi = sinhi_ref[...]

    pq = _dot(h, w_ref[:, C_Q:C_Q + ATT_Q_W])
    qn = pq * lax.rsqrt(_head_mean_sq(pq, ATT_HEAD_DIM) + EPS) * qnw_ref[...]
    for c in range(ATT_Q_W // LANES):
        sl = slice(c * LANES, (c + 1) * LANES)
        q_ref[:, sl] = (_rope(qn[:, sl], cos, sin_lo, sin_hi)
                        * (ATT_HEAD_DIM ** -0.5 * LOG2E)).astype(bf16)

    pk = _dot(h, w_ref[:, C_K:C_K + KDUP_W])
    kn = pk * lax.rsqrt(_head_mean_sq(pk, ATT_HEAD_DIM) + EPS) * knw_ref[...]
    for c in range(KDUP_W // LANES):
        sl = slice(c * LANES, (c + 1) * LANES)
        k_ref[:, sl] = _rope(kn[:, sl], cos, sin_lo, sin_hi).astype(bf16)

    v_ref[...] = _dot(h, w_ref[:, C_V:C_V + KDUP_W]).T.astype(bf16)
    z_ref[...] = _dot(h, w_ref[:, C_Z:C_Z + SSD_WIDTH]).astype(bf16)

    pqx = _dot(h, w_ref[:, C_QX:C_QX + XATT_W])
    for c in range(XATT_HEADS):
        sl = slice(c * XATT_HEAD_DIM, (c + 1) * XATT_HEAD_DIM)
        t = pqx[:, sl]
        tn = t * lax.rsqrt(jnp.mean(t * t, axis=-1, keepdims=True) + EPS) * xqnw_ref[...]
        qx_ref[:, sl] = (tn * (XATT_HEAD_DIM ** -0.5 * LOG2E)).astype(bf16)

    hp = _rms_rows(xp_ref[...], nw)
    hn = _rms_rows(xn_ref[...], nw)
    hp = jnp.where(si > 0, hp, jnp.zeros_like(hp)).astype(bf16)
    hn = jnp.where(si < pos_blocks - 1, hn, jnp.zeros_like(hn)).astype(bf16)
    h_ext = jnp.concatenate([hp, h, hn], axis=0)
    conv_s[...] = _dot(h_ext, w_ref[:, C_XBC:C_XBC + SSD_XBC_W])
    acc = jnp.broadcast_to(cb_ref[...], (tm, SSD_XBC_W))
    for k in range(SSD_CONV):
        off = CONV_HALO - SSD_CONV // 2 + k
        acc = acc + conv_s[off:off + tm, :] * cw_ref[k:k + 1, :]
    xc_ref[...] = (acc / (1.0 + jnp.exp2(acc * (-LOG2E)))).astype(bf16)

    Q = SSD_CHUNK
    dt = _dot(h, w_ref[:, C_DT:C_DT + DT_PAD_W]) + dtb_ref[...]
    dt = jnp.maximum(dt, 0.0) + jnp.log1p(jnp.exp(-jnp.abs(dt)))
    a = dt * arow_ref[...]
    li = _iota((Q, Q), 0)
    lj = _iota((Q, Q), 1)
    incl_le = jnp.where(li <= lj, 1.0, 0.0).astype(bf16)
    incl_ge = jnp.where(li >= lj, 1.0, 0.0).astype(bf16)
    pad_rows = jnp.zeros((Q - 2 * SSD_HEADS, Q), f32)
    for ch in range(tm // Q):
        r = slice(ch * Q, (ch + 1) * Q)
        aT = a[r].T
        dtT = dt[r].T
        pre = _dot3(aT[0:SSD_HEADS], incl_le)
        suf = _dot3(aT[SSD_HEADS:2 * SSD_HEADS], incl_ge)
        rows = jnp.concatenate([pre, suf, pad_rows], axis=0) * LOG2E
        rows_ref[r, :] = rows
        rowsl_ref[r, :] = rows - jnp.log(dtT) * LOG2E
        cols_ref[r, :] = rows.T
        dtT_ref[r, :] = dtT


def _in_proj(x2, norm_w, w_in_p, qnw, knw, xqnw, cos_t, sinlo_t, sinhi_t,
             conv_w8, conv_b, dt_bias_row, a_row, seq, tm):
    tokens = x2.shape[0]
    pos_blocks = seq // tm
    hpt = tm // CONV_HALO
    n_halo = tokens // CONV_HALO
    row = lambda i: (i, 0)
    const = lambda i: (0, 0)
    pos = lambda i: (i % pos_blocks, 0)
    prev = lambda i: (jnp.maximum(i * hpt - 1, 0), 0)
    nxt = lambda i: (jnp.minimum((i + 1) * hpt, n_halo - 1), 0)
    out_widths = [ATT_Q_W, KDUP_W, KDUP_W, SSD_WIDTH, SSD_XBC_W, XATT_W,
                  LANES, LANES, LANES, LANES]
    out_dtypes = [bf16, bf16, bf16, bf16, bf16, bf16, f32, f32, f32, f32]
    out_specs = [pl.BlockSpec((tm, w), row) for w in out_widths]
    out_shape = [jax.ShapeDtypeStruct((tokens, w), d) for w, d in zip(out_widths, out_dtypes)]
    out_specs[2] = pl.BlockSpec((KDUP_W, tm), lambda i: (i // pos_blocks, i % pos_blocks))
    out_shape[2] = jax.ShapeDtypeStruct((tokens // seq * KDUP_W, seq), bf16)
    return pl.pallas_call(
        functools.partial(_in_proj_body, pos_blocks),
        grid=(tokens // tm,),
        in_specs=[
            pl.BlockSpec((tm, D_MODEL), row),
            pl.BlockSpec((CONV_HALO, D_MODEL), prev),
            pl.BlockSpec((CONV_HALO, D_MODEL), nxt),
            pl.BlockSpec((1, D_MODEL), const),
            pl.BlockSpec((D_MODEL, D_IN_P), const, pipeline_mode=pl.Buffered(1)),
            pl.BlockSpec((1, ATT_Q_W), const),
            pl.BlockSpec((1, KDUP_W), const),
            pl.BlockSpec((1, XATT_HEAD_DIM), const),
            pl.BlockSpec((tm, LANES), pos),
            pl.BlockSpec((tm, LANES), pos),
            pl.BlockSpec((tm, LANES), pos),
            pl.BlockSpec((8, SSD_XBC_W), const),
            pl.BlockSpec((1, SSD_XBC_W), const),
            pl.BlockSpec((1, DT_PAD_W), const),
            pl.BlockSpec((1, DT_PAD_W), const),
        ],
        out_specs=out_specs,
        out_shape=out_shape,
        scratch_shapes=[pltpu.VMEM((tm + 2 * CONV_HALO, SSD_XBC_W), f32)],
        compiler_params=pltpu.CompilerParams(
            dimension_semantics=("arbitrary",), vmem_limit_bytes=VMEM_LIMIT),
        name="in_proj",
    )(x2, x2, x2, norm_w, w_in_p, qnw, knw, xqnw, cos_t, sinlo_t, sinhi_t,
      conv_w8, conv_b, dt_bias_row, a_row)


def _attn_kernel(sink_ref, mfirst_ref, mlast_ref, q_ref, kp_ref, kc_ref, kn_ref,
                 vp_ref, vc_ref, vn_ref, o_ref):
    blk = ATT_BLOCK
    heads_per_kv = ATT_HEADS // ATT_KV_HEADS
    cols = heads_per_kv * blk
    low_lanes = _iota((blk, LANES), 1) < ATT_HEAD_DIM
    head_of_col = _iota((1, cols), 1) // blk
    masks = (mfirst_ref, mlast_ref)
    half = LANES // 2

    kblocks, vblocks, sinks = [], [], []
    for g in range(ATT_KV_HEADS):
        gs = slice(g * LANES, (g + 1) * LANES)
        kblocks.append([kp_ref[:, gs], kc_ref[0:blk, gs], kc_ref[blk:2 * blk, gs], kn_ref[:, gs]])
        vblocks.append([vp_ref[gs, :], vc_ref[gs, 0:blk], vc_ref[gs, blk:2 * blk], vn_ref[gs, :]])
        sink = jnp.zeros((1, cols), f32)
        for j in range(heads_per_kv):
            sink = jnp.where(head_of_col == j, sink_ref[g * heads_per_kv + j] * LOG2E, sink)
        sinks.append(sink)

    def scores(g, t):
        qrows = slice(t * blk, (t + 1) * blk)
        kcat = jnp.concatenate(kblocks[g][t:t + 3], axis=0)
        qs = []
        for j in range(heads_per_kv):
            hd = g * heads_per_kv + j
            pair = q_ref[qrows, (hd // 2) * LANES:(hd // 2 + 1) * LANES]
            keep = low_lanes if hd % 2 == 0 else jnp.logical_not(low_lanes)
            qs.append(jnp.where(keep, pair, jnp.zeros_like(pair)))
        qg = jnp.concatenate(qs, axis=0)
        return _dot_nt(kcat, qg)

    def softmax(g, t, st):
        st = st + masks[t][...]
        m = jnp.maximum(jnp.max(st, axis=0, keepdims=True), sinks[g])
        p = jnp.exp2(st - m)
        den = jnp.sum(p, axis=0, keepdims=True) + jnp.exp2(sinks[g] - m)
        return p.astype(bf16), den

    def values(g, t, p, den):
        vt = jnp.concatenate(vblocks[g][t:t + 3], axis=1)
        return _dot(vt, p) / den

    def emit(g, t, ot):
        qrows = slice(t * blk, (t + 1) * blk)
        for jp in range(heads_per_kv // 2):
            even = ot[0:half, (2 * jp) * blk:(2 * jp + 1) * blk]
            odd = ot[half:, (2 * jp + 1) * blk:(2 * jp + 2) * blk]
            pr = g * (heads_per_kv // 2) + jp
            o_ref[qrows, pr * LANES:(pr + 1) * LANES] = (
                jnp.concatenate([even, odd], axis=0).T.astype(bf16))

    chains = [(g, t) for g in range(ATT_KV_HEADS) for t in range(ATT_BLOCKS_PER_STEP)]
    st_next = scores(*chains[0])
    pending = None
    for idx, (g, t) in enumerate(chains):
        st = st_next
        if idx + 1 < len(chains):
            st_next = scores(*chains[idx + 1])
        p, den = softmax(g, t, st)
        if pending is not None:
            emit(*pending)
        pending = (g, t, values(g, t, p, den))
    emit(*pending)


def _attn_masks():
    blk = ATT_BLOCK
    cols = (ATT_HEADS // ATT_KV_HEADS) * blk
    key = np.arange(3 * blk)[:, None]
    qi = np.arange(cols)[None, :] % blk
    band = (key >= qi) & (key <= qi + 2 * WINDOW)
    variants = [band & (key >= blk), band, band & (key < 2 * blk)]
    return jnp.asarray(np.where(np.stack(variants), 0.0, NEG_BIG), dtype=f32)


def _attention(sink, q, kdup, vt, batch, seq):
    blk = ATT_BLOCK
    nb = seq // blk
    per = ATT_BLOCKS_PER_STEP
    ns = nb // per
    cols = (ATT_HEADS // ATT_KV_HEADS) * blk
    masks = _attn_masks()
    cur = lambda b, s: (b * ns + s, 0)
    prev = lambda b, s: (b * nb + jnp.maximum(per * s - 1, 0), 0)
    nxt = lambda b, s: (b * nb + jnp.minimum(per * s + per, nb - 1), 0)
    vcur = lambda b, s: (b, s)
    vprev = lambda b, s: (b, jnp.maximum(per * s - 1, 0))
    vnxt = lambda b, s: (b, jnp.minimum(per * s + per, nb - 1))
    mfirst = lambda b, s: (jnp.where(s == 0, 0, 1), 0, 0)
    mlast = lambda b, s: (jnp.where(s == ns - 1, 2, 1), 0, 0)
    return pl.pallas_call(
        _attn_kernel,
        grid=(batch, ns),
        in_specs=[
            pl.BlockSpec(memory_space=pltpu.SMEM),
            pl.BlockSpec((None, 3 * blk, cols), mfirst),
            pl.BlockSpec((None, 3 * blk, cols), mlast),
            pl.BlockSpec((per * blk, ATT_Q_W), cur),
            pl.BlockSpec((blk, KDUP_W), prev),
            pl.BlockSpec((per * blk, KDUP_W), cur),
            pl.BlockSpec((blk, KDUP_W), nxt),
            pl.BlockSpec((KDUP_W, blk), vprev),
            pl.BlockSpec((KDUP_W, per * blk), vcur),
            pl.BlockSpec((KDUP_W, blk), vnxt),
        ],
        out_specs=pl.BlockSpec((per * blk, ATT_Q_W), cur),
        out_shape=jax.ShapeDtypeStruct((batch * seq, ATT_Q_W), bf16),
        compiler_params=pltpu.CompilerParams(
            dimension_semantics=("arbitrary", "arbitrary"), vmem_limit_bytes=VMEM_LIMIT),
        name="attention",
    )(sink, masks, masks, q, kdup, kdup, kdup, vt, vt, vt)


def _ssd_kernel(xc_ref, rows_ref, rowsl_ref, cols_ref, dtT_ref, z_ref, nw_ref, dmat_ref, y_ref,
                hf_s, sb_s, hf_run, hb_run, y_s):
    ph = pl.program_id(1)
    c = pl.program_id(2)
    nc = pl.num_programs(2)
    Q = SSD_CHUNK
    heads_per_group = SSD_HEADS // SSD_GROUPS
    low_lanes = _iota((Q, LANES), 1) < SSD_HEAD_DIM
    low_row = _iota((1, LANES), 1) < SSD_HEAD_DIM
    li = _iota((Q, Q), 0)
    si = _iota((Q, Q), 1)
    rows = rows_ref[...]
    pre = rows[0:SSD_HEADS]
    suf = rows[SSD_HEADS:2 * SSD_HEADS]

    @pl.when(ph == 0)
    def _phase0():
        @pl.when(c == 0)
        def _():
            hf_run[...] = jnp.zeros_like(hf_run)

        dtT = dtT_ref[...]
        pre_end = jnp.broadcast_to(pre[:, Q - 1:Q], (SSD_HEADS, Q))
        suf_end = jnp.broadcast_to(suf[:, 0:1], (SSD_HEADS, Q))
        wf = jnp.exp2(pre_end - pre) * dtT[0:SSD_HEADS]
        wb = jnp.exp2(suf_end - suf) * dtT[SSD_HEADS:2 * SSD_HEADS]
        decf = jnp.exp2(pre_end)

        for g in range(SSD_GROUPS):
            bT = xc_ref[:, C_B + g * SSD_STATE:C_B + (g + 1) * SSD_STATE].astype(f32).T
            for m in range(heads_per_group // 2):
                h0 = g * heads_per_group + 2 * m
                h1 = h0 + 1
                lhs = jnp.concatenate(
                    [bT * wf[h0:h0 + 1], bT * wb[h0:h0 + 1],
                     bT * wf[h1:h1 + 1], bT * wb[h1:h1 + 1]], axis=0).astype(bf16)
                cs = slice(h0 * SSD_HEAD_DIM, h0 * SSD_HEAD_DIM + LANES)
                res = _dot(lhs, xc_ref[:, cs])
                s_f = jnp.where(low_lanes, res[0:Q], res[2 * Q:3 * Q])
                s_b = jnp.where(low_lanes, res[Q:2 * Q], res[3 * Q:4 * Q])
                dec = jnp.where(low_row, decf[h0:h0 + 1], decf[h1:h1 + 1])
                prev = hf_run[:, cs]
                hf_s[c, :, cs] = prev.astype(bf16)
                hf_run[:, cs] = dec * prev + s_f
                sb_s[c, :, cs] = s_b.astype(bf16)

    @pl.when(ph == 1)
    def _phase1():
        @pl.when(c == 0)
        def _():
            hb_run[...] = jnp.zeros_like(hb_run)

        cc = nc - 1 - c
        cols = cols_ref[...]
        rowsl = rowsl_ref[...]
        decb = jnp.exp2(jnp.broadcast_to(suf[:, 0:1], (SSD_HEADS, Q)))
        lower = li >= si
        upper = si >= li

        for g in range(SSD_GROUPS):
            bg = xc_ref[:, C_B + g * SSD_STATE:C_B + (g + 1) * SSD_STATE]
            cg = xc_ref[:, C_C + g * SSD_STATE:C_C + (g + 1) * SSD_STATE]
            cbm = _dot_nt(cg, bg).astype(bf16)
            for m in range(heads_per_group // 2):
                h0 = g * heads_per_group + 2 * m
                parts = []
                for hd in (h0, h0 + 1):
                    hb = SSD_HEADS + hd
                    colf = jnp.broadcast_to(cols[:, hd:hd + 1], (Q, Q))
                    colb = jnp.broadcast_to(cols[:, hb:hb + 1], (Q, Q))
                    ef = jnp.exp2(jnp.where(lower, colf - rowsl[hd:hd + 1], NEG_BIG)).astype(bf16)
                    eb = jnp.exp2(jnp.where(upper, colb - rowsl[hb:hb + 1], NEG_BIG)).astype(bf16)
                    gm = cbm * (ef + eb) + dmat_ref[hd]
                    parts.append(jnp.concatenate(
                        [gm, cg * jnp.exp2(colf).astype(bf16), cg * jnp.exp2(colb).astype(bf16)],
                        axis=1))
                lhs = jnp.concatenate(parts, axis=0)
                cs = slice(h0 * SSD_HEAD_DIM, h0 * SSD_HEAD_DIM + LANES)
                hb_prev = hb_run[:, cs]
                rhs = jnp.concatenate(
                    [xc_ref[:, cs], hf_s[cc, :, cs], hb_prev.astype(bf16)], axis=0)
                res = _dot(lhs, rhs)
                y_s[:, cs] = jnp.where(low_lanes, res[0:Q], res[Q:2 * Q])
                dec = jnp.where(low_row, decb[h0:h0 + 1], decb[h0 + 1:h0 + 2])
                hb_run[:, cs] = dec * hb_prev + sb_s[cc, :, cs].astype(f32)

        zf = z_ref[...].astype(f32)
        y = y_s[...] * (zf / (1.0 + jnp.exp2(zf * (-LOG2E))))
        gw = SSD_WIDTH // SSD_GROUPS
        for g in range(SSD_GROUPS):
            yg = y[:, g * gw:(g + 1) * gw]
            ms = jnp.mean(yg * yg, axis=-1, keepdims=True)
            y_ref[:, g * gw:(g + 1) * gw] = (
                yg * lax.rsqrt(ms + EPS) * nw_ref[:, g * gw:(g + 1) * gw]).astype(bf16)


def _ssd(dmat, xc, rows, rowsl, cols, dtT, z, norm_w, batch, seq):
    Q = SSD_CHUNK
    nc = seq // Q
    last = nc - 1

    def both_map(b, ph, c):
        return (b * nc + c * (1 - ph) + (last - c) * ph, 0)

    def fwd_map(b, ph, c):
        return (b * nc + c * (1 - ph) + last * ph, 0)

    def back_map(b, ph, c):
        return (b * nc + last - c * ph, 0)

    const = lambda b, ph, c: (0, 0)
    return pl.pallas_call(
        _ssd_kernel,
        grid=(batch, 2, nc),
        in_specs=[
            pl.BlockSpec((Q, SSD_XBC_W), both_map),
            pl.BlockSpec((Q, LANES), both_map),
            pl.BlockSpec((Q, LANES), back_map),
            pl.BlockSpec((Q, LANES), back_map),
            pl.BlockSpec((Q, LANES), fwd_map),
            pl.BlockSpec((Q, SSD_WIDTH), back_map),
            pl.BlockSpec((1, SSD_WIDTH), const),
            pl.BlockSpec((SSD_HEADS, Q, Q), lambda b, ph, c: (0, 0, 0)),
        ],
        out_specs=pl.BlockSpec((Q, SSD_WIDTH), back_map),
        out_shape=jax.ShapeDtypeStruct((batch * seq, SSD_WIDTH), bf16),
        scratch_shapes=[
            pltpu.VMEM((nc, SSD_STATE, SSD_WIDTH), bf16),
            pltpu.VMEM((nc, SSD_STATE, SSD_WIDTH), bf16),
            pltpu.VMEM((SSD_STATE, SSD_WIDTH), f32),
            pltpu.VMEM((SSD_STATE, SSD_WIDTH), f32),
            pltpu.VMEM((Q, SSD_WIDTH), f32),
        ],
        compiler_params=pltpu.CompilerParams(
            dimension_semantics=("arbitrary", "arbitrary", "arbitrary"),
            vmem_limit_bytes=VMEM_LIMIT),
        name="ssd",
    )(xc, rows, rowsl, cols, dtT, z, norm_w, dmat)


def _mem_kv_kernel(mem_ref, nw_ref, w_ref, knw_ref, k_ref, v_ref):
    h = _rms_rows(mem_ref[...], nw_ref[...]).astype(bf16)
    kv = _dot(h, w_ref[...])
    for c in range(XATT_HEADS):
        sl = slice(c * XATT_HEAD_DIM, (c + 1) * XATT_HEAD_DIM)
        t = kv[:, sl]
        k_ref[:, sl] = (t * lax.rsqrt(jnp.mean(t * t, axis=-1, keepdims=True) + EPS)
                        * knw_ref[...]).astype(bf16)
    v_ref[...] = kv[:, XATT_W:].astype(bf16)


def _mem_kv(mem2, mem_norm_w, w_kv, xk_norm_w, batch, mem_len):
    row = lambda b: (b, 0)
    const = lambda b: (0, 0)
    return pl.pallas_call(
        _mem_kv_kernel,
        grid=(batch,),
        in_specs=[
            pl.BlockSpec((mem_len, D_MODEL), row),
            pl.BlockSpec((1, D_MODEL), const),
            pl.BlockSpec((D_MODEL, 2 * XATT_W), const),
            pl.BlockSpec((1, XATT_HEAD_DIM), const),
        ],
        out_specs=[pl.BlockSpec((mem_len, XATT_W), row)] * 2,
        out_shape=[jax.ShapeDtypeStruct((batch * mem_len, XATT_W), bf16)] * 2,
        compiler_params=pltpu.CompilerParams(
            dimension_semantics=("arbitrary",), vmem_limit_bytes=VMEM_LIMIT),
        name="mem_kv",
    )(mem2, mem_norm_w, w_kv, xk_norm_w)


def _xattn_kernel(q_ref, k_ref, v_ref, o_ref):
    for hd in range(XATT_HEADS):
        sl = slice(hd * XATT_HEAD_DIM, (hd + 1) * XATT_HEAD_DIM)
        s = _dot_nt(q_ref[:, sl], k_ref[:, sl])
        m = jnp.max(s, axis=-1, keepdims=True)
        p = jnp.exp2(s - m)
        den = jnp.sum(p, axis=-1, keepdims=True)
        o_ref[:, sl] = (_dot(p.astype(bf16), v_ref[:, sl]) / den).astype(bf16)


def _xattn(qx, km, vm, batch, seq, mem_len, tq):
    nq = seq // tq
    qmap = lambda b, i: (b * nq + i, 0)
    mmap = lambda b, i: (b, 0)
    return pl.pallas_call(
        _xattn_kernel,
        grid=(batch, nq),
        in_specs=[
            pl.BlockSpec((tq, XATT_W), qmap),
            pl.BlockSpec((mem_len, XATT_W), mmap),
            pl.BlockSpec((mem_len, XATT_W), mmap),
        ],
        out_specs=pl.BlockSpec((tq, XATT_W), qmap),
        out_shape=jax.ShapeDtypeStruct((batch * seq, XATT_W), bf16),
        compiler_params=pltpu.CompilerParams(
            dimension_semantics=("arbitrary", "arbitrary"), vmem_limit_bytes=VMEM_LIMIT),
        name="xattn",
    )(qx, km, vm)


FF_CHUNK = 1024


def _out_mlp_kernel(x_ref, a_ref, s_ref, c_ref, wo_ref, nw_ref, wu_ref, wd_ref, o_ref):
    mix = (_dot(a_ref[...], wo_ref[0:ATT_Q_W, :])
           + _dot(s_ref[...], wo_ref[ATT_Q_W:ATT_Q_W + SSD_WIDTH, :])
           + _dot(c_ref[...], wo_ref[ATT_Q_W + SSD_WIDTH:, :]))
    x1 = x_ref[...] + mix
    h = _rms_rows(x1, nw_ref[...]).astype(bf16)
    acc = x1
    for j in range(D_FF // FF_CHUNK):
        u = _dot(h, wu_ref[:, j * FF_CHUNK:(j + 1) * FF_CHUNK])
        r = jnp.maximum(u, 0.0)
        acc = acc + _dot((r * r).astype(bf16), wd_ref[j * FF_CHUNK:(j + 1) * FF_CHUNK, :])
    o_ref[...] = acc


def _out_mlp(x2, attn, ssd, xatt, w_out, norm_w, w_up, w_down, tm):
    tokens = x2.shape[0]
    row = lambda i: (i, 0)
    const = lambda i: (0, 0)
    resident = functools.partial(pl.BlockSpec, index_map=const, pipeline_mode=pl.Buffered(1))
    return pl.pallas_call(
        _out_mlp_kernel,
        grid=(tokens // tm,),
        in_specs=[
            pl.BlockSpec((tm, D_MODEL), row),
            pl.BlockSpec((tm, ATT_Q_W), row),
            pl.BlockSpec((tm, SSD_WIDTH), row),
            pl.BlockSpec((tm, XATT_W), row),
            resident((D_MIX, D_MODEL)),
            pl.BlockSpec((1, D_MODEL), const),
            resident((D_MODEL, D_FF)),
            resident((D_FF, D_MODEL)),
        ],
        out_specs=pl.BlockSpec((tm, D_MODEL), row),
        out_shape=jax.ShapeDtypeStruct((tokens, D_MODEL), f32),
        compiler_params=pltpu.CompilerParams(
            dimension_semantics=("arbitrary",), vmem_limit_bytes=VMEM_LIMIT),
        name="out_mlp",
    )(x2, attn, ssd, xatt, w_out, norm_w, w_up, w_down)


def _rope_tables(seq):
    half = ROPE_DIM // 2
    inv = ROPE_THETA ** (-jnp.arange(0, ROPE_DIM, 2, dtype=f32) / ROPE_DIM)
    ang = jnp.arange(seq, dtype=f32)[:, None] * inv[None, :]
    cos = jnp.cos(ang)
    sin = jnp.sin(ang)
    ones = jnp.ones((seq, ATT_HEAD_DIM - ROPE_DIM), f32)
    zeros_h = jnp.zeros((seq, half), f32)
    zeros_p = jnp.zeros((seq, ATT_HEAD_DIM - ROPE_DIM), f32)
    cos_head = jnp.concatenate([cos, cos, ones], axis=1)
    lo_head = jnp.concatenate([-sin, zeros_h, zeros_p], axis=1)
    hi_head = jnp.concatenate([zeros_h, sin, zeros_p], axis=1)
    rep = LANES // ATT_HEAD_DIM
    return (jnp.tile(cos_head, (1, rep)), jnp.tile(lo_head, (1, rep)),
            jnp.tile(hi_head, (1, rep)))


def _layer(x, mem, norm_mix_w, w_in, q_norm_w, k_norm_w, attn_sink, conv_w, conv_b,
           dt_bias_f, dt_bias_b, a_log_f, a_log_b, ssd_d, ssd_norm_w, mem_norm_w,
           w_mem_kv, xq_norm_w, xk_norm_w, w_out, norm_mlp_w, w_mlp_up, w_mlp_down, tables):
    batch, seq, _ = x.shape
    mem_len = mem.shape[1]
    tokens = batch * seq
    tm = 512

    s = np.cumsum([ATT_Q_W, ATT_KV_W, ATT_KV_W, SSD_WIDTH, SSD_XBC_W, SSD_DT_W, XATT_W])
    wb = w_in.astype(bf16)
    w_q, w_k, w_v = wb[:, :s[0]], wb[:, s[0]:s[1]], wb[:, s[1]:s[2]]
    w_z, w_xbc, w_dt, w_qx = (wb[:, s[2]:s[3]], wb[:, s[3]:s[4]],
                              wb[:, s[4]:s[5]], wb[:, s[5]:s[6]])
    hd = ATT_HEAD_DIM
    dup = lambda w: jnp.concatenate(
        [w[:, g * hd:(g + 1) * hd] for g in range(ATT_KV_HEADS) for _ in range(2)], axis=1)
    w_in_p = jnp.concatenate(
        [w_q, dup(w_k), dup(w_v), w_z, w_xbc, w_qx, w_dt,
         jnp.zeros((D_MODEL, DT_PAD_W - SSD_DT_W), bf16)], axis=1)

    pad = jnp.zeros((DT_PAD_W - SSD_DT_W,), f32)
    dt_bias_row = jnp.concatenate([dt_bias_f, dt_bias_b, pad]).reshape(1, DT_PAD_W)
    a_row = jnp.concatenate([-jnp.exp(a_log_f), -jnp.exp(a_log_b), pad]).reshape(1, DT_PAD_W)
    conv_w8 = jnp.concatenate([conv_w, jnp.zeros((8 - SSD_CONV, SSD_XBC_W), f32)], axis=0)
    dmat = (ssd_d.astype(f32)[:, None, None] * jnp.eye(SSD_CHUNK, dtype=f32)).astype(bf16)

    x2 = x.reshape(tokens, D_MODEL)
    cos_t, sinlo_t, sinhi_t = tables
    q, kdup, vt, z, xc, qx, rows, rowsl, cols, dtT = _in_proj(
        x2, norm_mix_w.reshape(1, D_MODEL), w_in_p,
        jnp.tile(q_norm_w, ATT_HEADS).reshape(1, ATT_Q_W),
        jnp.tile(k_norm_w, KDUP_W // hd).reshape(1, KDUP_W),
        xq_norm_w.reshape(1, XATT_HEAD_DIM), cos_t, sinlo_t, sinhi_t,
        conv_w8, conv_b.reshape(1, SSD_XBC_W), dt_bias_row, a_row, seq, tm)

    attn = _attention(attn_sink.astype(f32), q, kdup, vt, batch, seq)
    ssd = _ssd(dmat, xc, rows, rowsl, cols, dtT, z,
               ssd_norm_w.reshape(1, SSD_WIDTH), batch, seq)

    km, vm = _mem_kv(mem.reshape(batch * mem_len, D_MODEL), mem_norm_w.reshape(1, D_MODEL),
                     w_mem_kv.astype(bf16), xk_norm_w.reshape(1, XATT_HEAD_DIM), batch, mem_len)
    xatt = _xattn(qx, km, vm, batch, seq, mem_len, tm)

    out = _out_mlp(x2, attn, ssd, xatt, w_out.astype(bf16), norm_mlp_w.reshape(1, D_MODEL),
                   w_mlp_up.astype(bf16), w_mlp_down.astype(bf16), tm)
    return out.reshape(batch, seq, D_MODEL)


def kernel(x, mem, norm_mix_w, w_in, q_norm_w, k_norm_w, attn_sink, conv_w, conv_b,
           dt_bias_f, dt_bias_b, a_log_f, a_log_b, ssd_d, ssd_norm_w, mem_norm_w,
           w_mem_kv, xq_norm_w, xk_norm_w, w_out, norm_mlp_w, w_mlp_up, w_mlp_down):
    depth = w_in.shape[0]
    tables = _rope_tables(x.shape[1])
    for i in range(depth):
        x = _layer(x, mem, norm_mix_w[i], w_in[i], q_norm_w[i], k_norm_w[i], attn_sink[i],
                   conv_w[i], conv_b[i], dt_bias_f[i], dt_bias_b[i], a_log_f[i], a_log_b[i],
                   ssd_d[i], ssd_norm_w[i], mem_norm_w[i], w_mem_kv[i], xq_norm_w[i],
                   xk_norm_w[i], w_out[i], norm_mlp_w[i], w_mlp_up[i], w_mlp_down[i], tables)
    return x
```

```python
import functools

import numpy as np
import jax
import jax.numpy as jnp
from jax import lax
from jax.experimental import pallas as pl
from jax.experimental.pallas import tpu as pltpu

D_MODEL = 1024
EPS = 1e-6

ATT_HEADS = 8
ATT_KV_HEADS = 2
ATT_HEAD_DIM = 64
ATT_Q_W = ATT_HEADS * ATT_HEAD_DIM
ATT_KV_W = ATT_KV_HEADS * ATT_HEAD_DIM
WINDOW = 128
ATT_BLOCK = 128
ATT_BLOCKS_PER_STEP = 4
ROPE_THETA = 500000.0
ROPE_DIM = ATT_HEAD_DIM // 4

SSD_HEADS = 16
SSD_HEAD_DIM = 64
SSD_WIDTH = SSD_HEADS * SSD_HEAD_DIM
SSD_GROUPS = 2
SSD_STATE = 128
SSD_CONV = 5
SSD_CHUNK = 128
SSD_CHUNKS_PER_STEP = 2
SSD_XBC_W = SSD_WIDTH + 2 * SSD_GROUPS * SSD_STATE
SSD_DT_W = 2 * SSD_HEADS

XATT_HEADS = 4
XATT_HEAD_DIM = 128
XATT_W = XATT_HEADS * XATT_HEAD_DIM
XATT_TQ = 1024

D_FF = 4 * D_MODEL
D_MIX = 2 * D_MODEL

LANES = 128
BF16_SUBLANES = 16
VMEM_LIMIT = 56 * 1024 * 1024

KDUP_W = 2 * ATT_KV_W
DT_PAD_W = LANES
C_Q = 0
C_K = C_Q + ATT_Q_W
C_V = C_K + KDUP_W
C_Z = C_V + KDUP_W
C_XBC = C_Z + SSD_WIDTH
C_QX = C_XBC + SSD_XBC_W
C_DT = C_QX + XATT_W
D_IN_P = C_DT + DT_PAD_W

CONV_HALO = BF16_SUBLANES
C_B = SSD_WIDTH
C_C = SSD_WIDTH + SSD_GROUPS * SSD_STATE

NEG_BIG = -1e30
LOG2E = 1.4426950408889634

bf16 = jnp.bfloat16
f32 = jnp.float32


def _dot(a, b):
    return jnp.dot(a, b, preferred_element_type=f32)


def _dot_nt(a, b):
    return lax.dot_general(a, b, (((1,), (1,)), ((), ())), preferred_element_type=f32)


def _split3(x):
    x1 = x.astype(bf16)
    r1 = x - x1.astype(f32)
    x2 = r1.astype(bf16)
    r2 = r1 - x2.astype(f32)
    return x1, x2, r2.astype(bf16)


def _dot3(x, m01):
    x1, x2, x3 = _split3(x)
    return _dot(x1, m01) + _dot(x2, m01) + _dot(x3, m01)


def _iota(shape, dim):
    return lax.broadcasted_iota(jnp.int32, shape, dim)


def _segment_ones(width, seg):
    r = _iota((width, width), 0) // seg
    c = _iota((width, width), 1) // seg
    return jnp.where(r == c, 1.0, 0.0).astype(bf16)


def _head_mean_sq(t, seg):
    ones = _segment_ones(t.shape[1], seg)
    return _dot((t * t).astype(bf16), ones) * (1.0 / seg)


def _rope(t, cos, sin_lo, sin_hi):
    half = ROPE_DIM // 2
    return (t * cos + pltpu.roll(t, LANES - half, 1) * sin_lo
            + pltpu.roll(t, half, 1) * sin_hi)


def _rms_rows(x, w):
    ms = jnp.mean(x * x, axis=-1, keepdims=True)
    return (x * lax.rsqrt(ms + EPS)) * w


def _in_proj_body(pos_blocks, x_ref, xp_ref, xn_ref, nw_ref,
                  wq_ref, wk_ref, wv_ref, wz_ref, wxbc_ref, wqx_ref, wdt_ref, qnw_ref, knw_ref,
                  xqnw_ref, cos_ref, sinlo_ref, sinhi_ref, cw_ref, cb_ref, dtb_ref, arow_ref,
                  q_ref, k_ref, v_ref, z_ref, xc_ref, qx_ref,
                  rows_ref, rowsl_ref, cols_ref, dtT_ref, conv_s):
    i = pl.program_id(0)
    tm = x_ref.shape[0]
    si = i % pos_blocks
    nw = nw_ref[...]
    h = _rms_rows(x_ref[...], nw).astype(bf16)

    cos = cos_ref[...]
    sin_lo = sinlo_ref[...]
    sin_hi = sinhi_ref[...]

    pq = _dot(h, wq_ref[...])
    qn = pq * lax.rsqrt(_head_mean_sq(pq, ATT_HEAD_DIM) + EPS) * qnw_ref[...]
    for c in range(ATT_Q_W // LANES):
        sl = slice(c * LANES, (c + 1) * LANES)
        q_ref[:, sl] = (_rope(qn[:, sl], cos, sin_lo, sin_hi)
                        * (ATT_HEAD_DIM ** -0.5 * LOG2E)).astype(bf16)

    pk = _dot(h, wk_ref[...])
    kn = pk * lax.rsqrt(_head_mean_sq(pk, ATT_HEAD_DIM) + EPS) * knw_ref[...]
    for c in range(KDUP_W // LANES):
        sl = slice(c * LANES, (c + 1) * LANES)
        k_ref[:, sl] = _rope(kn[:, sl], cos, sin_lo, sin_hi).astype(bf16)

    v_ref[...] = _dot(h, wv_ref[...]).T.astype(bf16)
    z_ref[...] = _dot(h, wz_ref[...]).astype(bf16)

    pqx = _dot(h, wqx_ref[...])
    for c in range(XATT_HEADS):
        sl = slice(c * XATT_HEAD_DIM, (c + 1) * XATT_HEAD_DIM)
        t = pqx[:, sl]
        tn = t * lax.rsqrt(jnp.mean(t * t, axis=-1, keepdims=True) + EPS) * xqnw_ref[...]
        qx_ref[:, sl] = (tn * (XATT_HEAD_DIM ** -0.5 * LOG2E)).astype(bf16)

    hp = _rms_rows(xp_ref[...], nw)
    hn = _rms_rows(xn_ref[...], nw)
    hp = jnp.where(si > 0, hp, jnp.zeros_like(hp)).astype(bf16)
    hn = jnp.where(si < pos_blocks - 1, hn, jnp.zeros_like(hn)).astype(bf16)
    h_ext = jnp.concatenate([hp, h, hn], axis=0)
    conv_s[...] = _dot(h_ext, wxbc_ref[...])
    acc = jnp.broadcast_to(cb_ref[...], (tm, SSD_XBC_W))
    for k in range(SSD_CONV):
        off = CONV_HALO - SSD_CONV // 2 + k
        acc = acc + conv_s[off:off + tm, :] * cw_ref[k:k + 1, :]
    xc_ref[...] = (acc / (1.0 + jnp.exp2(acc * (-LOG2E)))).astype(bf16)

    Q = SSD_CHUNK
    dt = _dot(h, wdt_ref[...]) + dtb_ref[...]
    dt = jnp.maximum(dt, 0.0) + jnp.log1p(jnp.exp(-jnp.abs(dt)))
    a = dt * arow_ref[...]
    li = _iota((Q, Q), 0)
    lj = _iota((Q, Q), 1)
    incl_le = jnp.where(li <= lj, 1.0, 0.0).astype(bf16)
    incl_ge = jnp.where(li >= lj, 1.0, 0.0).astype(bf16)
    pad_rows = jnp.zeros((Q - 2 * SSD_HEADS, Q), f32)
    for ch in range(tm // Q):
        r = slice(ch * Q, (ch + 1) * Q)
        aT = a[r].T
        dtT = dt[r].T
        pre = _dot3(aT[0:SSD_HEADS], incl_le)
        suf = _dot3(aT[SSD_HEADS:2 * SSD_HEADS], incl_ge)
        rows = jnp.concatenate([pre, suf, pad_rows], axis=0) * LOG2E
        rows_ref[r, :] = rows
        rowsl_ref[r, :] = rows - jnp.log(dtT) * LOG2E
        cols_ref[r, :] = rows.T
        dtT_ref[r, :] = dtT


def _in_proj(x2, norm_w, weights, qnw, knw, xqnw, cos_t, sinlo_t, sinhi_t,
             conv_w8, conv_b, dt_bias_row, a_row, seq, tm):
    tokens = x2.shape[0]
    pos_blocks = seq // tm
    hpt = tm // CONV_HALO
    n_halo = tokens // CONV_HALO
    row = lambda i: (i, 0)
    const = lambda i: (0, 0)
    pos = lambda i: (i % pos_blocks, 0)
    prev = lambda i: (jnp.maximum(i * hpt - 1, 0), 0)
    nxt = lambda i: (jnp.minimum((i + 1) * hpt, n_halo - 1), 0)
    out_widths = [ATT_Q_W, KDUP_W, KDUP_W, SSD_WIDTH, SSD_XBC_W, XATT_W,
                  LANES, LANES, LANES, LANES]
    out_dtypes = [bf16, bf16, bf16, bf16, bf16, bf16, f32, f32, f32, f32]
    out_specs = [pl.BlockSpec((tm, w), row) for w in out_widths]
    out_shape = [jax.ShapeDtypeStruct((tokens, w), d) for w, d in zip(out_widths, out_dtypes)]
    out_specs[2] = pl.BlockSpec((KDUP_W, tm), lambda i: (i // pos_blocks, i % pos_blocks))
    out_shape[2] = jax.ShapeDtypeStruct((tokens // seq * KDUP_W, seq), bf16)
    return pl.pallas_call(
        functools.partial(_in_proj_body, pos_blocks),
        grid=(tokens // tm,),
        in_specs=[
            pl.BlockSpec((tm, D_MODEL), row),
            pl.BlockSpec((CONV_HALO, D_MODEL), prev),
            pl.BlockSpec((CONV_HALO, D_MODEL), nxt),
            pl.BlockSpec((1, D_MODEL), const),
            *[pl.BlockSpec(w.shape, const, pipeline_mode=pl.Buffered(1)) for w in weights],
            pl.BlockSpec((1, ATT_Q_W), const),
            pl.BlockSpec((1, KDUP_W), const),
            pl.BlockSpec((1, XATT_HEAD_DIM), const),
            pl.BlockSpec((tm, LANES), pos),
            pl.BlockSpec((tm, LANES), pos),
            pl.BlockSpec((tm, LANES), pos),
            pl.BlockSpec((8, SSD_XBC_W), const),
            pl.BlockSpec((1, SSD_XBC_W), const),
            pl.BlockSpec((1, DT_PAD_W), const),
            pl.BlockSpec((1, DT_PAD_W), const),
        ],
        out_specs=out_specs,
        out_shape=out_shape,
        scratch_shapes=[pltpu.VMEM((tm + 2 * CONV_HALO, SSD_XBC_W), f32)],
        compiler_params=pltpu.CompilerParams(
            dimension_semantics=("arbitrary",), vmem_limit_bytes=VMEM_LIMIT),
        name="in_proj",
    )(x2, x2, x2, norm_w, *weights, qnw, knw, xqnw, cos_t, sinlo_t, sinhi_t,
      conv_w8, conv_b, dt_bias_row, a_row)


def _attn_kernel(sink_ref, mfirst_ref, mmid_ref, mlast_ref, q_ref, kp_ref, kc_ref, kn_ref,
                 vp_ref, vc_ref, vn_ref, o_ref):
    blk = ATT_BLOCK
    per = ATT_BLOCKS_PER_STEP
    heads_per_kv = ATT_HEADS // ATT_KV_HEADS
    cols = heads_per_kv * blk
    low_lanes = _iota((blk, LANES), 1) < ATT_HEAD_DIM
    head_of_col = _iota((1, cols), 1) // blk
    masks = [mfirst_ref] + [mmid_ref] * (per - 2) + [mlast_ref]
    half = LANES // 2

    kblocks, vblocks, sinks = [], [], []
    for g in range(ATT_KV_HEADS):
        gs = slice(g * LANES, (g + 1) * LANES)
        kblocks.append([kp_ref[:, gs]] + [kc_ref[i * blk:(i + 1) * blk, gs] for i in range(per)]
                       + [kn_ref[:, gs]])
        vblocks.append([vp_ref[gs, :]] + [vc_ref[gs, i * blk:(i + 1) * blk] for i in range(per)]
                       + [vn_ref[gs, :]])
        sink = jnp.zeros((1, cols), f32)
        for j in range(heads_per_kv):
            sink = jnp.where(head_of_col == j, sink_ref[g * heads_per_kv + j] * LOG2E, sink)
        sinks.append(sink)

    def scores(g, t):
        qrows = slice(t * blk, (t + 1) * blk)
        kcat = jnp.concatenate(kblocks[g][t:t + 3], axis=0)
        qs = []
        for j in range(heads_per_kv):
            hd = g * heads_per_kv + j
            pair = q_ref[qrows, (hd // 2) * LANES:(hd // 2 + 1) * LANES]
            keep = low_lanes if hd % 2 == 0 else jnp.logical_not(low_lanes)
            qs.append(jnp.where(keep, pair, jnp.zeros_like(pair)))
        qg = jnp.concatenate(qs, axis=0)
        return _dot_nt(kcat, qg)

    def softmax(g, t, st):
        st = st + masks[t][...]
        m = jnp.maximum(jnp.max(st, axis=0, keepdims=True), sinks[g])
        p = jnp.exp2(st - m)
        den = jnp.sum(p, axis=0, keepdims=True) + jnp.exp2(sinks[g] - m)
        return p.astype(bf16), den

    def values(g, t, p, den):
        vt = jnp.concatenate(vblocks[g][t:t + 3], axis=1)
        return _dot(vt, p) / den

    def emit(g, t, ot):
        qrows = slice(t * blk, (t + 1) * blk)
        for jp in range(heads_per_kv // 2):
            even = ot[0:half, (2 * jp) * blk:(2 * jp + 1) * blk]
            odd = ot[half:, (2 * jp + 1) * blk:(2 * jp + 2) * blk]
            pr = g * (heads_per_kv // 2) + jp
            o_ref[qrows, pr * LANES:(pr + 1) * LANES] = (
                jnp.concatenate([even, odd], axis=0).T.astype(bf16))

    for g in range(ATT_KV_HEADS):
        for t in range(ATT_BLOCKS_PER_STEP):
            p, den = softmax(g, t, scores(g, t))
            emit(g, t, values(g, t, p, den))
def _attn_masks():
    blk = ATT_BLOCK
    cols = (ATT_HEADS // ATT_KV_HEADS) * blk
    key = np.arange(3 * blk)[:, None]
    qi = np.arange(cols)[None, :] % blk
    band = (key >= qi) & (key <= qi + 2 * WINDOW)
    variants = [band & (key >= blk), band, band & (key < 2 * blk)]
    return jnp.asarray(np.where(np.stack(variants), 0.0, NEG_BIG), dtype=f32)


def _attention(sink, q, kdup, vt, batch, seq):
    blk = ATT_BLOCK
    nb = seq // blk
    per = ATT_BLOCKS_PER_STEP
    ns = nb // per
    cols = (ATT_HEADS // ATT_KV_HEADS) * blk
    masks = _attn_masks()
    cur = lambda b, s: (b * ns + s, 0)
    prev = lambda b, s: (b * nb + jnp.maximum(per * s - 1, 0), 0)
    nxt = lambda b, s: (b * nb + jnp.minimum(per * s + per, nb - 1), 0)
    vcur = lambda b, s: (b, s)
    vprev = lambda b, s: (b, jnp.maximum(per * s - 1, 0))
    vnxt = lambda b, s: (b, jnp.minimum(per * s + per, nb - 1))
    mfirst = lambda b, s: (jnp.where(s == 0, 0, 1), 0, 0)
    mmid = lambda b, s: (1, 0, 0)
    mlast = lambda b, s: (jnp.where(s == ns - 1, 2, 1), 0, 0)
    return pl.pallas_call(
        _attn_kernel,
        grid=(batch, ns),
        in_specs=[
            pl.BlockSpec(memory_space=pltpu.SMEM),
            pl.BlockSpec((None, 3 * blk, cols), mfirst),
            pl.BlockSpec((None, 3 * blk, cols), mmid),
            pl.BlockSpec((None, 3 * blk, cols), mlast),
            pl.BlockSpec((per * blk, ATT_Q_W), cur),
            pl.BlockSpec((blk, KDUP_W), prev),
            pl.BlockSpec((per * blk, KDUP_W), cur),
            pl.BlockSpec((blk, KDUP_W), nxt),
            pl.BlockSpec((KDUP_W, blk), vprev),
            pl.BlockSpec((KDUP_W, per * blk), vcur),
            pl.BlockSpec((KDUP_W, blk), vnxt),
        ],
        out_specs=pl.BlockSpec((per * blk, ATT_Q_W), cur),
        out_shape=jax.ShapeDtypeStruct((batch * seq, ATT_Q_W), bf16),
        compiler_params=pltpu.CompilerParams(
            dimension_semantics=("arbitrary", "arbitrary"), vmem_limit_bytes=VMEM_LIMIT),
        name="attention",
    )(sink, masks, masks, masks, q, kdup, kdup, kdup, vt, vt, vt)


def _ssd_kernel(xc_ref, rows_ref, rowsl_ref, cols_ref, dtT_ref, z_ref, nw_ref, dmat_ref, y_ref,
                hf_s, sb_s, hf_run, hb_run, y_s):
    ph = pl.program_id(1)
    step = pl.program_id(2)
    nsteps = pl.num_programs(2)
    Q = SSD_CHUNK
    per = SSD_CHUNKS_PER_STEP
    heads_per_group = SSD_HEADS // SSD_GROUPS
    low_lanes = _iota((Q, LANES), 1) < SSD_HEAD_DIM
    low_row = _iota((1, LANES), 1) < SSD_HEAD_DIM
    li = _iota((Q, Q), 0)
    si = _iota((Q, Q), 1)

    def chunk_states(c, rs):
        rows = rows_ref[rs, :]
        dtT = dtT_ref[rs, :]
        pre = rows[0:SSD_HEADS]
        suf = rows[SSD_HEADS:2 * SSD_HEADS]
        pre_end = jnp.broadcast_to(pre[:, Q - 1:Q], (SSD_HEADS, Q))
        suf_end = jnp.broadcast_to(suf[:, 0:1], (SSD_HEADS, Q))
        wf = jnp.exp2(pre_end - pre) * dtT[0:SSD_HEADS]
        wb = jnp.exp2(suf_end - suf) * dtT[SSD_HEADS:2 * SSD_HEADS]
        decf = jnp.exp2(pre_end)

        for g in range(SSD_GROUPS):
            bT = xc_ref[rs, C_B + g * SSD_STATE:C_B + (g + 1) * SSD_STATE].astype(f32).T
            for m in range(heads_per_group // 2):
                h0 = g * heads_per_group + 2 * m
                h1 = h0 + 1
                lhs = jnp.concatenate(
                    [bT * wf[h0:h0 + 1], bT * wb[h0:h0 + 1],
                     bT * wf[h1:h1 + 1], bT * wb[h1:h1 + 1]], axis=0).astype(bf16)
                cs = slice(h0 * SSD_HEAD_DIM, h0 * SSD_HEAD_DIM + LANES)
                res = _dot(lhs, xc_ref[rs, cs])
                s_f = jnp.where(low_lanes, res[0:Q], res[2 * Q:3 * Q])
                s_b = jnp.where(low_lanes, res[Q:2 * Q], res[3 * Q:4 * Q])
                dec = jnp.where(low_row, decf[h0:h0 + 1], decf[h1:h1 + 1])
                prev = hf_run[:, cs]
                hf_s[c, :, cs] = prev.astype(bf16)
                hf_run[:, cs] = dec * prev + s_f
                sb_s[c, :, cs] = s_b.astype(bf16)

    def chunk_outputs(c, rs):
        rows = rows_ref[rs, :]
        suf = rows[SSD_HEADS:2 * SSD_HEADS]
        cols = cols_ref[rs, :]
        rowsl = rowsl_ref[rs, :]
        decb = jnp.exp2(jnp.broadcast_to(suf[:, 0:1], (SSD_HEADS, Q)))
        lower = li >= si
        upper = si >= li

        for g in range(SSD_GROUPS):
            bg = xc_ref[rs, C_B + g * SSD_STATE:C_B + (g + 1) * SSD_STATE]
            cg = xc_ref[rs, C_C + g * SSD_STATE:C_C + (g + 1) * SSD_STATE]
            cbm = _dot_nt(cg, bg).astype(bf16)
            for m in range(heads_per_group // 2):
                h0 = g * heads_per_group + 2 * m
                parts = []
                for hd in (h0, h0 + 1):
                    hb = SSD_HEADS + hd
                    colf = jnp.broadcast_to(cols[:, hd:hd + 1], (Q, Q))
                    colb = jnp.broadcast_to(cols[:, hb:hb + 1], (Q, Q))
                    ef = jnp.exp2(jnp.where(lower, colf - rowsl[hd:hd + 1], NEG_BIG)).astype(bf16)
                    eb = jnp.exp2(jnp.where(upper, colb - rowsl[hb:hb + 1], NEG_BIG)).astype(bf16)
                    gm = cbm * (ef + eb) + dmat_ref[hd]
                    parts.append(jnp.concatenate(
                        [gm, cg * jnp.exp2(colf).astype(bf16), cg * jnp.exp2(colb).astype(bf16)],
                        axis=1))
                lhs = jnp.concatenate(parts, axis=0)
                cs = slice(h0 * SSD_HEAD_DIM, h0 * SSD_HEAD_DIM + LANES)
                hb_prev = hb_run[:, cs]
                rhs = jnp.concatenate(
                    [xc_ref[rs, cs], hf_s[c, :, cs], hb_prev.astype(bf16)], axis=0)
                res = _dot(lhs, rhs)
                y_s[rs, cs] = jnp.where(low_lanes, res[0:Q], res[Q:2 * Q])
                dec = jnp.where(low_row, decb[h0:h0 + 1], decb[h0 + 1:h0 + 2])
                hb_run[:, cs] = dec * hb_prev + sb_s[c, :, cs].astype(f32)

    @pl.when(ph == 0)
    def _phase0():
        @pl.when(step == 0)
        def _():
            hf_run[...] = jnp.zeros_like(hf_run)

        for sub in range(per):
            chunk_states(step * per + sub, slice(sub * Q, (sub + 1) * Q))

    @pl.when(ph == 1)
    def _phase1():
        @pl.when(step == 0)
        def _():
            hb_run[...] = jnp.zeros_like(hb_run)

        back_step = nsteps - 1 - step
        for sub in reversed(range(per)):
            chunk_outputs(back_step * per + sub, slice(sub * Q, (sub + 1) * Q))

        zf = z_ref[...].astype(f32)
        y = y_s[...] * (zf / (1.0 + jnp.exp2(zf * (-LOG2E))))
        gw = SSD_WIDTH // SSD_GROUPS
        for g in range(SSD_GROUPS):
            yg = y[:, g * gw:(g + 1) * gw]
            ms = jnp.mean(yg * yg, axis=-1, keepdims=True)
            y_ref[:, g * gw:(g + 1) * gw] = (
                yg * lax.rsqrt(ms + EPS) * nw_ref[:, g * gw:(g + 1) * gw]).astype(bf16)


def _ssd(dmat, xc, rows, rowsl, cols, dtT, z, norm_w, batch, seq):
    Q = SSD_CHUNK
    nc = seq // Q
    rb = SSD_CHUNKS_PER_STEP * Q
    ns = seq // rb
    last = ns - 1

    def both_map(b, ph, s):
        return (b * ns + s * (1 - ph) + (last - s) * ph, 0)

    def fwd_map(b, ph, s):
        return (b * ns + s * (1 - ph) + last * ph, 0)

    def back_map(b, ph, s):
        return (b * ns + last - s * ph, 0)

    const = lambda b, ph, s: (0, 0)
    return pl.pallas_call(
        _ssd_kernel,
        grid=(batch, 2, ns),
        in_specs=[
            pl.BlockSpec((rb, SSD_XBC_W), both_map),
            pl.BlockSpec((rb, LANES), both_map),
            pl.BlockSpec((rb, LANES), back_map),
            pl.BlockSpec((rb, LANES), back_map),
            pl.BlockSpec((rb, LANES), fwd_map),
            pl.BlockSpec((rb, SSD_WIDTH), back_map),
            pl.BlockSpec((1, SSD_WIDTH), const),
            pl.BlockSpec((SSD_HEADS, Q, Q), lambda b, ph, s: (0, 0, 0)),
        ],
        out_specs=pl.BlockSpec((rb, SSD_WIDTH), back_map),
        out_shape=jax.ShapeDtypeStruct((batch * seq, SSD_WIDTH), bf16),
        scratch_shapes=[
            pltpu.VMEM((nc, SSD_STATE, SSD_WIDTH), bf16),
            pltpu.VMEM((nc, SSD_STATE, SSD_WIDTH), bf16),
            pltpu.VMEM((SSD_STATE, SSD_WIDTH), f32),
            pltpu.VMEM((SSD_STATE, SSD_WIDTH), f32),
            pltpu.VMEM((rb, SSD_WIDTH), f32),
        ],
        compiler_params=pltpu.CompilerParams(
            dimension_semantics=("arbitrary", "arbitrary", "arbitrary"),
            vmem_limit_bytes=VMEM_LIMIT),
        name="ssd",
    )(xc, rows, rowsl, cols, dtT, z, norm_w, dmat)


def _mem_kv_kernel(mem_ref, nw_ref, w_ref, knw_ref, k_ref, v_ref):
    h = _rms_rows(mem_ref[...], nw_ref[...]).astype(bf16)
    kv = _dot(h, w_ref[...])
    for c in range(XATT_HEADS):
        sl = slice(c * XATT_HEAD_DIM, (c + 1) * XATT_HEAD_DIM)
        t = kv[:, sl]
        k_ref[:, sl] = (t * lax.rsqrt(jnp.mean(t * t, axis=-1, keepdims=True) + EPS)
                        * knw_ref[...]).astype(bf16)
    v_ref[...] = kv[:, XATT_W:].astype(bf16)


def _mem_kv(mem2, mem_norm_w, w_kv, xk_norm_w, batch, mem_len):
    row = lambda b: (b, 0)
    const = lambda b: (0, 0)
    return pl.pallas_call(
        _mem_kv_kernel,
        grid=(batch,),
        in_specs=[
            pl.BlockSpec((mem_len, D_MODEL), row),
            pl.BlockSpec((1, D_MODEL), const),
            pl.BlockSpec((D_MODEL, 2 * XATT_W), const),
            pl.BlockSpec((1, XATT_HEAD_DIM), const),
        ],
        out_specs=[pl.BlockSpec((mem_len, XATT_W), row)] * 2,
        out_shape=[jax.ShapeDtypeStruct((batch * mem_len, XATT_W), bf16)] * 2,
        compiler_params=pltpu.CompilerParams(
            dimension_semantics=("arbitrary",), vmem_limit_bytes=VMEM_LIMIT),
        name="mem_kv",
    )(mem2, mem_norm_w, w_kv, xk_norm_w)


def _xattn_kernel(q_ref, k_ref, v_ref, o_ref):
    for hd in range(XATT_HEADS):
        sl = slice(hd * XATT_HEAD_DIM, (hd + 1) * XATT_HEAD_DIM)
        s = _dot_nt(q_ref[:, sl], k_ref[:, sl])
        m = jnp.max(s, axis=-1, keepdims=True)
        p = jnp.exp2(s - m)
        den = jnp.sum(p, axis=-1, keepdims=True)
        o_ref[:, sl] = (_dot(p.astype(bf16), v_ref[:, sl]) / den).astype(bf16)


def _xattn(qx, km, vm, batch, seq, mem_len, tq):
    nq = seq // tq
    qmap = lambda b, i: (b * nq + i, 0)
    mmap = lambda b, i: (b, 0)
    return pl.pallas_call(
        _xattn_kernel,
        grid=(batch, nq),
        in_specs=[
            pl.BlockSpec((tq, XATT_W), qmap),
            pl.BlockSpec((mem_len, XATT_W), mmap),
            pl.BlockSpec((mem_len, XATT_W), mmap),
        ],
        out_specs=pl.BlockSpec((tq, XATT_W), qmap),
        out_shape=jax.ShapeDtypeStruct((batch * seq, XATT_W), bf16),
        compiler_params=pltpu.CompilerParams(
            dimension_semantics=("arbitrary", "arbitrary"), vmem_limit_bytes=VMEM_LIMIT),
        name="xattn",
    )(qx, km, vm)


FF_CHUNK = 1024


def _out_mlp_kernel(x_ref, a_ref, s_ref, c_ref, wo_ref, nw_ref, wu_ref, wd_ref, o_ref):
    mix = (_dot(a_ref[...], wo_ref[0:ATT_Q_W, :])
           + _dot(s_ref[...], wo_ref[ATT_Q_W:ATT_Q_W + SSD_WIDTH, :])
           + _dot(c_ref[...], wo_ref[ATT_Q_W + SSD_WIDTH:, :]))
    x1 = x_ref[...] + mix
    h = _rms_rows(x1, nw_ref[...]).astype(bf16)
    acc = x1
    for j in range(D_FF // FF_CHUNK):
        u = _dot(h, wu_ref[:, j * FF_CHUNK:(j + 1) * FF_CHUNK])
        r = jnp.maximum(u, 0.0)
        acc = acc + _dot((r * r).astype(bf16), wd_ref[j * FF_CHUNK:(j + 1) * FF_CHUNK, :])
    o_ref[...] = acc


def _out_mlp(x2, attn, ssd, xatt, w_out, norm_w, w_up, w_down, tm):
    tokens = x2.shape[0]
    row = lambda i: (i, 0)
    const = lambda i: (0, 0)
    resident = functools.partial(pl.BlockSpec, index_map=const, pipeline_mode=pl.Buffered(1))
    return pl.pallas_call(
        _out_mlp_kernel,
        grid=(tokens // tm,),
        in_specs=[
            pl.BlockSpec((tm, D_MODEL), row),
            pl.BlockSpec((tm, ATT_Q_W), row),
            pl.BlockSpec((tm, SSD_WIDTH), row),
            pl.BlockSpec((tm, XATT_W), row),
            resident((D_MIX, D_MODEL)),
            pl.BlockSpec((1, D_MODEL), const),
            resident((D_MODEL, D_FF)),
            resident((D_FF, D_MODEL)),
        ],
        out_specs=pl.BlockSpec((tm, D_MODEL), row),
        out_shape=jax.ShapeDtypeStruct((tokens, D_MODEL), f32),
        compiler_params=pltpu.CompilerParams(
            dimension_semantics=("arbitrary",), vmem_limit_bytes=VMEM_LIMIT),
        name="out_mlp",
    )(x2, attn, ssd, xatt, w_out, norm_w, w_up, w_down)


def _rope_tables(seq):
    half = ROPE_DIM // 2
    inv = ROPE_THETA ** (-jnp.arange(0, ROPE_DIM, 2, dtype=f32) / ROPE_DIM)
    ang = jnp.arange(seq, dtype=f32)[:, None] * inv[None, :]
    cos = jnp.cos(ang)
    sin = jnp.sin(ang)
    ones = jnp.ones((seq, ATT_HEAD_DIM - ROPE_DIM), f32)
    zeros_h = jnp.zeros((seq, half), f32)
    zeros_p = jnp.zeros((seq, ATT_HEAD_DIM - ROPE_DIM), f32)
    cos_head = jnp.concatenate([cos, cos, ones], axis=1)
    lo_head = jnp.concatenate([-sin, zeros_h, zeros_p], axis=1)
    hi_head = jnp.concatenate([zeros_h, sin, zeros_p], axis=1)
    rep = LANES // ATT_HEAD_DIM
    return (jnp.tile(cos_head, (1, rep)), jnp.tile(lo_head, (1, rep)),
            jnp.tile(hi_head, (1, rep)))


def _layer(x, mem, norm_mix_w, w_in, q_norm_w, k_norm_w, attn_sink, conv_w, conv_b,
           dt_bias_f, dt_bias_b, a_log_f, a_log_b, ssd_d, ssd_norm_w, mem_norm_w,
           w_mem_kv, xq_norm_w, xk_norm_w, w_out, norm_mlp_w, w_mlp_up, w_mlp_down, tables):
    batch, seq, _ = x.shape
    mem_len = mem.shape[1]
    tokens = batch * seq
    tm = 512

    s = np.cumsum([ATT_Q_W, ATT_KV_W, ATT_KV_W, SSD_WIDTH, SSD_XBC_W, SSD_DT_W, XATT_W])
    piece = lambda lo, hi: w_in[:, lo:hi].astype(bf16)
    w_q, w_k, w_v = piece(0, s[0]), piece(s[0], s[1]), piece(s[1], s[2])
    w_z, w_xbc, w_dt, w_qx = (piece(s[2], s[3]), piece(s[3], s[4]),
                              piece(s[4], s[5]), piece(s[5], s[6]))
    hd = ATT_HEAD_DIM
    dup = lambda w: jnp.concatenate(
        [w[:, g * hd:(g + 1) * hd] for g in range(ATT_KV_HEADS) for _ in range(2)], axis=1)
    w_dt_p = jnp.concatenate([w_dt, jnp.zeros((D_MODEL, DT_PAD_W - SSD_DT_W), bf16)], axis=1)
    weights = (w_q, dup(w_k), dup(w_v), w_z, w_xbc, w_qx, w_dt_p)

    pad = jnp.zeros((DT_PAD_W - SSD_DT_W,), f32)
    dt_bias_row = jnp.concatenate([dt_bias_f, dt_bias_b, pad]).reshape(1, DT_PAD_W)
    a_row = jnp.concatenate([-jnp.exp(a_log_f), -jnp.exp(a_log_b), pad]).reshape(1, DT_PAD_W)
    conv_w8 = jnp.concatenate([conv_w, jnp.zeros((8 - SSD_CONV, SSD_XBC_W), f32)], axis=0)
    dmat = (ssd_d.astype(f32)[:, None, None] * jnp.eye(SSD_CHUNK, dtype=f32)).astype(bf16)

    x2 = x.reshape(tokens, D_MODEL)
    cos_t, sinlo_t, sinhi_t = tables
    q, kdup, vt, z, xc, qx, rows, rowsl, cols, dtT = _in_proj(
        x2, norm_mix_w.reshape(1, D_MODEL), weights,
        jnp.tile(q_norm_w, ATT_HEADS).reshape(1, ATT_Q_W),
        jnp.tile(k_norm_w, KDUP_W // hd).reshape(1, KDUP_W),
        xq_norm_w.reshape(1, XATT_HEAD_DIM), cos_t, sinlo_t, sinhi_t,
        conv_w8, conv_b.reshape(1, SSD_XBC_W), dt_bias_row, a_row, seq, tm)

    attn = _attention(attn_sink.astype(f32), q, kdup, vt, batch, seq)
    ssd = _ssd(dmat, xc, rows, rowsl, cols, dtT, z,
               ssd_norm_w.reshape(1, SSD_WIDTH), batch, seq)

    km, vm = _mem_kv(mem.reshape(batch * mem_len, D_MODEL), mem_norm_w.reshape(1, D_MODEL),
                     w_mem_kv.astype(bf16), xk_norm_w.reshape(1, XATT_HEAD_DIM), batch, mem_len)
    xatt = _xattn(qx, km, vm, batch, seq, mem_len, min(XATT_TQ, seq))

    out = _out_mlp(x2, attn, ssd, xatt, w_out.astype(bf16), norm_mlp_w.reshape(1, D_MODEL),
                   w_mlp_up.astype(bf16), w_mlp_down.astype(bf16), tm)
    return out.reshape(batch, seq, D_MODEL)


def kernel(x, mem, norm_mix_w, w_in, q_norm_w, k_norm_w, attn_sink, conv_w, conv_b,
           dt_bias_f, dt_bias_b, a_log_f, a_log_b, ssd_d, ssd_norm_w, mem_norm_w,
           w_mem_kv, xq_norm_w, xk_norm_w, w_out, norm_mlp_w, w_mlp_up, w_mlp_down):
    depth = w_in.shape[0]
    tables = _rope_tables(x.shape[1])
    for i in range(depth):
        x = _layer(x, mem, norm_mix_w[i], w_in[i], q_norm_w[i], k_norm_w[i], attn_sink[i],
                   conv_w[i], conv_b[i], dt_bias_f[i], dt_bias_b[i], a_log_f[i], a_log_b[i],
                   ssd_d[i], ssd_norm_w[i], mem_norm_w[i], w_mem_kv[i], xq_norm_w[i],
                   xk_norm_w[i], w_out[i], norm_mlp_w[i], w_mlp_up[i], w_mlp_down[i], tables)
    return x
```

```python
import functools

import numpy as np
import jax
import jax.numpy as jnp
from jax import lax
from jax.experimental import pallas as pl
from jax.experimental.pallas import tpu as pltpu

D_MODEL = 1024
EPS = 1e-6

ATT_HEADS = 8
ATT_KV_HEADS = 2
ATT_HEAD_DIM = 64
ATT_Q_W = ATT_HEADS * ATT_HEAD_DIM
ATT_KV_W = ATT_KV_HEADS * ATT_HEAD_DIM
WINDOW = 128
ATT_BLOCK = 128
ATT_BLOCKS_PER_STEP = 4
ROPE_THETA = 500000.0
ROPE_DIM = ATT_HEAD_DIM // 4

SSD_HEADS = 16
SSD_HEAD_DIM = 64
SSD_WIDTH = SSD_HEADS * SSD_HEAD_DIM
SSD_GROUPS = 2
SSD_STATE = 128
SSD_CONV = 5
SSD_CHUNK = 128
SSD_CHUNKS_PER_STEP = 4
SSD_XBC_W = SSD_WIDTH + 2 * SSD_GROUPS * SSD_STATE
SSD_DT_W = 2 * SSD_HEADS

XATT_HEADS = 4
XATT_HEAD_DIM = 128
XATT_W = XATT_HEADS * XATT_HEAD_DIM
XATT_TQ = 1024

D_FF = 4 * D_MODEL
D_MIX = 2 * D_MODEL

LANES = 128
BF16_SUBLANES = 16
VMEM_LIMIT = 56 * 1024 * 1024

KDUP_W = 2 * ATT_KV_W
DT_PAD_W = LANES
C_Q = 0
C_K = C_Q + ATT_Q_W
C_V = C_K + KDUP_W
C_Z = C_V + KDUP_W
C_XBC = C_Z + SSD_WIDTH
C_QX = C_XBC + SSD_XBC_W
C_DT = C_QX + XATT_W
D_IN_P = C_DT + DT_PAD_W

CONV_HALO = BF16_SUBLANES
C_B = SSD_WIDTH
C_C = SSD_WIDTH + SSD_GROUPS * SSD_STATE

NEG_BIG = -1e30
LOG2E = 1.4426950408889634

bf16 = jnp.bfloat16
f32 = jnp.float32


def _dot(a, b):
    return jnp.dot(a, b, preferred_element_type=f32)


def _dot_nt(a, b):
    return lax.dot_general(a, b, (((1,), (1,)), ((), ())), preferred_element_type=f32)


def _split3(x):
    x1 = x.astype(bf16)
    r1 = x - x1.astype(f32)
    x2 = r1.astype(bf16)
    r2 = r1 - x2.astype(f32)
    return x1, x2, r2.astype(bf16)


def _dot3(x, m01):
    x1, x2, x3 = _split3(x)
    return _dot(x1, m01) + _dot(x2, m01) + _dot(x3, m01)


def _iota(shape, dim):
    return lax.broadcasted_iota(jnp.int32, shape, dim)


def _segment_ones(width, seg):
    r = _iota((width, width), 0) // seg
    c = _iota((width, width), 1) // seg
    return jnp.where(r == c, 1.0, 0.0).astype(bf16)


def _head_mean_sq(t, seg):
    ones = _segment_ones(t.shape[1], seg)
    return _dot((t * t).astype(bf16), ones) * (1.0 / seg)


def _rope(t, cos, sin_lo, sin_hi):
    half = ROPE_DIM // 2
    return (t * cos + pltpu.roll(t, LANES - half, 1) * sin_lo
            + pltpu.roll(t, half, 1) * sin_hi)


def _rms_rows(x, w):
    ms = jnp.mean(x * x, axis=-1, keepdims=True)
    return (x * lax.rsqrt(ms + EPS)) * w


def _in_proj_body(pos_blocks, x_ref, xp_ref, xn_ref, nw_ref,
                  wq_ref, wk_ref, wv_ref, wz_ref, wxbc_ref, wqx_ref, wdt_ref, qnw_ref, knw_ref,
                  xqnw_ref, cos_ref, sinlo_ref, sinhi_ref, cw_ref, cb_ref, dtb_ref, arow_ref,
                  q_ref, k_ref, v_ref, z_ref, xc_ref, qx_ref,
                  rows_ref, rowsl_ref, cols_ref, dtT_ref, conv_s):
    i = pl.program_id(0)
    tm = x_ref.shape[0]
    si = i % pos_blocks
    nw = nw_ref[...]
    h = _rms_rows(x_ref[...], nw).astype(bf16)

    cos = cos_ref[...]
    sin_lo = sinlo_ref[...]
    sin_hi = sinhi_ref[...]

    pq = _dot(h, wq_ref[...])
    qn = pq * lax.rsqrt(_head_mean_sq(pq, ATT_HEAD_DIM) + EPS) * qnw_ref[...]
    for c in range(ATT_Q_W // LANES):
        sl = slice(c * LANES, (c + 1) * LANES)
        q_ref[:, sl] = (_rope(qn[:, sl], cos, sin_lo, sin_hi)
                        * (ATT_HEAD_DIM ** -0.5 * LOG2E)).astype(bf16)

    pk = _dot(h, wk_ref[...])
    kn = pk * lax.rsqrt(_head_mean_sq(pk, ATT_HEAD_DIM) + EPS) * knw_ref[...]
    for c in range(KDUP_W // LANES):
        sl = slice(c * LANES, (c + 1) * LANES)
        k_ref[:, sl] = _rope(kn[:, sl], cos, sin_lo, sin_hi).astype(bf16)

    v_ref[...] = _dot(h, wv_ref[...]).T.astype(bf16)
    z_ref[...] = _dot(h, wz_ref[...]).astype(bf16)

    pqx = _dot(h, wqx_ref[...])
    for c in range(XATT_HEADS):
        sl = slice(c * XATT_HEAD_DIM, (c + 1) * XATT_HEAD_DIM)
        t = pqx[:, sl]
        tn = t * lax.rsqrt(jnp.mean(t * t, axis=-1, keepdims=True) + EPS) * xqnw_ref[...]
        qx_ref[:, sl] = (tn * (XATT_HEAD_DIM ** -0.5 * LOG2E)).astype(bf16)

    hp = _rms_rows(xp_ref[...], nw)
    hn = _rms_rows(xn_ref[...], nw)
    hp = jnp.where(si > 0, hp, jnp.zeros_like(hp)).astype(bf16)
    hn = jnp.where(si < pos_blocks - 1, hn, jnp.zeros_like(hn)).astype(bf16)
    h_ext = jnp.concatenate([hp, h, hn], axis=0)
    conv_s[...] = _dot(h_ext, wxbc_ref[...])
    acc = jnp.broadcast_to(cb_ref[...], (tm, SSD_XBC_W))
    for k in range(SSD_CONV):
        off = CONV_HALO - SSD_CONV // 2 + k
        acc = acc + conv_s[off:off + tm, :] * cw_ref[k:k + 1, :]
    xc_ref[...] = (acc / (1.0 + jnp.exp2(acc * (-LOG2E)))).astype(bf16)

    Q = SSD_CHUNK
    dt = _dot(h, wdt_ref[...]) + dtb_ref[...]
    dt = jnp.maximum(dt, 0.0) + jnp.log1p(jnp.exp(-jnp.abs(dt)))
    a = dt * arow_ref[...]
    li = _iota((Q, Q), 0)
    lj = _iota((Q, Q), 1)
    incl_le = jnp.where(li <= lj, 1.0, 0.0).astype(bf16)
    incl_ge = jnp.where(li >= lj, 1.0, 0.0).astype(bf16)
    pad_rows = jnp.zeros((Q - 2 * SSD_HEADS, Q), f32)
    for ch in range(tm // Q):
        r = slice(ch * Q, (ch + 1) * Q)
        aT = a[r].T
        dtT = dt[r].T
        pre = _dot3(aT[0:SSD_HEADS], incl_le)
        suf = _dot3(aT[SSD_HEADS:2 * SSD_HEADS], incl_ge)
        rows = jnp.concatenate([pre, suf, pad_rows], axis=0) * LOG2E
        rows_ref[r, :] = rows
        rowsl_ref[r, :] = rows - jnp.log(dtT) * LOG2E
        cols_ref[r, :] = rows.T
        dtT_ref[r, :] = dtT


def _in_proj(x2, norm_w, weights, qnw, knw, xqnw, cos_t, sinlo_t, sinhi_t,
             conv_w8, conv_b, dt_bias_row, a_row, seq, tm):
    tokens = x2.shape[0]
    pos_blocks = seq // tm
    hpt = tm // CONV_HALO
    n_halo = tokens // CONV_HALO
    row = lambda i: (i, 0)
    const = lambda i: (0, 0)
    pos = lambda i: (i % pos_blocks, 0)
    prev = lambda i: (jnp.maximum(i * hpt - 1, 0), 0)
    nxt = lambda i: (jnp.minimum((i + 1) * hpt, n_halo - 1), 0)
    out_widths = [ATT_Q_W, KDUP_W, KDUP_W, SSD_WIDTH, SSD_XBC_W, XATT_W,
                  LANES, LANES, LANES, LANES]
    out_dtypes = [bf16, bf16, bf16, bf16, bf16, bf16, f32, f32, f32, f32]
    out_specs = [pl.BlockSpec((tm, w), row) for w in out_widths]
    out_shape = [jax.ShapeDtypeStruct((tokens, w), d) for w, d in zip(out_widths, out_dtypes)]
    out_specs[2] = pl.BlockSpec((KDUP_W, tm), lambda i: (i // pos_blocks, i % pos_blocks))
    out_shape[2] = jax.ShapeDtypeStruct((tokens // seq * KDUP_W, seq), bf16)
    return pl.pallas_call(
        functools.partial(_in_proj_body, pos_blocks),
        grid=(tokens // tm,),
        in_specs=[
            pl.BlockSpec((tm, D_MODEL), row),
            pl.BlockSpec((CONV_HALO, D_MODEL), prev),
            pl.BlockSpec((CONV_HALO, D_MODEL), nxt),
            pl.BlockSpec((1, D_MODEL), const),
            *[pl.BlockSpec(w.shape, const, pipeline_mode=pl.Buffered(1)) for w in weights],
            pl.BlockSpec((1, ATT_Q_W), const),
            pl.BlockSpec((1, KDUP_W), const),
            pl.BlockSpec((1, XATT_HEAD_DIM), const),
            pl.BlockSpec((tm, LANES), pos),
            pl.BlockSpec((tm, LANES), pos),
            pl.BlockSpec((tm, LANES), pos),
            pl.BlockSpec((8, SSD_XBC_W), const),
            pl.BlockSpec((1, SSD_XBC_W), const),
            pl.BlockSpec((1, DT_PAD_W), const),
            pl.BlockSpec((1, DT_PAD_W), const),
        ],
        out_specs=out_specs,
        out_shape=out_shape,
        scratch_shapes=[pltpu.VMEM((tm + 2 * CONV_HALO, SSD_XBC_W), f32)],
        compiler_params=pltpu.CompilerParams(
            dimension_semantics=("arbitrary",), vmem_limit_bytes=VMEM_LIMIT),
        name="in_proj",
    )(x2, x2, x2, norm_w, *weights, qnw, knw, xqnw, cos_t, sinlo_t, sinhi_t,
      conv_w8, conv_b, dt_bias_row, a_row)


def _attn_kernel(sink_ref, mfirst_ref, mmid_ref, mlast_ref, q_ref, kp_ref, kc_ref, kn_ref,
                 vp_ref, vc_ref, vn_ref, o_ref):
    blk = ATT_BLOCK
    per = ATT_BLOCKS_PER_STEP
    heads_per_kv = ATT_HEADS // ATT_KV_HEADS
    cols = heads_per_kv * blk
    low_lanes = _iota((blk, LANES), 1) < ATT_HEAD_DIM
    head_of_col = _iota((1, cols), 1) // blk
    masks = [mfirst_ref] + [mmid_ref] * (per - 2) + [mlast_ref]
    half = LANES // 2

    kblocks, vblocks, sinks = [], [], []
    for g in range(ATT_KV_HEADS):
        gs = slice(g * LANES, (g + 1) * LANES)
        kblocks.append([kp_ref[:, gs]] + [kc_ref[i * blk:(i + 1) * blk, gs] for i in range(per)]
                       + [kn_ref[:, gs]])
        vblocks.append([vp_ref[gs, :]] + [vc_ref[gs, i * blk:(i + 1) * blk] for i in range(per)]
                       + [vn_ref[gs, :]])
        sink = jnp.zeros((1, cols), f32)
        for j in range(heads_per_kv):
            sink = jnp.where(head_of_col == j, sink_ref[g * heads_per_kv + j] * LOG2E, sink)
        sinks.append(sink)

    def scores(g, t):
        qrows = slice(t * blk, (t + 1) * blk)
        kcat = jnp.concatenate(kblocks[g][t:t + 3], axis=0)
        qs = []
        for j in range(heads_per_kv):
            hd = g * heads_per_kv + j
            pair = q_ref[qrows, (hd // 2) * LANES:(hd // 2 + 1) * LANES]
            keep = low_lanes if hd % 2 == 0 else jnp.logical_not(low_lanes)
            qs.append(jnp.where(keep, pair, jnp.zeros_like(pair)))
        qg = jnp.concatenate(qs, axis=0)
        return _dot_nt(kcat, qg)

    def softmax(g, t, st):
        st = st + masks[t][...]
        m = jnp.maximum(jnp.max(st, axis=0, keepdims=True), sinks[g])
        p = jnp.exp2(st - m)
        den = jnp.sum(p, axis=0, keepdims=True) + jnp.exp2(sinks[g] - m)
        return p.astype(bf16), den

    def values(g, t, p, den):
        vt = jnp.concatenate(vblocks[g][t:t + 3], axis=1)
        return _dot(vt, p) / den

    def emit(g, t, ot):
        qrows = slice(t * blk, (t + 1) * blk)
        for jp in range(heads_per_kv // 2):
            even = ot[0:half, (2 * jp) * blk:(2 * jp + 1) * blk]
            odd = ot[half:, (2 * jp + 1) * blk:(2 * jp + 2) * blk]
            pr = g * (heads_per_kv // 2) + jp
            o_ref[qrows, pr * LANES:(pr + 1) * LANES] = (
                jnp.concatenate([even, odd], axis=0).T.astype(bf16))

    for g in range(ATT_KV_HEADS):
        for t in range(ATT_BLOCKS_PER_STEP):
            p, den = softmax(g, t, scores(g, t))
            emit(g, t, values(g, t, p, den))
def _attn_masks():
    blk = ATT_BLOCK
    cols = (ATT_HEADS // ATT_KV_HEADS) * blk
    key = np.arange(3 * blk)[:, None]
    qi = np.arange(cols)[None, :] % blk
    band = (key >= qi) & (key <= qi + 2 * WINDOW)
    variants = [band & (key >= blk), band, band & (key < 2 * blk)]
    return jnp.asarray(np.where(np.stack(variants), 0.0, NEG_BIG), dtype=f32)


def _attention(sink, q, kdup, vt, batch, seq):
    blk = ATT_BLOCK
    nb = seq // blk
    per = ATT_BLOCKS_PER_STEP
    ns = nb // per
    cols = (ATT_HEADS // ATT_KV_HEADS) * blk
    masks = _attn_masks()
    cur = lambda b, s: (b * ns + s, 0)
    prev = lambda b, s: (b * nb + jnp.maximum(per * s - 1, 0), 0)
    nxt = lambda b, s: (b * nb + jnp.minimum(per * s + per, nb - 1), 0)
    vcur = lambda b, s: (b, s)
    vprev = lambda b, s: (b, jnp.maximum(per * s - 1, 0))
    vnxt = lambda b, s: (b, jnp.minimum(per * s + per, nb - 1))
    mfirst = lambda b, s: (jnp.where(s == 0, 0, 1), 0, 0)
    mmid = lambda b, s: (1, 0, 0)
    mlast = lambda b, s: (jnp.where(s == ns - 1, 2, 1), 0, 0)
    return pl.pallas_call(
        _attn_kernel,
        grid=(batch, ns),
        in_specs=[
            pl.BlockSpec(memory_space=pltpu.SMEM),
            pl.BlockSpec((None, 3 * blk, cols), mfirst),
            pl.BlockSpec((None, 3 * blk, cols), mmid),
            pl.BlockSpec((None, 3 * blk, cols), mlast),
            pl.BlockSpec((per * blk, ATT_Q_W), cur),
            pl.BlockSpec((blk, KDUP_W), prev),
            pl.BlockSpec((per * blk, KDUP_W), cur),
            pl.BlockSpec((blk, KDUP_W), nxt),
            pl.BlockSpec((KDUP_W, blk), vprev),
            pl.BlockSpec((KDUP_W, per * blk), vcur),
            pl.BlockSpec((KDUP_W, blk), vnxt),
        ],
        out_specs=pl.BlockSpec((per * blk, ATT_Q_W), cur),
        out_shape=jax.ShapeDtypeStruct((batch * seq, ATT_Q_W), bf16),
        compiler_params=pltpu.CompilerParams(
            dimension_semantics=("arbitrary", "arbitrary"), vmem_limit_bytes=VMEM_LIMIT),
        name="attention",
    )(sink, masks, masks, masks, q, kdup, kdup, kdup, vt, vt, vt)


def _ssd_kernel(xc_ref, rows_ref, rowsl_ref, cols_ref, dtT_ref, dmat_ref, y_ref,
                hf_s, sb_s, hf_run, hb_run):
    ph = pl.program_id(1)
    step = pl.program_id(2)
    nsteps = pl.num_programs(2)
    Q = SSD_CHUNK
    per = SSD_CHUNKS_PER_STEP
    heads_per_group = SSD_HEADS // SSD_GROUPS
    low_lanes = _iota((Q, LANES), 1) < SSD_HEAD_DIM
    low_row = _iota((1, LANES), 1) < SSD_HEAD_DIM
    li = _iota((Q, Q), 0)
    si = _iota((Q, Q), 1)

    def chunk_states(c, rs):
        rows = rows_ref[rs, :]
        dtT = dtT_ref[rs, :]
        pre = rows[0:SSD_HEADS]
        suf = rows[SSD_HEADS:2 * SSD_HEADS]
        pre_end = jnp.broadcast_to(pre[:, Q - 1:Q], (SSD_HEADS, Q))
        suf_end = jnp.broadcast_to(suf[:, 0:1], (SSD_HEADS, Q))
        wf = jnp.exp2(pre_end - pre) * dtT[0:SSD_HEADS]
        wb = jnp.exp2(suf_end - suf) * dtT[SSD_HEADS:2 * SSD_HEADS]
        decf = jnp.exp2(pre_end)

        for g in range(SSD_GROUPS):
            bT = xc_ref[rs, C_B + g * SSD_STATE:C_B + (g + 1) * SSD_STATE].astype(f32).T
            for m in range(heads_per_group // 2):
                h0 = g * heads_per_group + 2 * m
                h1 = h0 + 1
                lhs = jnp.concatenate(
                    [bT * wf[h0:h0 + 1], bT * wb[h0:h0 + 1],
                     bT * wf[h1:h1 + 1], bT * wb[h1:h1 + 1]], axis=0).astype(bf16)
                cs = slice(h0 * SSD_HEAD_DIM, h0 * SSD_HEAD_DIM + LANES)
                res = _dot(lhs, xc_ref[rs, cs])
                s_f = jnp.where(low_lanes, res[0:Q], res[2 * Q:3 * Q])
                s_b = jnp.where(low_lanes, res[Q:2 * Q], res[3 * Q:4 * Q])
                dec = jnp.where(low_row, decf[h0:h0 + 1], decf[h1:h1 + 1])
                prev = hf_run[:, cs]
                hf_s[c, :, cs] = prev.astype(bf16)
                hf_run[:, cs] = dec * prev + s_f
                sb_s[c, :, cs] = s_b.astype(bf16)

    def chunk_outputs(c, rs):
        rows = rows_ref[rs, :]
        suf = rows[SSD_HEADS:2 * SSD_HEADS]
        cols = cols_ref[rs, :]
        rowsl = rowsl_ref[rs, :]
        decb = jnp.exp2(jnp.broadcast_to(suf[:, 0:1], (SSD_HEADS, Q)))
        lower = li >= si
        upper = si >= li

        for g in range(SSD_GROUPS):
            bg = xc_ref[rs, C_B + g * SSD_STATE:C_B + (g + 1) * SSD_STATE]
            cg = xc_ref[rs, C_C + g * SSD_STATE:C_C + (g + 1) * SSD_STATE]
            cbm = _dot_nt(cg, bg).astype(bf16)
            for m in range(heads_per_group // 2):
                h0 = g * heads_per_group + 2 * m
                parts = []
                for hd in (h0, h0 + 1):
                    hb = SSD_HEADS + hd
                    colf = jnp.broadcast_to(cols[:, hd:hd + 1], (Q, Q))
                    colb = jnp.broadcast_to(cols[:, hb:hb + 1], (Q, Q))
                    ef = jnp.exp2(jnp.where(lower, colf - rowsl[hd:hd + 1], NEG_BIG)).astype(bf16)
                    eb = jnp.exp2(jnp.where(upper, colb - rowsl[hb:hb + 1], NEG_BIG)).astype(bf16)
                    gm = cbm * (ef + eb) + dmat_ref[hd]
                    parts.append(jnp.concatenate(
                        [gm, cg * jnp.exp2(colf).astype(bf16), cg * jnp.exp2(colb).astype(bf16)],
                        axis=1))
                lhs = jnp.concatenate(parts, axis=0)
                cs = slice(h0 * SSD_HEAD_DIM, h0 * SSD_HEAD_DIM + LANES)
                hb_prev = hb_run[:, cs]
                rhs = jnp.concatenate(
                    [xc_ref[rs, cs], hf_s[c, :, cs], hb_prev.astype(bf16)], axis=0)
                res = _dot(lhs, rhs)
                y_ref[rs, cs] = jnp.where(low_lanes, res[0:Q], res[Q:2 * Q]).astype(bf16)
                dec = jnp.where(low_row, decb[h0:h0 + 1], decb[h0 + 1:h0 + 2])
                hb_run[:, cs] = dec * hb_prev + sb_s[c, :, cs].astype(f32)

    @pl.when(ph == 0)
    def _phase0():
        @pl.when(step == 0)
        def _():
            hf_run[...] = jnp.zeros_like(hf_run)

        for sub in range(per):
            chunk_states(step * per + sub, slice(sub * Q, (sub + 1) * Q))

    @pl.when(ph == 1)
    def _phase1():
        @pl.when(step == 0)
        def _():
            hb_run[...] = jnp.zeros_like(hb_run)

        back_step = nsteps - 1 - step
        for sub in reversed(range(per)):
            chunk_outputs(back_step * per + sub, slice(sub * Q, (sub + 1) * Q))


def _ssd(dmat, xc, rows, rowsl, cols, dtT, batch, seq):
    Q = SSD_CHUNK
    nc = seq // Q
    rb = SSD_CHUNKS_PER_STEP * Q
    ns = seq // rb
    last = ns - 1

    def both_map(b, ph, s):
        return (b * ns + s * (1 - ph) + (last - s) * ph, 0)

    def fwd_map(b, ph, s):
        return (b * ns + s * (1 - ph) + last * ph, 0)

    def back_map(b, ph, s):
        return (b * ns + last - s * ph, 0)

    return pl.pallas_call(
        _ssd_kernel,
        grid=(batch, 2, ns),
        in_specs=[
            pl.BlockSpec((rb, SSD_XBC_W), both_map),
            pl.BlockSpec((rb, LANES), both_map),
            pl.BlockSpec((rb, LANES), back_map),
            pl.BlockSpec((rb, LANES), back_map),
            pl.BlockSpec((rb, LANES), fwd_map),
            pl.BlockSpec((SSD_HEADS, Q, Q), lambda b, ph, s: (0, 0, 0)),
        ],
        out_specs=pl.BlockSpec((rb, SSD_WIDTH), back_map),
        out_shape=jax.ShapeDtypeStruct((batch * seq, SSD_WIDTH), bf16),
        scratch_shapes=[
            pltpu.VMEM((nc, SSD_STATE, SSD_WIDTH), bf16),
            pltpu.VMEM((nc, SSD_STATE, SSD_WIDTH), bf16),
            pltpu.VMEM((SSD_STATE, SSD_WIDTH), f32),
            pltpu.VMEM((SSD_STATE, SSD_WIDTH), f32),
        ],
        compiler_params=pltpu.CompilerParams(
            dimension_semantics=("arbitrary", "arbitrary", "arbitrary"),
            vmem_limit_bytes=VMEM_LIMIT),
        name="ssd",
    )(xc, rows, rowsl, cols, dtT, dmat)


def _mem_kv_kernel(mem_ref, nw_ref, w_ref, knw_ref, k_ref, v_ref):
    h = _rms_rows(mem_ref[...], nw_ref[...]).astype(bf16)
    kv = _dot(h, w_ref[...])
    for c in range(XATT_HEADS):
        sl = slice(c * XATT_HEAD_DIM, (c + 1) * XATT_HEAD_DIM)
        t = kv[:, sl]
        k_ref[:, sl] = (t * lax.rsqrt(jnp.mean(t * t, axis=-1, keepdims=True) + EPS)
                        * knw_ref[...]).astype(bf16)
    v_ref[...] = kv[:, XATT_W:].astype(bf16)


def _mem_kv(mem2, mem_norm_w, w_kv, xk_norm_w, batch, mem_len):
    row = lambda b: (b, 0)
    const = lambda b: (0, 0)
    return pl.pallas_call(
        _mem_kv_kernel,
        grid=(batch,),
        in_specs=[
            pl.BlockSpec((mem_len, D_MODEL), row),
            pl.BlockSpec((1, D_MODEL), const),
            pl.BlockSpec((D_MODEL, 2 * XATT_W), const),
            pl.BlockSpec((1, XATT_HEAD_DIM), const),
        ],
        out_specs=[pl.BlockSpec((mem_len, XATT_W), row)] * 2,
        out_shape=[jax.ShapeDtypeStruct((batch * mem_len, XATT_W), bf16)] * 2,
        compiler_params=pltpu.CompilerParams(
            dimension_semantics=("arbitrary",), vmem_limit_bytes=VMEM_LIMIT),
        name="mem_kv",
    )(mem2, mem_norm_w, w_kv, xk_norm_w)


def _xattn_kernel(q_ref, k_ref, v_ref, o_ref):
    for hd in range(XATT_HEADS):
        sl = slice(hd * XATT_HEAD_DIM, (hd + 1) * XATT_HEAD_DIM)
        s = _dot_nt(q_ref[:, sl], k_ref[:, sl])
        m = jnp.max(s, axis=-1, keepdims=True)
        p = jnp.exp2(s - m)
        den = jnp.sum(p, axis=-1, keepdims=True)
        o_ref[:, sl] = (_dot(p.astype(bf16), v_ref[:, sl]) / den).astype(bf16)


def _xattn(qx, km, vm, batch, seq, mem_len, tq):
    nq = seq // tq
    qmap = lambda b, i: (b * nq + i, 0)
    mmap = lambda b, i: (b, 0)
    return pl.pallas_call(
        _xattn_kernel,
        grid=(batch, nq),
        in_specs=[
            pl.BlockSpec((tq, XATT_W), qmap),
            pl.BlockSpec((mem_len, XATT_W), mmap),
            pl.BlockSpec((mem_len, XATT_W), mmap),
        ],
        out_specs=pl.BlockSpec((tq, XATT_W), qmap),
        out_shape=jax.ShapeDtypeStruct((batch * seq, XATT_W), bf16),
        compiler_params=pltpu.CompilerParams(
            dimension_semantics=("arbitrary", "arbitrary"), vmem_limit_bytes=VMEM_LIMIT),
        name="xattn",
    )(qx, km, vm)


FF_CHUNK = 1024


def _out_mlp_kernel(x_ref, a_ref, s_ref, z_ref, c_ref, wo_ref, snw_ref, nw_ref, wu_ref, wd_ref,
                    o_ref):
    zf = z_ref[...].astype(f32)
    y = s_ref[...].astype(f32) * (zf / (1.0 + jnp.exp2(zf * (-LOG2E))))
    gw = SSD_WIDTH // SSD_GROUPS
    mix = (_dot(a_ref[...], wo_ref[0:ATT_Q_W, :])
           + _dot(c_ref[...], wo_ref[ATT_Q_W + SSD_WIDTH:, :]))
    for g in range(SSD_GROUPS):
        yg = y[:, g * gw:(g + 1) * gw]
        ms = jnp.mean(yg * yg, axis=-1, keepdims=True)
        yn = (yg * lax.rsqrt(ms + EPS) * snw_ref[:, g * gw:(g + 1) * gw]).astype(bf16)
        mix = mix + _dot(yn, wo_ref[ATT_Q_W + g * gw:ATT_Q_W + (g + 1) * gw, :])
    x1 = x_ref[...] + mix
    h = _rms_rows(x1, nw_ref[...]).astype(bf16)
    acc = x1
    for j in range(D_FF // FF_CHUNK):
        u = _dot(h, wu_ref[:, j * FF_CHUNK:(j + 1) * FF_CHUNK])
        r = jnp.maximum(u, 0.0)
        acc = acc + _dot((r * r).astype(bf16), wd_ref[j * FF_CHUNK:(j + 1) * FF_CHUNK, :])
    o_ref[...] = acc


def _out_mlp(x2, attn, ssd, z, xatt, w_out, ssd_norm_w, norm_w, w_up, w_down, tm):
    tokens = x2.shape[0]
    row = lambda i: (i, 0)
    const = lambda i: (0, 0)
    resident = functools.partial(pl.BlockSpec, index_map=const, pipeline_mode=pl.Buffered(1))
    return pl.pallas_call(
        _out_mlp_kernel,
        grid=(tokens // tm,),
        in_specs=[
            pl.BlockSpec((tm, D_MODEL), row),
            pl.BlockSpec((tm, ATT_Q_W), row),
            pl.BlockSpec((tm, SSD_WIDTH), row),
            pl.BlockSpec((tm, SSD_WIDTH), row),
            pl.BlockSpec((tm, XATT_W), row),
            resident((D_MIX, D_MODEL)),
            pl.BlockSpec((1, SSD_WIDTH), const),
            pl.BlockSpec((1, D_MODEL), const),
            resident((D_MODEL, D_FF)),
            resident((D_FF, D_MODEL)),
        ],
        out_specs=pl.BlockSpec((tm, D_MODEL), row),
        out_shape=jax.ShapeDtypeStruct((tokens, D_MODEL), f32),
        compiler_params=pltpu.CompilerParams(
            dimension_semantics=("arbitrary",), vmem_limit_bytes=VMEM_LIMIT),
        name="out_mlp",
    )(x2, attn, ssd, z, xatt, w_out, ssd_norm_w, norm_w, w_up, w_down)


def _rope_tables(seq):
    half = ROPE_DIM // 2
    inv = ROPE_THETA ** (-jnp.arange(0, ROPE_DIM, 2, dtype=f32) / ROPE_DIM)
    ang = jnp.arange(seq, dtype=f32)[:, None] * inv[None, :]
    cos = jnp.cos(ang)
    sin = jnp.sin(ang)
    ones = jnp.ones((seq, ATT_HEAD_DIM - ROPE_DIM), f32)
    zeros_h = jnp.zeros((seq, half), f32)
    zeros_p = jnp.zeros((seq, ATT_HEAD_DIM - ROPE_DIM), f32)
    cos_head = jnp.concatenate([cos, cos, ones], axis=1)
    lo_head = jnp.concatenate([-sin, zeros_h, zeros_p], axis=1)
    hi_head = jnp.concatenate([zeros_h, sin, zeros_p], axis=1)
    rep = LANES // ATT_HEAD_DIM
    return (jnp.tile(cos_head, (1, rep)), jnp.tile(lo_head, (1, rep)),
            jnp.tile(hi_head, (1, rep)))


def _layer(x, mem, norm_mix_w, w_in, q_norm_w, k_norm_w, attn_sink, conv_w, conv_b,
           dt_bias_f, dt_bias_b, a_log_f, a_log_b, ssd_d, ssd_norm_w, mem_norm_w,
           w_mem_kv, xq_norm_w, xk_norm_w, w_out, norm_mlp_w, w_mlp_up, w_mlp_down, tables):
    batch, seq, _ = x.shape
    mem_len = mem.shape[1]
    tokens = batch * seq
    tm = 512

    s = np.cumsum([ATT_Q_W, ATT_KV_W, ATT_KV_W, SSD_WIDTH, SSD_XBC_W, SSD_DT_W, XATT_W])
    piece = lambda lo, hi: w_in[:, lo:hi].astype(bf16)
    w_q, w_k, w_v = piece(0, s[0]), piece(s[0], s[1]), piece(s[1], s[2])
    w_z, w_xbc, w_dt, w_qx = (piece(s[2], s[3]), piece(s[3], s[4]),
                              piece(s[4], s[5]), piece(s[5], s[6]))
    hd = ATT_HEAD_DIM
    dup = lambda w: jnp.concatenate(
        [w[:, g * hd:(g + 1) * hd] for g in range(ATT_KV_HEADS) for _ in range(2)], axis=1)
    w_dt_p = jnp.concatenate([w_dt, jnp.zeros((D_MODEL, DT_PAD_W - SSD_DT_W), bf16)], axis=1)
    weights = (w_q, dup(w_k), dup(w_v), w_z, w_xbc, w_qx, w_dt_p)

    pad = jnp.zeros((DT_PAD_W - SSD_DT_W,), f32)
    dt_bias_row = jnp.concatenate([dt_bias_f, dt_bias_b, pad]).reshape(1, DT_PAD_W)
    a_row = jnp.concatenate([-jnp.exp(a_log_f), -jnp.exp(a_log_b), pad]).reshape(1, DT_PAD_W)
    conv_w8 = jnp.concatenate([conv_w, jnp.zeros((8 - SSD_CONV, SSD_XBC_W), f32)], axis=0)
    dmat = (ssd_d.astype(f32)[:, None, None] * jnp.eye(SSD_CHUNK, dtype=f32)).astype(bf16)

    x2 = x.reshape(tokens, D_MODEL)
    cos_t, sinlo_t, sinhi_t = tables
    q, kdup, vt, z, xc, qx, rows, rowsl, cols, dtT = _in_proj(
        x2, norm_mix_w.reshape(1, D_MODEL), weights,
        jnp.tile(q_norm_w, ATT_HEADS).reshape(1, ATT_Q_W),
        jnp.tile(k_norm_w, KDUP_W // hd).reshape(1, KDUP_W),
        xq_norm_w.reshape(1, XATT_HEAD_DIM), cos_t, sinlo_t, sinhi_t,
        conv_w8, conv_b.reshape(1, SSD_XBC_W), dt_bias_row, a_row, seq, tm)

    attn = _attention(attn_sink.astype(f32), q, kdup, vt, batch, seq)
    ssd = _ssd(dmat, xc, rows, rowsl, cols, dtT, batch, seq)

    km, vm = _mem_kv(mem.reshape(batch * mem_len, D_MODEL), mem_norm_w.reshape(1, D_MODEL),
                     w_mem_kv.astype(bf16), xk_norm_w.reshape(1, XATT_HEAD_DIM), batch, mem_len)
    xatt = _xattn(qx, km, vm, batch, seq, mem_len, min(XATT_TQ, seq))

    out = _out_mlp(x2, attn, ssd, z, xatt, w_out.astype(bf16),
                   ssd_norm_w.reshape(1, SSD_WIDTH), norm_mlp_w.reshape(1, D_MODEL),
                   w_mlp_up.astype(bf16), w_mlp_down.astype(bf16), tm)
    return out.reshape(batch, seq, D_MODEL)


def kernel(x, mem, norm_mix_w, w_in, q_norm_w, k_norm_w, attn_sink, conv_w, conv_b,
           dt_bias_f, dt_bias_b, a_log_f, a_log_b, ssd_d, ssd_norm_w, mem_norm_w,
           w_mem_kv, xq_norm_w, xk_norm_w, w_out, norm_mlp_w, w_mlp_up, w_mlp_down):
    depth = w_in.shape[0]
    tables = _rope_tables(x.shape[1])
    for i in range(depth):
        x = _layer(x, mem, norm_mix_w[i], w_in[i], q_norm_w[i], k_norm_w[i], attn_sink[i],
                   conv_w[i], conv_b[i], dt_bias_f[i], dt_bias_b[i], a_log_f[i], a_log_b[i],
                   ssd_d[i], ssd_norm_w[i], mem_norm_w[i], w_mem_kv[i], xq_norm_w[i],
                   xk_norm_w[i], w_out[i], norm_mlp_w[i], w_mlp_up[i], w_mlp_down[i], tables)
    return x
```

```python
import functools

import numpy as np
import jax
import jax.numpy as jnp
from jax import lax
from jax.experimental import pallas as pl
from jax.experimental.pallas import tpu as pltpu

D_MODEL = 1024
EPS = 1e-6

ATT_HEADS = 8
ATT_KV_HEADS = 2
ATT_HEAD_DIM = 64
ATT_Q_W = ATT_HEADS * ATT_HEAD_DIM
ATT_KV_W = ATT_KV_HEADS * ATT_HEAD_DIM
WINDOW = 128
ATT_BLOCK = 128
ATT_BLOCKS_PER_STEP = 4
ATT_HEADS_PER_CHAIN = 4
ROPE_THETA = 500000.0
ROPE_DIM = ATT_HEAD_DIM // 4

SSD_HEADS = 16
SSD_HEAD_DIM = 64
SSD_WIDTH = SSD_HEADS * SSD_HEAD_DIM
SSD_GROUPS = 2
SSD_STATE = 128
SSD_CONV = 5
SSD_CHUNK = 128
SSD_CHUNKS_PER_STEP = 4
SSD_XBC_W = SSD_WIDTH + 2 * SSD_GROUPS * SSD_STATE
SSD_DT_W = 2 * SSD_HEADS

XATT_HEADS = 4
XATT_HEAD_DIM = 128
XATT_W = XATT_HEADS * XATT_HEAD_DIM
XATT_TQ = 1024

D_FF = 4 * D_MODEL
D_MIX = 2 * D_MODEL

LANES = 128
BF16_SUBLANES = 16
F32_SUBLANES = 8
VMEM_LIMIT = 56 * 1024 * 1024

KDUP_W = 2 * ATT_KV_W
DT_PAD_W = LANES
C_Q = 0
C_K = C_Q + ATT_Q_W
C_V = C_K + KDUP_W
C_Z = C_V + KDUP_W
C_XBC = C_Z + SSD_WIDTH
C_QX = C_XBC + SSD_XBC_W
C_DT = C_QX + XATT_W
D_IN_P = C_DT + DT_PAD_W

CONV_HALO = BF16_SUBLANES
C_B = SSD_WIDTH
C_C = SSD_WIDTH + SSD_GROUPS * SSD_STATE

NEG_BIG = -1e30
LOG2E = 1.4426950408889634

bf16 = jnp.bfloat16
f32 = jnp.float32


def _dot(a, b):
    return jnp.dot(a, b, preferred_element_type=f32)


def _dot_nt(a, b):
    return lax.dot_general(a, b, (((1,), (1,)), ((), ())), preferred_element_type=f32)


def _split3(x):
    x1 = x.astype(bf16)
    r1 = x - x1.astype(f32)
    x2 = r1.astype(bf16)
    r2 = r1 - x2.astype(f32)
    return x1, x2, r2.astype(bf16)


def _dot3(x, m01):
    x1, x2, x3 = _split3(x)
    return _dot(x1, m01) + _dot(x2, m01) + _dot(x3, m01)


def _iota(shape, dim):
    return lax.broadcasted_iota(jnp.int32, shape, dim)


def _segment_ones(width, seg):
    r = _iota((width, width), 0) // seg
    c = _iota((width, width), 1) // seg
    return jnp.where(r == c, 1.0, 0.0).astype(bf16)


def _head_mean_sq(t, seg):
    ones = _segment_ones(t.shape[1], seg)
    return _dot((t * t).astype(bf16), ones) * (1.0 / seg)


def _rope(t, cos, sin_lo, sin_hi):
    half = ROPE_DIM // 2
    return (t * cos + pltpu.roll(t, LANES - half, 1) * sin_lo
            + pltpu.roll(t, half, 1) * sin_hi)


def _scheduled_after(x, anchor):
    folded = anchor[:, 0:LANES]
    for c0 in range(LANES, anchor.shape[1], LANES):
        folded = folded + anchor[:, c0:c0 + LANES]
    bits = pltpu.bitcast(jnp.broadcast_to(folded, (F32_SUBLANES, LANES)), jnp.uint32)
    sixteen = jnp.full(bits.shape, 16, jnp.uint32)
    zero = lax.shift_right_logical(lax.shift_right_logical(bits, sixteen), sixteen)
    zero = jnp.concatenate([zero.astype(f32)] * 2, axis=0).astype(bf16)
    head = x[0:BF16_SUBLANES, :] + jnp.concatenate([zero] * (x.shape[1] // LANES), axis=1)
    return jnp.concatenate([head, x[BF16_SUBLANES:, :]], axis=0)


def _rms_rows(x, w):
    ms = jnp.mean(x * x, axis=-1, keepdims=True)
    return (x * lax.rsqrt(ms + EPS)) * w


def _in_proj_body(pos_blocks, x_ref, xp_ref, xn_ref, nw_ref,
                  wq_ref, wk_ref, wvdt_ref, wz_ref, wxbc_ref, wqx_ref, qnw_ref, knw_ref,
                  xqnw_ref, cos_ref, sinlo_ref, sinhi_ref, cw_ref, cb_ref, dtb_ref, arow_ref,
                  q_ref, k_ref, v_ref, z_ref, xc_ref, qx_ref,
                  rows_ref, rowsl_ref, cols_ref, dtT_ref, conv_s):
    i = pl.program_id(0)
    tm = x_ref.shape[0]
    si = i % pos_blocks
    nw = nw_ref[...]
    h = _rms_rows(x_ref[...], nw).astype(bf16)

    cos = cos_ref[...]
    sin_lo = sinlo_ref[...]
    sin_hi = sinhi_ref[...]

    pq = _dot(h, wq_ref[...])
    qn = pq * lax.rsqrt(_head_mean_sq(pq, ATT_HEAD_DIM) + EPS) * qnw_ref[...]
    for c in range(ATT_Q_W // LANES):
        sl = slice(c * LANES, (c + 1) * LANES)
        q_ref[:, sl] = (_rope(qn[:, sl], cos, sin_lo, sin_hi)
                        * (ATT_HEAD_DIM ** -0.5 * LOG2E)).astype(bf16)

    pk = _dot(h, wk_ref[...])
    kn = pk * lax.rsqrt(_head_mean_sq(pk, ATT_HEAD_DIM) + EPS) * knw_ref[...]
    for c in range(KDUP_W // LANES):
        sl = slice(c * LANES, (c + 1) * LANES)
        k_ref[:, sl] = _rope(kn[:, sl], cos, sin_lo, sin_hi).astype(bf16)

    pvdt = _dot(h, wvdt_ref[...])
    v_ref[...] = pvdt[:, 0:ATT_KV_W].T.astype(bf16)
    z_ref[...] = _dot(h, wz_ref[...]).astype(bf16)

    pqx = _dot(h, wqx_ref[...])
    for c in range(XATT_HEADS):
        sl = slice(c * XATT_HEAD_DIM, (c + 1) * XATT_HEAD_DIM)
        t = pqx[:, sl]
        tn = t * lax.rsqrt(jnp.mean(t * t, axis=-1, keepdims=True) + EPS) * xqnw_ref[...]
        qx_ref[:, sl] = (tn * (XATT_HEAD_DIM ** -0.5 * LOG2E)).astype(bf16)

    hp = _rms_rows(xp_ref[...], nw)
    hn = _rms_rows(xn_ref[...], nw)
    hp = jnp.where(si > 0, hp, jnp.zeros_like(hp)).astype(bf16)
    hn = jnp.where(si < pos_blocks - 1, hn, jnp.zeros_like(hn)).astype(bf16)
    h_ext = jnp.concatenate([hp, h, hn], axis=0)
    conv_s[...] = _dot(h_ext, wxbc_ref[...])
    acc = jnp.broadcast_to(cb_ref[...], (tm, SSD_XBC_W))
    for k in range(SSD_CONV):
        off = CONV_HALO - SSD_CONV // 2 + k
        acc = acc + conv_s[off:off + tm, :] * cw_ref[k:k + 1, :]
    xc_ref[...] = (acc / (1.0 + jnp.exp2(acc * (-LOG2E)))).astype(bf16)

    Q = SSD_CHUNK
    dt = pvdt[:, ATT_KV_W:ATT_KV_W + DT_PAD_W] + dtb_ref[...]
    dt = jnp.maximum(dt, 0.0) + jnp.log1p(jnp.exp(-jnp.abs(dt)))
    a = dt * arow_ref[...]
    li = _iota((Q, Q), 0)
    lj = _iota((Q, Q), 1)
    incl_le = jnp.where(li <= lj, 1.0, 0.0).astype(bf16)
    incl_ge = jnp.where(li >= lj, 1.0, 0.0).astype(bf16)
    pad_rows = jnp.zeros((Q - 2 * SSD_HEADS, Q), f32)
    for ch in range(tm // Q):
        r = slice(ch * Q, (ch + 1) * Q)
        aT = a[r].T
        dtT = dt[r].T
        pre = _dot3(aT[0:SSD_HEADS], incl_le)
        suf = _dot3(aT[SSD_HEADS:2 * SSD_HEADS], incl_ge)
        rows = jnp.concatenate([pre, suf, pad_rows], axis=0) * LOG2E
        rows_ref[r, :] = rows
        rowsl_ref[r, :] = rows - jnp.log(dtT) * LOG2E
        cols_ref[r, :] = rows.T
        dtT_ref[r, :] = dtT


def _in_proj(x2, norm_w, weights, qnw, knw, xqnw, cos_t, sinlo_t, sinhi_t,
             conv_w8, conv_b, dt_bias_row, a_row, seq, tm):
    tokens = x2.shape[0]
    pos_blocks = seq // tm
    hpt = tm // CONV_HALO
    n_halo = tokens // CONV_HALO
    row = lambda i: (i, 0)
    const = lambda i: (0, 0)
    pos = lambda i: (i % pos_blocks, 0)
    prev = lambda i: (jnp.maximum(i * hpt - 1, 0), 0)
    nxt = lambda i: (jnp.minimum((i + 1) * hpt, n_halo - 1), 0)
    out_widths = [ATT_Q_W, KDUP_W, KDUP_W, SSD_WIDTH, SSD_XBC_W, XATT_W,
                  LANES, LANES, LANES, LANES]
    out_dtypes = [bf16, bf16, bf16, bf16, bf16, bf16, f32, f32, f32, f32]
    out_specs = [pl.BlockSpec((tm, w), row) for w in out_widths]
    out_shape = [jax.ShapeDtypeStruct((tokens, w), d) for w, d in zip(out_widths, out_dtypes)]
    out_specs[2] = pl.BlockSpec((ATT_KV_W, tm), lambda i: (i // pos_blocks, i % pos_blocks))
    out_shape[2] = jax.ShapeDtypeStruct((tokens // seq * ATT_KV_W, seq), bf16)
    return pl.pallas_call(
        functools.partial(_in_proj_body, pos_blocks),
        grid=(tokens // tm,),
        in_specs=[
            pl.BlockSpec((tm, D_MODEL), row),
            pl.BlockSpec((CONV_HALO, D_MODEL), prev),
            pl.BlockSpec((CONV_HALO, D_MODEL), nxt),
            pl.BlockSpec((1, D_MODEL), const),
            *[pl.BlockSpec(w.shape, const, pipeline_mode=pl.Buffered(1)) for w in weights],
            pl.BlockSpec((1, ATT_Q_W), const),
            pl.BlockSpec((1, KDUP_W), const),
            pl.BlockSpec((1, XATT_HEAD_DIM), const),
            pl.BlockSpec((tm, LANES), pos),
            pl.BlockSpec((tm, LANES), pos),
            pl.BlockSpec((tm, LANES), pos),
            pl.BlockSpec((8, SSD_XBC_W), const),
            pl.BlockSpec((1, SSD_XBC_W), const),
            pl.BlockSpec((1, DT_PAD_W), const),
            pl.BlockSpec((1, DT_PAD_W), const),
        ],
        out_specs=out_specs,
        out_shape=out_shape,
        scratch_shapes=[pltpu.VMEM((tm + 2 * CONV_HALO, SSD_XBC_W), f32)],
        compiler_params=pltpu.CompilerParams(
            dimension_semantics=("arbitrary",), vmem_limit_bytes=VMEM_LIMIT),
        name="in_proj",
    )(x2, x2, x2, norm_w, *weights, qnw, knw, xqnw, cos_t, sinlo_t, sinhi_t,
      conv_w8, conv_b, dt_bias_row, a_row)


def _attn_kernel(sink_ref, mfirst_ref, mmid_ref, mlast_ref, q_ref, kp_ref, kc_ref, kn_ref,
                 vp_ref, vc_ref, vn_ref, o_ref):
    blk = ATT_BLOCK
    per = ATT_BLOCKS_PER_STEP
    heads_per_kv = ATT_HEADS // ATT_KV_HEADS
    hpc = ATT_HEADS_PER_CHAIN
    cols = hpc * blk
    low_lanes = _iota((blk, LANES), 1) < ATT_HEAD_DIM
    head_of_col = _iota((1, cols), 1) // blk
    masks = [mfirst_ref] + [mmid_ref] * (per - 2) + [mlast_ref]
    half = LANES // 2

    kblocks, vblocks = [], []
    for g in range(ATT_KV_HEADS):
        gs = slice(g * LANES, (g + 1) * LANES)
        vs = slice(g * ATT_HEAD_DIM, (g + 1) * ATT_HEAD_DIM)
        kblocks.append([kp_ref[:, gs]] + [kc_ref[i * blk:(i + 1) * blk, gs] for i in range(per)]
                       + [kn_ref[:, gs]])
        vblocks.append([vp_ref[vs, :]] + [vc_ref[vs, i * blk:(i + 1) * blk] for i in range(per)]
                       + [vn_ref[vs, :]])
    ones_rows = jnp.ones((LANES - ATT_HEAD_DIM, 3 * blk), bf16)

    def scores(g, t, h0):
        qrows = slice(t * blk, (t + 1) * blk)
        kcat = jnp.concatenate(kblocks[g][t:t + 3], axis=0)
        qs = []
        sink = jnp.zeros((1, cols), f32)
        for j in range(hpc):
            hd = h0 + j
            pair = q_ref[qrows, (hd // 2) * LANES:(hd // 2 + 1) * LANES]
            keep = low_lanes if hd % 2 == 0 else jnp.logical_not(low_lanes)
            qs.append(jnp.where(keep, pair, jnp.zeros_like(pair)))
            sink = jnp.where(head_of_col == j, sink_ref[hd] * LOG2E, sink)
        qg = jnp.concatenate(qs, axis=0)
        st = _dot_nt(kcat, qg) + masks[t][:, 0:cols]
        m = jnp.maximum(jnp.max(st, axis=0, keepdims=True), sink)
        return st, m, sink

    def finish(g, t, h0, st, m, sink, m_next):
        qrows = slice(t * blk, (t + 1) * blk)
        vt = jnp.concatenate(
            [jnp.concatenate(vblocks[g][t:t + 3], axis=1), ones_rows], axis=0)
        if m_next is not None:
            vt = _scheduled_after(vt, m_next)
        p = jnp.exp2(st - m).astype(bf16)
        ov = _dot(vt, p)
        den = ov[ATT_HEAD_DIM:ATT_HEAD_DIM + 1] + jnp.exp2(sink - m)
        ot = ov[0:ATT_HEAD_DIM] / den
        for jp in range(hpc // 2):
            even = ot[:, (2 * jp) * blk:(2 * jp + 1) * blk]
            odd = ot[:, (2 * jp + 1) * blk:(2 * jp + 2) * blk]
            pr = h0 // 2 + jp
            o_ref[qrows, pr * LANES:(pr + 1) * LANES] = (
                jnp.concatenate([even, odd], axis=0).T.astype(bf16))

    chains = [(g, t, g * heads_per_kv + c * hpc) for g in range(ATT_KV_HEADS)
              for t in range(per) for c in range(heads_per_kv // hpc)]
    nxt = scores(*chains[0])
    for idx, ch in enumerate(chains):
        st, m, sink = nxt
        nxt = scores(*chains[idx + 1]) if idx + 1 < len(chains) else None
        finish(*ch, st, m, sink, None if nxt is None else nxt[1])


def _attn_masks():
    blk = ATT_BLOCK
    cols = (ATT_HEADS // ATT_KV_HEADS) * blk
    key = np.arange(3 * blk)[:, None]
    qi = np.arange(cols)[None, :] % blk
    band = (key >= qi) & (key <= qi + 2 * WINDOW)
    variants = [band & (key >= blk), band, band & (key < 2 * blk)]
    return jnp.asarray(np.where(np.stack(variants), 0.0, NEG_BIG), dtype=f32)


def _attention(sink, q, kdup, vt, batch, seq):
    blk = ATT_BLOCK
    nb = seq // blk
    per = ATT_BLOCKS_PER_STEP
    ns = nb // per
    cols = (ATT_HEADS // ATT_KV_HEADS) * blk
    masks = _attn_masks()
    cur = lambda b, s: (b * ns + s, 0)
    prev = lambda b, s: (b * nb + jnp.maximum(per * s - 1, 0), 0)
    nxt = lambda b, s: (b * nb + jnp.minimum(per * s + per, nb - 1), 0)
    vcur = lambda b, s: (b, s)
    vprev = lambda b, s: (b, jnp.maximum(per * s - 1, 0))
    vnxt = lambda b, s: (b, jnp.minimum(per * s + per, nb - 1))
    mfirst = lambda b, s: (jnp.where(s == 0, 0, 1), 0, 0)
    mmid = lambda b, s: (1, 0, 0)
    mlast = lambda b, s: (jnp.where(s == ns - 1, 2, 1), 0, 0)
    return pl.pallas_call(
        _attn_kernel,
        grid=(batch, ns),
        in_specs=[
            pl.BlockSpec(memory_space=pltpu.SMEM),
            pl.BlockSpec((None, 3 * blk, cols), mfirst),
            pl.BlockSpec((None, 3 * blk, cols), mmid),
            pl.BlockSpec((None, 3 * blk, cols), mlast),
            pl.BlockSpec((per * blk, ATT_Q_W), cur),
            pl.BlockSpec((blk, KDUP_W), prev),
            pl.BlockSpec((per * blk, KDUP_W), cur),
            pl.BlockSpec((blk, KDUP_W), nxt),
            pl.BlockSpec((ATT_KV_W, blk), vprev),
            pl.BlockSpec((ATT_KV_W, per * blk), vcur),
            pl.BlockSpec((ATT_KV_W, blk), vnxt),
        ],
        out_specs=pl.BlockSpec((per * blk, ATT_Q_W), cur),
        out_shape=jax.ShapeDtypeStruct((batch * seq, ATT_Q_W), bf16),
        compiler_params=pltpu.CompilerParams(
            dimension_semantics=("arbitrary", "arbitrary"), vmem_limit_bytes=VMEM_LIMIT),
        name="attention",
    )(sink, masks, masks, masks, q, kdup, kdup, kdup, vt, vt, vt)


def _ssd_kernel(xc_ref, rows_ref, rowsl_ref, cols_ref, dtT_ref, dmat_ref, y_ref,
                hf_s, sb_s, hf_run, hb_run):
    ph = pl.program_id(1)
    step = pl.program_id(2)
    nsteps = pl.num_programs(2)
    Q = SSD_CHUNK
    per = SSD_CHUNKS_PER_STEP
    heads_per_group = SSD_HEADS // SSD_GROUPS
    low_lanes = _iota((Q, LANES), 1) < SSD_HEAD_DIM
    low_row = _iota((1, LANES), 1) < SSD_HEAD_DIM
    li = _iota((Q, Q), 0)
    si = _iota((Q, Q), 1)

    def chunk_states(c, rs):
        rows = rows_ref[rs, :]
        dtT = dtT_ref[rs, :]
        pre = rows[0:SSD_HEADS]
        suf = rows[SSD_HEADS:2 * SSD_HEADS]
        pre_end = jnp.broadcast_to(pre[:, Q - 1:Q], (SSD_HEADS, Q))
        suf_end = jnp.broadcast_to(suf[:, 0:1], (SSD_HEADS, Q))
        wf = jnp.exp2(pre_end - pre) * dtT[0:SSD_HEADS]
        wb = jnp.exp2(suf_end - suf) * dtT[SSD_HEADS:2 * SSD_HEADS]
        decf = jnp.exp2(pre_end)

        for g in range(SSD_GROUPS):
            bT = xc_ref[rs, C_B + g * SSD_STATE:C_B + (g + 1) * SSD_STATE].astype(f32).T
            for m in range(heads_per_group // 2):
                h0 = g * heads_per_group + 2 * m
                h1 = h0 + 1
                lhs = jnp.concatenate(
                    [bT * wf[h0:h0 + 1], bT * wb[h0:h0 + 1],
                     bT * wf[h1:h1 + 1], bT * wb[h1:h1 + 1]], axis=0).astype(bf16)
                cs = slice(h0 * SSD_HEAD_DIM, h0 * SSD_HEAD_DIM + LANES)
                res = _dot(lhs, xc_ref[rs, cs])
                s_f = jnp.where(low_lanes, res[0:Q], res[2 * Q:3 * Q])
                s_b = jnp.where(low_lanes, res[Q:2 * Q], res[3 * Q:4 * Q])
                dec = jnp.where(low_row, decf[h0:h0 + 1], decf[h1:h1 + 1])
                prev = hf_run[:, cs]
                hf_s[c, :, cs] = prev.astype(bf16)
                hf_run[:, cs] = dec * prev + s_f
                sb_s[c, :, cs] = s_b.astype(bf16)

    def chunk_outputs(c, rs):
        rows = rows_ref[rs, :]
        suf = rows[SSD_HEADS:2 * SSD_HEADS]
        cols = cols_ref[rs, :]
        rowsl = rowsl_ref[rs, :]
        decb = jnp.exp2(jnp.broadcast_to(suf[:, 0:1], (SSD_HEADS, Q)))
        lower = li >= si
        upper = si >= li

        for g in range(SSD_GROUPS):
            bg = xc_ref[rs, C_B + g * SSD_STATE:C_B + (g + 1) * SSD_STATE]
            cg = xc_ref[rs, C_C + g * SSD_STATE:C_C + (g + 1) * SSD_STATE]
            cbm = _dot_nt(cg, bg).astype(bf16)
            for m in range(heads_per_group // 2):
                h0 = g * heads_per_group + 2 * m
                parts = []
                for hd in (h0, h0 + 1):
                    hb = SSD_HEADS + hd
                    colf = jnp.broadcast_to(cols[:, hd:hd + 1], (Q, Q))
                    colb = jnp.broadcast_to(cols[:, hb:hb + 1], (Q, Q))
                    ef = jnp.exp2(jnp.where(lower, colf - rowsl[hd:hd + 1], NEG_BIG)).astype(bf16)
                    eb = jnp.exp2(jnp.where(upper, colb - rowsl[hb:hb + 1], NEG_BIG)).astype(bf16)
                    gm = cbm * (ef + eb) + dmat_ref[hd]
                    parts.append(jnp.concatenate(
                        [gm, cg * jnp.exp2(colf).astype(bf16), cg * jnp.exp2(colb).astype(bf16)],
                        axis=1))
                lhs = jnp.concatenate(parts, axis=0)
                cs = slice(h0 * SSD_HEAD_DIM, h0 * SSD_HEAD_DIM + LANES)
                hb_prev = hb_run[:, cs]
                rhs = jnp.concatenate(
                    [xc_ref[rs, cs], hf_s[c, :, cs], hb_prev.astype(bf16)], axis=0)
                res = _dot(lhs, rhs)
                y_ref[rs, cs] = jnp.where(low_lanes, res[0:Q], res[Q:2 * Q]).astype(bf16)
                dec = jnp.where(low_row, decb[h0:h0 + 1], decb[h0 + 1:h0 + 2])
                hb_run[:, cs] = dec * hb_prev + sb_s[c, :, cs].astype(f32)

    @pl.when(ph == 0)
    def _phase0():
        @pl.when(step == 0)
        def _():
            hf_run[...] = jnp.zeros_like(hf_run)

        for sub in range(per):
            chunk_states(step * per + sub, slice(sub * Q, (sub + 1) * Q))

    @pl.when(ph == 1)
    def _phase1():
        @pl.when(step == 0)
        def _():
            hb_run[...] = jnp.zeros_like(hb_run)

        back_step = nsteps - 1 - step
        for sub in reversed(range(per)):
            chunk_outputs(back_step * per + sub, slice(sub * Q, (sub + 1) * Q))


def _ssd(dmat, xc, rows, rowsl, cols, dtT, batch, seq):
    Q = SSD_CHUNK
    nc = seq // Q
    rb = SSD_CHUNKS_PER_STEP * Q
    ns = seq // rb
    last = ns - 1

    def both_map(b, ph, s):
        return (b * ns + s * (1 - ph) + (last - s) * ph, 0)

    def fwd_map(b, ph, s):
        return (b * ns + s * (1 - ph) + last * ph, 0)

    def back_map(b, ph, s):
        return (b * ns + last - s * ph, 0)

    return pl.pallas_call(
        _ssd_kernel,
        grid=(batch, 2, ns),
        in_specs=[
            pl.BlockSpec((rb, SSD_XBC_W), both_map),
            pl.BlockSpec((rb, LANES), both_map),
            pl.BlockSpec((rb, LANES), back_map),
            pl.BlockSpec((rb, LANES), back_map),
            pl.BlockSpec((rb, LANES), fwd_map),
            pl.BlockSpec((SSD_HEADS, Q, Q), lambda b, ph, s: (0, 0, 0)),
        ],
        out_specs=pl.BlockSpec((rb, SSD_WIDTH), back_map),
        out_shape=jax.ShapeDtypeStruct((batch * seq, SSD_WIDTH), bf16),
        scratch_shapes=[
            pltpu.VMEM((nc, SSD_STATE, SSD_WIDTH), bf16),
            pltpu.VMEM((nc, SSD_STATE, SSD_WIDTH), bf16),
            pltpu.VMEM((SSD_STATE, SSD_WIDTH), f32),
            pltpu.VMEM((SSD_STATE, SSD_WIDTH), f32),
        ],
        compiler_params=pltpu.CompilerParams(
            dimension_semantics=("arbitrary", "arbitrary", "arbitrary"),
            vmem_limit_bytes=VMEM_LIMIT),
        name="ssd",
    )(xc, rows, rowsl, cols, dtT, dmat)


def _mem_kv_kernel(mem_ref, nw_ref, w_ref, knw_ref, k_ref, v_ref):
    h = _rms_rows(mem_ref[...], nw_ref[...]).astype(bf16)
    kv = _dot(h, w_ref[...])
    for c in range(XATT_HEADS):
        sl = slice(c * XATT_HEAD_DIM, (c + 1) * XATT_HEAD_DIM)
        t = kv[:, sl]
        k_ref[:, sl] = (t * lax.rsqrt(jnp.mean(t * t, axis=-1, keepdims=True) + EPS)
                        * knw_ref[...]).astype(bf16)
    v_ref[...] = kv[:, XATT_W:].astype(bf16)


def _mem_kv(mem2, mem_norm_w, w_kv, xk_norm_w, batch, mem_len):
    row = lambda b: (b, 0)
    const = lambda b: (0, 0)
    return pl.pallas_call(
        _mem_kv_kernel,
        grid=(batch,),
        in_specs=[
            pl.BlockSpec((mem_len, D_MODEL), row),
            pl.BlockSpec((1, D_MODEL), const),
            pl.BlockSpec((D_MODEL, 2 * XATT_W), const),
            pl.BlockSpec((1, XATT_HEAD_DIM), const),
        ],
        out_specs=[pl.BlockSpec((mem_len, XATT_W), row)] * 2,
        out_shape=[jax.ShapeDtypeStruct((batch * mem_len, XATT_W), bf16)] * 2,
        compiler_params=pltpu.CompilerParams(
            dimension_semantics=("arbitrary",), vmem_limit_bytes=VMEM_LIMIT),
        name="mem_kv",
    )(mem2, mem_norm_w, w_kv, xk_norm_w)


def _xattn_kernel(q_ref, k_ref, v_ref, o_ref):
    for hd in range(XATT_HEADS):
        sl = slice(hd * XATT_HEAD_DIM, (hd + 1) * XATT_HEAD_DIM)
        s = _dot_nt(q_ref[:, sl], k_ref[:, sl])
        m = jnp.max(s, axis=-1, keepdims=True)
        p = jnp.exp2(s - m)
        den = jnp.sum(p, axis=-1, keepdims=True)
        o_ref[:, sl] = (_dot(p.astype(bf16), v_ref[:, sl]) / den).astype(bf16)


def _xattn(qx, km, vm, batch, seq, mem_len, tq):
    nq = seq // tq
    qmap = lambda b, i: (b * nq + i, 0)
    mmap = lambda b, i: (b, 0)
    return pl.pallas_call(
        _xattn_kernel,
        grid=(batch, nq),
        in_specs=[
            pl.BlockSpec((tq, XATT_W), qmap),
            pl.BlockSpec((mem_len, XATT_W), mmap),
            pl.BlockSpec((mem_len, XATT_W), mmap),
        ],
        out_specs=pl.BlockSpec((tq, XATT_W), qmap),
        out_shape=jax.ShapeDtypeStruct((batch * seq, XATT_W), bf16),
        compiler_params=pltpu.CompilerParams(
            dimension_semantics=("arbitrary", "arbitrary"), vmem_limit_bytes=VMEM_LIMIT),
        name="xattn",
    )(qx, km, vm)


FF_CHUNK = 1024


def _out_mlp_kernel(x_ref, a_ref, s_ref, z_ref, c_ref, wo_ref, snw_ref, nw_ref, wu_ref, wd_ref,
                    o_ref):
    zf = z_ref[...].astype(f32)
    y = s_ref[...].astype(f32) * (zf / (1.0 + jnp.exp2(zf * (-LOG2E))))
    gw = SSD_WIDTH // SSD_GROUPS
    mix = (_dot(a_ref[...], wo_ref[0:ATT_Q_W, :])
           + _dot(c_ref[...], wo_ref[ATT_Q_W + SSD_WIDTH:, :]))
    for g in range(SSD_GROUPS):
        yg = y[:, g * gw:(g + 1) * gw]
        ms = jnp.mean(yg * yg, axis=-1, keepdims=True)
        yn = (yg * lax.rsqrt(ms + EPS) * snw_ref[:, g * gw:(g + 1) * gw]).astype(bf16)
        mix = mix + _dot(yn, wo_ref[ATT_Q_W + g * gw:ATT_Q_W + (g + 1) * gw, :])
    x1 = x_ref[...] + mix
    h = _rms_rows(x1, nw_ref[...]).astype(bf16)
    acc = x1
    for j in range(D_FF // FF_CHUNK):
        u = _dot(h, wu_ref[:, j * FF_CHUNK:(j + 1) * FF_CHUNK])
        r = jnp.maximum(u, 0.0)
        acc = acc + _dot((r * r).astype(bf16), wd_ref[j * FF_CHUNK:(j + 1) * FF_CHUNK, :])
    o_ref[...] = acc


def _out_mlp(x2, attn, ssd, z, xatt, w_out, ssd_norm_w, norm_w, w_up, w_down, tm):
    tokens = x2.shape[0]
    n_tiles = tokens // tm
    row = lambda i: (i, 0)
    const = lambda i: (0, 0)
    resident = functools.partial(pl.BlockSpec, index_map=const, pipeline_mode=pl.Buffered(1))
    return pl.pallas_call(
        _out_mlp_kernel,
        grid=(n_tiles,),
        in_specs=[
            pl.BlockSpec((tm, D_MODEL), row),
            pl.BlockSpec((tm, ATT_Q_W), row),
            pl.BlockSpec((tm, SSD_WIDTH), row),
            pl.BlockSpec((tm, SSD_WIDTH), row),
            pl.BlockSpec((tm, XATT_W), row),
            resident((D_MIX, D_MODEL)),
            pl.BlockSpec((1, SSD_WIDTH), const),
            pl.BlockSpec((1, D_MODEL), const),
            resident((D_MODEL, D_FF)),
            resident((D_FF, D_MODEL)),
        ],
        out_specs=pl.BlockSpec((tm, D_MODEL), row),
        out_shape=jax.ShapeDtypeStruct((tokens, D_MODEL), f32),
        compiler_params=pltpu.CompilerParams(
            dimension_semantics=("arbitrary",), vmem_limit_bytes=VMEM_LIMIT),
        name="out_mlp",
    )(x2, attn, ssd, z, xatt, w_out, ssd_norm_w, norm_w, w_up, w_down)


def _rope_tables(seq):
    half = ROPE_DIM // 2
    inv = ROPE_THETA ** (-jnp.arange(0, ROPE_DIM, 2, dtype=f32) / ROPE_DIM)
    ang = jnp.arange(seq, dtype=f32)[:, None] * inv[None, :]
    cos = jnp.cos(ang)
    sin = jnp.sin(ang)
    ones = jnp.ones((seq, ATT_HEAD_DIM - ROPE_DIM), f32)
    zeros_h = jnp.zeros((seq, half), f32)
    zeros_p = jnp.zeros((seq, ATT_HEAD_DIM - ROPE_DIM), f32)
    cos_head = jnp.concatenate([cos, cos, ones], axis=1)
    lo_head = jnp.concatenate([-sin, zeros_h, zeros_p], axis=1)
    hi_head = jnp.concatenate([zeros_h, sin, zeros_p], axis=1)
    rep = LANES // ATT_HEAD_DIM
    return (jnp.tile(cos_head, (1, rep)), jnp.tile(lo_head, (1, rep)),
            jnp.tile(hi_head, (1, rep)))


def _layer(x, mem, norm_mix_w, w_in, q_norm_w, k_norm_w, attn_sink, conv_w, conv_b,
           dt_bias_f, dt_bias_b, a_log_f, a_log_b, ssd_d, ssd_norm_w, mem_norm_w,
           w_mem_kv, xq_norm_w, xk_norm_w, w_out, norm_mlp_w, w_mlp_up, w_mlp_down, tables):
    batch, seq, _ = x.shape
    mem_len = mem.shape[1]
    tokens = batch * seq
    tm = 512

    s = np.cumsum([ATT_Q_W, ATT_KV_W, ATT_KV_W, SSD_WIDTH, SSD_XBC_W, SSD_DT_W, XATT_W])
    piece = lambda lo, hi: w_in[:, lo:hi].astype(bf16)
    w_q, w_k, w_v = piece(0, s[0]), piece(s[0], s[1]), piece(s[1], s[2])
    w_z, w_xbc, w_dt, w_qx = (piece(s[2], s[3]), piece(s[3], s[4]),
                              piece(s[4], s[5]), piece(s[5], s[6]))
    hd = ATT_HEAD_DIM
    dup = lambda w: jnp.concatenate(
        [w[:, g * hd:(g + 1) * hd] for g in range(ATT_KV_HEADS) for _ in range(2)], axis=1)
    w_vdt = jnp.concatenate(
        [w_v, w_dt, jnp.zeros((D_MODEL, DT_PAD_W - SSD_DT_W), bf16)], axis=1)
    weights = (w_q, dup(w_k), w_vdt, w_z, w_xbc, w_qx)

    pad = jnp.zeros((DT_PAD_W - SSD_DT_W,), f32)
    dt_bias_row = jnp.concatenate([dt_bias_f, dt_bias_b, pad]).reshape(1, DT_PAD_W)
    a_row = jnp.concatenate([-jnp.exp(a_log_f), -jnp.exp(a_log_b), pad]).reshape(1, DT_PAD_W)
    conv_w8 = jnp.concatenate([conv_w, jnp.zeros((8 - SSD_CONV, SSD_XBC_W), f32)], axis=0)
    dmat = (ssd_d.astype(f32)[:, None, None] * jnp.eye(SSD_CHUNK, dtype=f32)).astype(bf16)

    x2 = x.reshape(tokens, D_MODEL)
    cos_t, sinlo_t, sinhi_t = tables
    q, kdup, vt, z, xc, qx, rows, rowsl, cols, dtT = _in_proj(
        x2, norm_mix_w.reshape(1, D_MODEL), weights,
        jnp.tile(q_norm_w, ATT_HEADS).reshape(1, ATT_Q_W),
        jnp.tile(k_norm_w, KDUP_W // hd).reshape(1, KDUP_W),
        xq_norm_w.reshape(1, XATT_HEAD_DIM), cos_t, sinlo_t, sinhi_t,
        conv_w8, conv_b.reshape(1, SSD_XBC_W), dt_bias_row, a_row, seq, tm)

    attn = _attention(attn_sink.astype(f32), q, kdup, vt, batch, seq)
    ssd = _ssd(dmat, xc, rows, rowsl, cols, dtT, batch, seq)

    km, vm = _mem_kv(mem.reshape(batch * mem_len, D_MODEL), mem_norm_w.reshape(1, D_MODEL),
                     w_mem_kv.astype(bf16), xk_norm_w.reshape(1, XATT_HEAD_DIM), batch, mem_len)
    xatt = _xattn(qx, km, vm, batch, seq, mem_len, min(XATT_TQ, seq))

    out = _out_mlp(x2, attn, ssd, z, xatt, w_out.astype(bf16),
                   ssd_norm_w.reshape(1, SSD_WIDTH), norm_mlp_w.reshape(1, D_MODEL),
                   w_mlp_up.astype(bf16), w_mlp_down.astype(bf16), tm)
    return out.reshape(batch, seq, D_MODEL)


def kernel(x, mem, norm_mix_w, w_in, q_norm_w, k_norm_w, attn_sink, conv_w, conv_b,
           dt_bias_f, dt_bias_b, a_log_f, a_log_b, ssd_d, ssd_norm_w, mem_norm_w,
           w_mem_kv, xq_norm_w, xk_norm_w, w_out, norm_mlp_w, w_mlp_up, w_mlp_down):
    depth = w_in.shape[0]
    tables = _rope_tables(x.shape[1])
    for i in range(depth):
        x = _layer(x, mem, norm_mix_w[i], w_in[i], q_norm_w[i], k_norm_w[i], attn_sink[i],
                   conv_w[i], conv_b[i], dt_bias_f[i], dt_bias_b[i], a_log_f[i], a_log_b[i],
                   ssd_d[i], ssd_norm_w[i], mem_norm_w[i], w_mem_kv[i], xq_norm_w[i],
                   xk_norm_w[i], w_out[i], norm_mlp_w[i], w_mlp_up[i], w_mlp_down[i], tables)
    return x
```

```python
import functools

import numpy as np
import jax
import jax.numpy as jnp
from jax import lax
from jax.experimental import pallas as pl
from jax.experimental.pallas import tpu as pltpu

D_MODEL = 1024
EPS = 1e-6

ATT_HEADS = 8
ATT_KV_HEADS = 2
ATT_HEAD_DIM = 64
ATT_Q_W = ATT_HEADS * ATT_HEAD_DIM
ATT_KV_W = ATT_KV_HEADS * ATT_HEAD_DIM
WINDOW = 128
ATT_BLOCK = 128
ATT_BLOCKS_PER_STEP = 8
ATT_HEADS_PER_CHAIN = 4
ROPE_THETA = 500000.0
ROPE_DIM = ATT_HEAD_DIM // 4

SSD_HEADS = 16
SSD_HEAD_DIM = 64
SSD_WIDTH = SSD_HEADS * SSD_HEAD_DIM
SSD_GROUPS = 2
SSD_STATE = 128
SSD_CONV = 5
SSD_CHUNK = 128
SSD_CHUNKS_PER_STEP = 8
SSD_XBC_W = SSD_WIDTH + 2 * SSD_GROUPS * SSD_STATE
SSD_DT_W = 2 * SSD_HEADS

XATT_HEADS = 4
XATT_HEAD_DIM = 128
XATT_W = XATT_HEADS * XATT_HEAD_DIM
XATT_TQ = 1024

D_FF = 4 * D_MODEL
D_MIX = 2 * D_MODEL

LANES = 128
BF16_SUBLANES = 16
F32_SUBLANES = 8
VMEM_LIMIT = 56 * 1024 * 1024

KDUP_W = 2 * ATT_KV_W
DT_PAD_W = LANES
C_Q = 0
C_K = C_Q + ATT_Q_W
C_V = C_K + KDUP_W
C_Z = C_V + KDUP_W
C_XBC = C_Z + SSD_WIDTH
C_QX = C_XBC + SSD_XBC_W
C_DT = C_QX + XATT_W
D_IN_P = C_DT + DT_PAD_W

CONV_HALO = BF16_SUBLANES
C_B = SSD_WIDTH
C_C = SSD_WIDTH + SSD_GROUPS * SSD_STATE

NEG_BIG = -1e30
LOG2E = 1.4426950408889634

bf16 = jnp.bfloat16
f32 = jnp.float32


def _dot(a, b):
    return jnp.dot(a, b, preferred_element_type=f32)


def _dot_nt(a, b):
    return lax.dot_general(a, b, (((1,), (1,)), ((), ())), preferred_element_type=f32)


def _split3(x):
    x1 = x.astype(bf16)
    r1 = x - x1.astype(f32)
    x2 = r1.astype(bf16)
    r2 = r1 - x2.astype(f32)
    return x1, x2, r2.astype(bf16)


def _dot3(x, m01):
    x1, x2, x3 = _split3(x)
    return _dot(x1, m01) + _dot(x2, m01) + _dot(x3, m01)


def _iota(shape, dim):
    return lax.broadcasted_iota(jnp.int32, shape, dim)


def _segment_ones(width, seg):
    r = _iota((width, width), 0) // seg
    c = _iota((width, width), 1) // seg
    return jnp.where(r == c, 1.0, 0.0).astype(bf16)


def _head_mean_sq(t, seg):
    ones = _segment_ones(t.shape[1], seg)
    return _dot((t * t).astype(bf16), ones) * (1.0 / seg)


def _rope(t, cos, sin_lo, sin_hi):
    half = ROPE_DIM // 2
    return (t * cos + pltpu.roll(t, LANES - half, 1) * sin_lo
            + pltpu.roll(t, half, 1) * sin_hi)


def _scheduled_after(x, anchor):
    folded = anchor[:, 0:LANES]
    for c0 in range(LANES, anchor.shape[1], LANES):
        folded = folded + anchor[:, c0:c0 + LANES]
    bits = pltpu.bitcast(jnp.broadcast_to(folded, (F32_SUBLANES, LANES)), jnp.uint32)
    sixteen = jnp.full(bits.shape, 16, jnp.uint32)
    zero = lax.shift_right_logical(lax.shift_right_logical(bits, sixteen), sixteen)
    zero = jnp.concatenate([zero.astype(f32)] * 2, axis=0).astype(bf16)
    head = x[0:BF16_SUBLANES, :] + jnp.concatenate([zero] * (x.shape[1] // LANES), axis=1)
    return jnp.concatenate([head, x[BF16_SUBLANES:, :]], axis=0)


def _rms_rows(x, w):
    ms = jnp.mean(x * x, axis=-1, keepdims=True)
    return (x * lax.rsqrt(ms + EPS)) * w


def _in_proj_body(pos_blocks, x_ref, xp_ref, xn_ref, nw_ref,
                  wq_ref, wk_ref, wvdt_ref, wz_ref, wxbc_ref, wqx_ref, qnw_ref, knw_ref,
                  xqnw_ref, cos_ref, sinlo_ref, sinhi_ref, cw_ref, cb_ref, dtb_ref, arow_ref,
                  q_ref, k_ref, v_ref, z_ref, xc_ref, qx_ref,
                  rows_ref, rowsl_ref, cols_ref, dtT_ref, conv_s):
    i = pl.program_id(0)
    tm = x_ref.shape[0]
    si = i % pos_blocks
    nw = nw_ref[...]
    h = _rms_rows(x_ref[...], nw).astype(bf16)

    cos = cos_ref[...]
    sin_lo = sinlo_ref[...]
    sin_hi = sinhi_ref[...]

    pq = _dot(h, wq_ref[...])
    qn = pq * lax.rsqrt(_head_mean_sq(pq, ATT_HEAD_DIM) + EPS) * qnw_ref[...]
    for c in range(ATT_Q_W // LANES):
        sl = slice(c * LANES, (c + 1) * LANES)
        q_ref[:, sl] = (_rope(qn[:, sl], cos, sin_lo, sin_hi)
                        * (ATT_HEAD_DIM ** -0.5 * LOG2E)).astype(bf16)

    pk = _dot(h, wk_ref[...])
    kn = pk * lax.rsqrt(_head_mean_sq(pk, ATT_HEAD_DIM) + EPS) * knw_ref[...]
    for c in range(KDUP_W // LANES):
        sl = slice(c * LANES, (c + 1) * LANES)
        k_ref[:, sl] = _rope(kn[:, sl], cos, sin_lo, sin_hi).astype(bf16)

    pvdt = _dot(h, wvdt_ref[...])
    v_ref[...] = pvdt[:, 0:ATT_KV_W].T.astype(bf16)
    z_ref[...] = _dot(h, wz_ref[...]).astype(bf16)

    pqx = _dot(h, wqx_ref[...])
    for c in range(XATT_HEADS):
        sl = slice(c * XATT_HEAD_DIM, (c + 1) * XATT_HEAD_DIM)
        t = pqx[:, sl]
        tn = t * lax.rsqrt(jnp.mean(t * t, axis=-1, keepdims=True) + EPS) * xqnw_ref[...]
        qx_ref[:, sl] = (tn * (XATT_HEAD_DIM ** -0.5 * LOG2E)).astype(bf16)

    hp = _rms_rows(xp_ref[...], nw)
    hn = _rms_rows(xn_ref[...], nw)
    hp = jnp.where(si > 0, hp, jnp.zeros_like(hp)).astype(bf16)
    hn = jnp.where(si < pos_blocks - 1, hn, jnp.zeros_like(hn)).astype(bf16)
    h_ext = jnp.concatenate([hp, h, hn], axis=0)
    conv_s[...] = _dot(h_ext, wxbc_ref[...])
    sub = F32_SUBLANES
    nt = tm // sub
    t0 = CONV_HALO // sub
    u3 = conv_s[...].reshape((tm + 2 * CONV_HALO) // sub, sub, SSD_XBC_W)
    row_in_tile = _iota((1, sub, SSD_XBC_W), 1)
    acc = jnp.broadcast_to(cb_ref[...].reshape(1, 1, SSD_XBC_W), (nt, sub, SSD_XBC_W))
    for k in range(SSD_CONV):
        d = k - SSD_CONV // 2
        wk = cw_ref[k:k + 1, :].reshape(1, 1, SSD_XBC_W)
        if d == 0:
            shifted = u3[t0:t0 + nt]
        elif d > 0:
            rot = pltpu.roll(u3, sub - d, 1)
            shifted = jnp.where(row_in_tile < sub - d, rot[t0:t0 + nt], rot[t0 + 1:t0 + nt + 1])
        else:
            rot = pltpu.roll(u3, -d, 1)
            shifted = jnp.where(row_in_tile >= -d, rot[t0:t0 + nt], rot[t0 - 1:t0 + nt - 1])
        acc = acc + shifted * wk
    xc = acc / (1.0 + jnp.exp2(acc * (-LOG2E)))
    xc_ref[...] = xc.reshape(tm, SSD_XBC_W).astype(bf16)

    Q = SSD_CHUNK
    dt = pvdt[:, ATT_KV_W:ATT_KV_W + DT_PAD_W] + dtb_ref[...]
    dt = jnp.maximum(dt, 0.0) + jnp.log1p(jnp.exp(-jnp.abs(dt)))
    a = dt * arow_ref[...]
    li = _iota((Q, Q), 0)
    lj = _iota((Q, Q), 1)
    incl_le = jnp.where(li <= lj, 1.0, 0.0).astype(bf16)
    incl_ge = jnp.where(li >= lj, 1.0, 0.0).astype(bf16)
    pad_rows = jnp.zeros((Q - 2 * SSD_HEADS, Q), f32)
    for ch in range(tm // Q):
        r = slice(ch * Q, (ch + 1) * Q)
        aT = a[r].T
        dtT = dt[r].T
        pre = _dot3(aT[0:SSD_HEADS], incl_le)
        suf = _dot3(aT[SSD_HEADS:2 * SSD_HEADS], incl_ge)
        rows = jnp.concatenate([pre, suf, pad_rows], axis=0) * LOG2E
        rows_ref[r, :] = rows
        rowsl_ref[r, :] = rows - jnp.log(dtT) * LOG2E
        cols_ref[r, :] = rows.T
        dtT_ref[r, :] = dtT


def _in_proj(x2, norm_w, weights, qnw, knw, xqnw, cos_t, sinlo_t, sinhi_t,
             conv_w8, conv_b, dt_bias_row, a_row, seq, tm):
    tokens = x2.shape[0]
    pos_blocks = seq // tm
    hpt = tm // CONV_HALO
    n_halo = tokens // CONV_HALO
    row = lambda i: (i, 0)
    const = lambda i: (0, 0)
    pos = lambda i: (i % pos_blocks, 0)
    prev = lambda i: (jnp.maximum(i * hpt - 1, 0), 0)
    nxt = lambda i: (jnp.minimum((i + 1) * hpt, n_halo - 1), 0)
    out_widths = [ATT_Q_W, KDUP_W, KDUP_W, SSD_WIDTH, SSD_XBC_W, XATT_W,
                  LANES, LANES, LANES, LANES]
    out_dtypes = [bf16, bf16, bf16, bf16, bf16, bf16, f32, f32, f32, f32]
    out_specs = [pl.BlockSpec((tm, w), row) for w in out_widths]
    out_shape = [jax.ShapeDtypeStruct((tokens, w), d) for w, d in zip(out_widths, out_dtypes)]
    out_specs[2] = pl.BlockSpec((ATT_KV_W, tm), lambda i: (i // pos_blocks, i % pos_blocks))
    out_shape[2] = jax.ShapeDtypeStruct((tokens // seq * ATT_KV_W, seq), bf16)
    return pl.pallas_call(
        functools.partial(_in_proj_body, pos_blocks),
        grid=(tokens // tm,),
        in_specs=[
            pl.BlockSpec((tm, D_MODEL), row),
            pl.BlockSpec((CONV_HALO, D_MODEL), prev),
            pl.BlockSpec((CONV_HALO, D_MODEL), nxt),
            pl.BlockSpec((1, D_MODEL), const),
            *[pl.BlockSpec(w.shape, const, pipeline_mode=pl.Buffered(1)) for w in weights],
            pl.BlockSpec((1, ATT_Q_W), const),
            pl.BlockSpec((1, KDUP_W), const),
            pl.BlockSpec((1, XATT_HEAD_DIM), const),
            pl.BlockSpec((tm, LANES), pos),
            pl.BlockSpec((tm, LANES), pos),
            pl.BlockSpec((tm, LANES), pos),
            pl.BlockSpec((8, SSD_XBC_W), const),
            pl.BlockSpec((1, SSD_XBC_W), const),
            pl.BlockSpec((1, DT_PAD_W), const),
            pl.BlockSpec((1, DT_PAD_W), const),
        ],
        out_specs=out_specs,
        out_shape=out_shape,
        scratch_shapes=[pltpu.VMEM((tm + 2 * CONV_HALO, SSD_XBC_W), f32)],
        compiler_params=pltpu.CompilerParams(
            dimension_semantics=("arbitrary",), vmem_limit_bytes=VMEM_LIMIT),
        name="in_proj",
    )(x2, x2, x2, norm_w, *weights, qnw, knw, xqnw, cos_t, sinlo_t, sinhi_t,
      conv_w8, conv_b, dt_bias_row, a_row)


def _attn_kernel(sink_ref, mfirst_ref, mmid_ref, mlast_ref, q_ref, kp_ref, kc_ref, kn_ref,
                 vp_ref, vc_ref, vn_ref, o_ref):
    blk = ATT_BLOCK
    per = q_ref.shape[0] // blk
    heads_per_kv = ATT_HEADS // ATT_KV_HEADS
    hpc = ATT_HEADS_PER_CHAIN
    cols = hpc * blk
    low_lanes = _iota((blk, LANES), 1) < ATT_HEAD_DIM
    head_of_col = _iota((1, cols), 1) // blk
    masks = [mfirst_ref] + [mmid_ref] * (per - 2) + [mlast_ref]
    half = LANES // 2

    kblocks, vblocks = [], []
    for g in range(ATT_KV_HEADS):
        gs = slice(g * LANES, (g + 1) * LANES)
        vs = slice(g * ATT_HEAD_DIM, (g + 1) * ATT_HEAD_DIM)
        kblocks.append([kp_ref[:, gs]] + [kc_ref[i * blk:(i + 1) * blk, gs] for i in range(per)]
                       + [kn_ref[:, gs]])
        vblocks.append([vp_ref[vs, :]] + [vc_ref[vs, i * blk:(i + 1) * blk] for i in range(per)]
                       + [vn_ref[vs, :]])
    ones_rows = jnp.ones((LANES - ATT_HEAD_DIM, 3 * blk), bf16)

    def scores(g, t, h0):
        qrows = slice(t * blk, (t + 1) * blk)
        kcat = jnp.concatenate(kblocks[g][t:t + 3], axis=0)
        qs = []
        sink = jnp.zeros((1, cols), f32)
        for j in range(hpc):
            hd = h0 + j
            pair = q_ref[qrows, (hd // 2) * LANES:(hd // 2 + 1) * LANES]
            keep = low_lanes if hd % 2 == 0 else jnp.logical_not(low_lanes)
            qs.append(jnp.where(keep, pair, jnp.zeros_like(pair)))
            sink = jnp.where(head_of_col == j, sink_ref[hd] * LOG2E, sink)
        qg = jnp.concatenate(qs, axis=0)
        raw = _dot_nt(kcat, qg)
        st = raw + masks[t][:, 0:cols]
        m = jnp.maximum(jnp.max(st, axis=0, keepdims=True), sink)
        return st, m, sink, raw[3 * blk - 1:3 * blk, :]

    def finish(g, t, h0, st, m, sink, tail_next):
        qrows = slice(t * blk, (t + 1) * blk)
        vt = jnp.concatenate(
            [jnp.concatenate(vblocks[g][t:t + 3], axis=1), ones_rows], axis=0)
        if tail_next is not None:
            vt = _scheduled_after(vt, tail_next)
        p = jnp.exp2(st - m).astype(bf16)
        ov = _dot(vt, p)
        den = ov[ATT_HEAD_DIM:ATT_HEAD_DIM + 1] + jnp.exp2(sink - m)
        ot = ov[0:ATT_HEAD_DIM] / den
        for jp in range(hpc // 2):
            even = ot[:, (2 * jp) * blk:(2 * jp + 1) * blk]
            odd = ot[:, (2 * jp + 1) * blk:(2 * jp + 2) * blk]
            pr = h0 // 2 + jp
            o_ref[qrows, pr * LANES:(pr + 1) * LANES] = (
                jnp.concatenate([even, odd], axis=0).T.astype(bf16))

    chains = [(g, t, g * heads_per_kv + c * hpc) for g in range(ATT_KV_HEADS)
              for t in range(per) for c in range(heads_per_kv // hpc)]
    nxt = scores(*chains[0])
    for idx, ch in enumerate(chains):
        st, m, sink, _ = nxt
        nxt = scores(*chains[idx + 1]) if idx + 1 < len(chains) else None
        finish(*ch, st, m, sink, None if nxt is None else nxt[3])


def _attn_masks():
    blk = ATT_BLOCK
    cols = (ATT_HEADS // ATT_KV_HEADS) * blk
    key = np.arange(3 * blk)[:, None]
    qi = np.arange(cols)[None, :] % blk
    band = (key >= qi) & (key <= qi + 2 * WINDOW)
    variants = [band & (key >= blk), band, band & (key < 2 * blk)]
    return jnp.asarray(np.where(np.stack(variants), 0.0, NEG_BIG), dtype=f32)


def _attention(sink, q, kdup, vt, batch, seq):
    blk = ATT_BLOCK
    nb = seq // blk
    per = min(ATT_BLOCKS_PER_STEP, nb)
    ns = nb // per
    cols = (ATT_HEADS // ATT_KV_HEADS) * blk
    masks = _attn_masks()
    cur = lambda b, s: (b * ns + s, 0)
    prev = lambda b, s: (b * nb + jnp.maximum(per * s - 1, 0), 0)
    nxt = lambda b, s: (b * nb + jnp.minimum(per * s + per, nb - 1), 0)
    vcur = lambda b, s: (b, s)
    vprev = lambda b, s: (b, jnp.maximum(per * s - 1, 0))
    vnxt = lambda b, s: (b, jnp.minimum(per * s + per, nb - 1))
    mfirst = lambda b, s: (jnp.where(s == 0, 0, 1), 0, 0)
    mmid = lambda b, s: (1, 0, 0)
    mlast = lambda b, s: (jnp.where(s == ns - 1, 2, 1), 0, 0)
    return pl.pallas_call(
        _attn_kernel,
        grid=(batch, ns),
        in_specs=[
            pl.BlockSpec(memory_space=pltpu.SMEM),
            pl.BlockSpec((None, 3 * blk, cols), mfirst),
            pl.BlockSpec((None, 3 * blk, cols), mmid),
            pl.BlockSpec((None, 3 * blk, cols), mlast),
            pl.BlockSpec((per * blk, ATT_Q_W), cur),
            pl.BlockSpec((blk, KDUP_W), prev),
            pl.BlockSpec((per * blk, KDUP_W), cur),
            pl.BlockSpec((blk, KDUP_W), nxt),
            pl.BlockSpec((ATT_KV_W, blk), vprev),
            pl.BlockSpec((ATT_KV_W, per * blk), vcur),
            pl.BlockSpec((ATT_KV_W, blk), vnxt),
        ],
        out_specs=pl.BlockSpec((per * blk, ATT_Q_W), cur),
        out_shape=jax.ShapeDtypeStruct((batch * seq, ATT_Q_W), bf16),
        compiler_params=pltpu.CompilerParams(
            dimension_semantics=("arbitrary", "arbitrary"), vmem_limit_bytes=VMEM_LIMIT),
        name="attention",
    )(sink, masks, masks, masks, q, kdup, kdup, kdup, vt, vt, vt)


def _ssd_kernel(xc_ref, rows_ref, rowsl_ref, cols_ref, dtT_ref, dmat_ref, y_ref,
                hf_s, sb_s, hf_run, hb_run):
    ph = pl.program_id(1)
    step = pl.program_id(2)
    nsteps = pl.num_programs(2)
    Q = SSD_CHUNK
    per = xc_ref.shape[0] // Q
    heads_per_group = SSD_HEADS // SSD_GROUPS
    low_lanes = _iota((Q, LANES), 1) < SSD_HEAD_DIM
    low_row = _iota((1, LANES), 1) < SSD_HEAD_DIM
    li = _iota((Q, Q), 0)
    si = _iota((Q, Q), 1)

    def chunk_states(c, rs):
        rows = rows_ref[rs, :]
        dtT = dtT_ref[rs, :]
        pre = rows[0:SSD_HEADS]
        suf = rows[SSD_HEADS:2 * SSD_HEADS]
        pre_end = jnp.broadcast_to(pre[:, Q - 1:Q], (SSD_HEADS, Q))
        suf_end = jnp.broadcast_to(suf[:, 0:1], (SSD_HEADS, Q))
        wf = jnp.exp2(pre_end - pre) * dtT[0:SSD_HEADS]
        wb = jnp.exp2(suf_end - suf) * dtT[SSD_HEADS:2 * SSD_HEADS]
        decf = jnp.exp2(pre_end)

        for g in range(SSD_GROUPS):
            bT = xc_ref[rs, C_B + g * SSD_STATE:C_B + (g + 1) * SSD_STATE].astype(f32).T
            for m in range(heads_per_group // 2):
                h0 = g * heads_per_group + 2 * m
                h1 = h0 + 1
                lhs = jnp.concatenate(
                    [bT * wf[h0:h0 + 1], bT * wb[h0:h0 + 1],
                     bT * wf[h1:h1 + 1], bT * wb[h1:h1 + 1]], axis=0).astype(bf16)
                cs = slice(h0 * SSD_HEAD_DIM, h0 * SSD_HEAD_DIM + LANES)
                res = _dot(lhs, xc_ref[rs, cs])
                s_f = jnp.where(low_lanes, res[0:Q], res[2 * Q:3 * Q])
                s_b = jnp.where(low_lanes, res[Q:2 * Q], res[3 * Q:4 * Q])
                dec = jnp.where(low_row, decf[h0:h0 + 1], decf[h1:h1 + 1])
                prev = hf_run[:, cs]
                hf_s[c, :, cs] = prev.astype(bf16)
                hf_run[:, cs] = dec * prev + s_f
                sb_s[c, :, cs] = s_b.astype(bf16)

    def chunk_outputs(c, rs):
        rows = rows_ref[rs, :]
        suf = rows[SSD_HEADS:2 * SSD_HEADS]
        cols = cols_ref[rs, :]
        rowsl = rowsl_ref[rs, :]
        decb = jnp.exp2(jnp.broadcast_to(suf[:, 0:1], (SSD_HEADS, Q)))
        lower = li >= si
        upper = si >= li

        for g in range(SSD_GROUPS):
            bg = xc_ref[rs, C_B + g * SSD_STATE:C_B + (g + 1) * SSD_STATE]
            cg = xc_ref[rs, C_C + g * SSD_STATE:C_C + (g + 1) * SSD_STATE]
            cbm = _dot_nt(cg, bg).astype(bf16)
            for m in range(heads_per_group // 2):
                h0 = g * heads_per_group + 2 * m
                parts = []
                for hd in (h0, h0 + 1):
                    hb = SSD_HEADS + hd
                    colf = jnp.broadcast_to(cols[:, hd:hd + 1], (Q, Q))
                    colb = jnp.broadcast_to(cols[:, hb:hb + 1], (Q, Q))
                    ef = jnp.exp2(jnp.where(lower, colf - rowsl[hd:hd + 1], NEG_BIG)).astype(bf16)
                    eb = jnp.exp2(jnp.where(upper, colb - rowsl[hb:hb + 1], NEG_BIG)).astype(bf16)
                    gm = cbm * (ef + eb) + dmat_ref[hd]
                    parts.append(jnp.concatenate(
                        [gm, cg * jnp.exp2(colf).astype(bf16), cg * jnp.exp2(colb).astype(bf16)],
                        axis=1))
                lhs = jnp.concatenate(parts, axis=0)
                cs = slice(h0 * SSD_HEAD_DIM, h0 * SSD_HEAD_DIM + LANES)
                hb_prev = hb_run[:, cs]
                rhs = jnp.concatenate(
                    [xc_ref[rs, cs], hf_s[c, :, cs], hb_prev.astype(bf16)], axis=0)
                res = _dot(lhs, rhs)
                y_ref[rs, cs] = jnp.where(low_lanes, res[0:Q], res[Q:2 * Q]).astype(bf16)
                dec = jnp.where(low_row, decb[h0:h0 + 1], decb[h0 + 1:h0 + 2])
                hb_run[:, cs] = dec * hb_prev + sb_s[c, :, cs].astype(f32)

    @pl.when(ph == 0)
    def _phase0():
        @pl.when(step == 0)
        def _():
            hf_run[...] = jnp.zeros_like(hf_run)

        for sub in range(per):
            chunk_states(step * per + sub, slice(sub * Q, (sub + 1) * Q))

    @pl.when(ph == 1)
    def _phase1():
        @pl.when(step == 0)
        def _():
            hb_run[...] = jnp.zeros_like(hb_run)

        back_step = nsteps - 1 - step
        for sub in reversed(range(per)):
            chunk_outputs(back_step * per + sub, slice(sub * Q, (sub + 1) * Q))


def _ssd(dmat, xc, rows, rowsl, cols, dtT, batch, seq):
    Q = SSD_CHUNK
    nc = seq // Q
    rb = min(SSD_CHUNKS_PER_STEP, nc) * Q
    ns = seq // rb
    last = ns - 1

    def both_map(b, ph, s):
        return (b * ns + s * (1 - ph) + (last - s) * ph, 0)

    def fwd_map(b, ph, s):
        return (b * ns + s * (1 - ph) + last * ph, 0)

    def back_map(b, ph, s):
        return (b * ns + last - s * ph, 0)

    return pl.pallas_call(
        _ssd_kernel,
        grid=(batch, 2, ns),
        in_specs=[
            pl.BlockSpec((rb, SSD_XBC_W), both_map),
            pl.BlockSpec((rb, LANES), both_map),
            pl.BlockSpec((rb, LANES), back_map),
            pl.BlockSpec((rb, LANES), back_map),
            pl.BlockSpec((rb, LANES), fwd_map),
            pl.BlockSpec((SSD_HEADS, Q, Q), lambda b, ph, s: (0, 0, 0)),
        ],
        out_specs=pl.BlockSpec((rb, SSD_WIDTH), back_map),
        out_shape=jax.ShapeDtypeStruct((batch * seq, SSD_WIDTH), bf16),
        scratch_shapes=[
            pltpu.VMEM((nc, SSD_STATE, SSD_WIDTH), bf16),
            pltpu.VMEM((nc, SSD_STATE, SSD_WIDTH), bf16),
            pltpu.VMEM((SSD_STATE, SSD_WIDTH), f32),
            pltpu.VMEM((SSD_STATE, SSD_WIDTH), f32),
        ],
        compiler_params=pltpu.CompilerParams(
            dimension_semantics=("arbitrary", "arbitrary", "arbitrary"),
            vmem_limit_bytes=VMEM_LIMIT),
        name="ssd",
    )(xc, rows, rowsl, cols, dtT, dmat)


def _mem_kv_kernel(mem_ref, nw_ref, w_ref, knw_ref, k_ref, v_ref):
    h = _rms_rows(mem_ref[...], nw_ref[...]).astype(bf16)
    kv = _dot(h, w_ref[...])
    for c in range(XATT_HEADS):
        sl = slice(c * XATT_HEAD_DIM, (c + 1) * XATT_HEAD_DIM)
        t = kv[:, sl]
        k_ref[:, sl] = (t * lax.rsqrt(jnp.mean(t * t, axis=-1, keepdims=True) + EPS)
                        * knw_ref[...]).astype(bf16)
    v_ref[...] = kv[:, XATT_W:].astype(bf16)


def _mem_kv(mem2, mem_norm_w, w_kv, xk_norm_w, batch, mem_len):
    row = lambda b: (b, 0)
    const = lambda b: (0, 0)
    return pl.pallas_call(
        _mem_kv_kernel,
        grid=(batch,),
        in_specs=[
            pl.BlockSpec((mem_len, D_MODEL), row),
            pl.BlockSpec((1, D_MODEL), const),
            pl.BlockSpec((D_MODEL, 2 * XATT_W), const),
            pl.BlockSpec((1, XATT_HEAD_DIM), const),
        ],
        out_specs=[pl.BlockSpec((mem_len, XATT_W), row)] * 2,
        out_shape=[jax.ShapeDtypeStruct((batch * mem_len, XATT_W), bf16)] * 2,
        compiler_params=pltpu.CompilerParams(
            dimension_semantics=("arbitrary",), vmem_limit_bytes=VMEM_LIMIT),
        name="mem_kv",
    )(mem2, mem_norm_w, w_kv, xk_norm_w)


def _xattn_kernel(q_ref, k_ref, v_ref, o_ref):
    tq = q_ref.shape[0]
    head = lambda hd: slice(hd * XATT_HEAD_DIM, (hd + 1) * XATT_HEAD_DIM)
    scores = lambda hd: _dot_nt(q_ref[:, head(hd)], k_ref[:, head(hd)])
    s_next = scores(0)
    for hd in range(XATT_HEADS):
        s = s_next
        v = v_ref[:, head(hd)]
        if hd + 1 < XATT_HEADS:
            s_next = scores(hd + 1)
            v = _scheduled_after(v, s_next[tq - 1:tq, :])
        m = jnp.max(s, axis=-1, keepdims=True)
        p = jnp.exp2(s - m)
        den = jnp.sum(p, axis=-1, keepdims=True)
        o_ref[:, head(hd)] = (_dot(p.astype(bf16), v) / den).astype(bf16)


def _xattn(qx, km, vm, batch, seq, mem_len, tq):
    nq = seq // tq
    qmap = lambda b, i: (b * nq + i, 0)
    mmap = lambda b, i: (b, 0)
    return pl.pallas_call(
        _xattn_kernel,
        grid=(batch, nq),
        in_specs=[
            pl.BlockSpec((tq, XATT_W), qmap),
            pl.BlockSpec((mem_len, XATT_W), mmap),
            pl.BlockSpec((mem_len, XATT_W), mmap),
        ],
        out_specs=pl.BlockSpec((tq, XATT_W), qmap),
        out_shape=jax.ShapeDtypeStruct((batch * seq, XATT_W), bf16),
        compiler_params=pltpu.CompilerParams(
            dimension_semantics=("arbitrary", "arbitrary"), vmem_limit_bytes=VMEM_LIMIT),
        name="xattn",
    )(qx, km, vm)


FF_CHUNK = 1024


def _out_mlp_kernel(x_ref, a_ref, s_ref, z_ref, c_ref, wo_ref, snw_ref, nw_ref, wu_ref, wd_ref,
                    o_ref):
    zf = z_ref[...].astype(f32)
    y = s_ref[...].astype(f32) * (zf / (1.0 + jnp.exp2(zf * (-LOG2E))))
    gw = SSD_WIDTH // SSD_GROUPS
    mix = (_dot(a_ref[...], wo_ref[0:ATT_Q_W, :])
           + _dot(c_ref[...], wo_ref[ATT_Q_W + SSD_WIDTH:, :]))
    for g in range(SSD_GROUPS):
        yg = y[:, g * gw:(g + 1) * gw]
        ms = jnp.mean(yg * yg, axis=-1, keepdims=True)
        yn = (yg * lax.rsqrt(ms + EPS) * snw_ref[:, g * gw:(g + 1) * gw]).astype(bf16)
        mix = mix + _dot(yn, wo_ref[ATT_Q_W + g * gw:ATT_Q_W + (g + 1) * gw, :])
    x1 = x_ref[...] + mix
    h = _rms_rows(x1, nw_ref[...]).astype(bf16)
    acc = x1
    for j in range(D_FF // FF_CHUNK):
        u = _dot(h, wu_ref[:, j * FF_CHUNK:(j + 1) * FF_CHUNK])
        r = jnp.maximum(u, 0.0)
        acc = acc + _dot((r * r).astype(bf16), wd_ref[j * FF_CHUNK:(j + 1) * FF_CHUNK, :])
    o_ref[...] = acc


def _out_mlp(x2, attn, ssd, z, xatt, w_out, ssd_norm_w, norm_w, w_up, w_down, tm):
    tokens = x2.shape[0]
    n_tiles = tokens // tm
    row = lambda i: (i, 0)
    const = lambda i: (0, 0)
    resident = functools.partial(pl.BlockSpec, index_map=const, pipeline_mode=pl.Buffered(1))
    return pl.pallas_call(
        _out_mlp_kernel,
        grid=(n_tiles,),
        in_specs=[
            pl.BlockSpec((tm, D_MODEL), row),
            pl.BlockSpec((tm, ATT_Q_W), row),
            pl.BlockSpec((tm, SSD_WIDTH), row),
            pl.BlockSpec((tm, SSD_WIDTH), row),
            pl.BlockSpec((tm, XATT_W), row),
            resident((D_MIX, D_MODEL)),
            pl.BlockSpec((1, SSD_WIDTH), const),
            pl.BlockSpec((1, D_MODEL), const),
            resident((D_MODEL, D_FF)),
            resident((D_FF, D_MODEL)),
        ],
        out_specs=pl.BlockSpec((tm, D_MODEL), row),
        out_shape=jax.ShapeDtypeStruct((tokens, D_MODEL), f32),
        compiler_params=pltpu.CompilerParams(
            dimension_semantics=("arbitrary",), vmem_limit_bytes=VMEM_LIMIT),
        name="out_mlp",
    )(x2, attn, ssd, z, xatt, w_out, ssd_norm_w, norm_w, w_up, w_down)


def _rope_tables(seq):
    half = ROPE_DIM // 2
    inv = ROPE_THETA ** (-jnp.arange(0, ROPE_DIM, 2, dtype=f32) / ROPE_DIM)
    ang = jnp.arange(seq, dtype=f32)[:, None] * inv[None, :]
    cos = jnp.cos(ang)
    sin = jnp.sin(ang)
    ones = jnp.ones((seq, ATT_HEAD_DIM - ROPE_DIM), f32)
    zeros_h = jnp.zeros((seq, half), f32)
    zeros_p = jnp.zeros((seq, ATT_HEAD_DIM - ROPE_DIM), f32)
    cos_head = jnp.concatenate([cos, cos, ones], axis=1)
    lo_head = jnp.concatenate([-sin, zeros_h, zeros_p], axis=1)
    hi_head = jnp.concatenate([zeros_h, sin, zeros_p], axis=1)
    rep = LANES // ATT_HEAD_DIM
    return (jnp.tile(cos_head, (1, rep)), jnp.tile(lo_head, (1, rep)),
            jnp.tile(hi_head, (1, rep)))


def _layer(x, mem, norm_mix_w, w_in, q_norm_w, k_norm_w, attn_sink, conv_w, conv_b,
           dt_bias_f, dt_bias_b, a_log_f, a_log_b, ssd_d, ssd_norm_w, mem_norm_w,
           w_mem_kv, xq_norm_w, xk_norm_w, w_out, norm_mlp_w, w_mlp_up, w_mlp_down, tables):
    batch, seq, _ = x.shape
    mem_len = mem.shape[1]
    tokens = batch * seq
    tm = 512

    s = np.cumsum([ATT_Q_W, ATT_KV_W, ATT_KV_W, SSD_WIDTH, SSD_XBC_W, SSD_DT_W, XATT_W])
    piece = lambda lo, hi: w_in[:, lo:hi].astype(bf16)
    w_q, w_k, w_v = piece(0, s[0]), piece(s[0], s[1]), piece(s[1], s[2])
    w_z, w_xbc, w_dt, w_qx = (piece(s[2], s[3]), piece(s[3], s[4]),
                              piece(s[4], s[5]), piece(s[5], s[6]))
    hd = ATT_HEAD_DIM
    dup = lambda w: jnp.concatenate(
        [w[:, g * hd:(g + 1) * hd] for g in range(ATT_KV_HEADS) for _ in range(2)], axis=1)
    w_vdt = jnp.concatenate(
        [w_v, w_dt, jnp.zeros((D_MODEL, DT_PAD_W - SSD_DT_W), bf16)], axis=1)
    weights = (w_q, dup(w_k), w_vdt, w_z, w_xbc, w_qx)

    pad = jnp.zeros((DT_PAD_W - SSD_DT_W,), f32)
    dt_bias_row = jnp.concatenate([dt_bias_f, dt_bias_b, pad]).reshape(1, DT_PAD_W)
    a_row = jnp.concatenate([-jnp.exp(a_log_f), -jnp.exp(a_log_b), pad]).reshape(1, DT_PAD_W)
    conv_w8 = jnp.concatenate([conv_w, jnp.zeros((8 - SSD_CONV, SSD_XBC_W), f32)], axis=0)
    dmat = (ssd_d.astype(f32)[:, None, None] * jnp.eye(SSD_CHUNK, dtype=f32)).astype(bf16)

    x2 = x.reshape(tokens, D_MODEL)
    cos_t, sinlo_t, sinhi_t = tables
    q, kdup, vt, z, xc, qx, rows, rowsl, cols, dtT = _in_proj(
        x2, norm_mix_w.reshape(1, D_MODEL), weights,
        jnp.tile(q_norm_w, ATT_HEADS).reshape(1, ATT_Q_W),
        jnp.tile(k_norm_w, KDUP_W // hd).reshape(1, KDUP_W),
        xq_norm_w.reshape(1, XATT_HEAD_DIM), cos_t, sinlo_t, sinhi_t,
        conv_w8, conv_b.reshape(1, SSD_XBC_W), dt_bias_row, a_row, seq, tm)

    attn = _attention(attn_sink.astype(f32), q, kdup, vt, batch, seq)
    ssd = _ssd(dmat, xc, rows, rowsl, cols, dtT, batch, seq)

    km, vm = _mem_kv(mem.reshape(batch * mem_len, D_MODEL), mem_norm_w.reshape(1, D_MODEL),
                     w_mem_kv.astype(bf16), xk_norm_w.reshape(1, XATT_HEAD_DIM), batch, mem_len)
    xatt = _xattn(qx, km, vm, batch, seq, mem_len, min(XATT_TQ, seq))

    out = _out_mlp(x2, attn, ssd, z, xatt, w_out.astype(bf16),
                   ssd_norm_w.reshape(1, SSD_WIDTH), norm_mlp_w.reshape(1, D_MODEL),
                   w_mlp_up.astype(bf16), w_mlp_down.astype(bf16), tm)
    return out.reshape(batch, seq, D_MODEL)


def kernel(x, mem, norm_mix_w, w_in, q_norm_w, k_norm_w, attn_sink, conv_w, conv_b,
           dt_bias_f, dt_bias_b, a_log_f, a_log_b, ssd_d, ssd_norm_w, mem_norm_w,
           w_mem_kv, xq_norm_w, xk_norm_w, w_out, norm_mlp_w, w_mlp_up, w_mlp_down):
    depth = w_in.shape[0]
    tables = _rope_tables(x.shape[1])
    for i in range(depth):
        x = _layer(x, mem, norm_mix_w[i], w_in[i], q_norm_w[i], k_norm_w[i], attn_sink[i],
                   conv_w[i], conv_b[i], dt_bias_f[i], dt_bias_b[i], a_log_f[i], a_log_b[i],
                   ssd_d[i], ssd_norm_w[i], mem_norm_w[i], w_mem_kv[i], xq_norm_w[i],
                   xk_norm_w[i], w_out[i], norm_mlp_w[i], w_mlp_up[i], w_mlp_down[i], tables)
    return x
```

```python
import functools

import numpy as np
import jax
import jax.numpy as jnp
from jax import lax
from jax.experimental import pallas as pl
from jax.experimental.pallas import tpu as pltpu

D_MODEL = 1024
EPS = 1e-6

ATT_HEADS = 8
ATT_KV_HEADS = 2
ATT_HEAD_DIM = 64
ATT_Q_W = ATT_HEADS * ATT_HEAD_DIM
ATT_KV_W = ATT_KV_HEADS * ATT_HEAD_DIM
WINDOW = 128
ATT_BLOCK = 128
ATT_BLOCKS_PER_STEP = 8
ATT_HEADS_PER_CHAIN = 4
ROPE_THETA = 500000.0
ROPE_DIM = ATT_HEAD_DIM // 4

SSD_HEADS = 16
SSD_HEAD_DIM = 64
SSD_WIDTH = SSD_HEADS * SSD_HEAD_DIM
SSD_GROUPS = 2
SSD_STATE = 128
SSD_CONV = 5
SSD_CHUNK = 128
SSD_CHUNKS_PER_STEP = 8
SSD_XBC_W = SSD_WIDTH + 2 * SSD_GROUPS * SSD_STATE
SSD_DT_W = 2 * SSD_HEADS

XATT_HEADS = 4
XATT_HEAD_DIM = 128
XATT_W = XATT_HEADS * XATT_HEAD_DIM
XATT_TQ = 1024

D_FF = 4 * D_MODEL
D_MIX = 2 * D_MODEL

LANES = 128
MXU_WIDTH = 256
BF16_SUBLANES = 16
F32_SUBLANES = 8
VMEM_LIMIT = 56 * 1024 * 1024

KDUP_W = 2 * ATT_KV_W
DT_PAD_W = LANES
C_Q = 0
C_K = C_Q + ATT_Q_W
C_V = C_K + KDUP_W
C_Z = C_V + KDUP_W
C_XBC = C_Z + SSD_WIDTH
C_QX = C_XBC + SSD_XBC_W
C_DT = C_QX + XATT_W
D_IN_P = C_DT + DT_PAD_W

CONV_HALO = BF16_SUBLANES
C_B = SSD_WIDTH
C_C = SSD_WIDTH + SSD_GROUPS * SSD_STATE

NEG_BIG = -1e30
LOG2E = 1.4426950408889634

bf16 = jnp.bfloat16
f32 = jnp.float32


def _dot(a, b):
    return jnp.dot(a, b, preferred_element_type=f32)


def _dot_nt(a, b):
    return lax.dot_general(a, b, (((1,), (1,)), ((), ())), preferred_element_type=f32)


def _split3(x):
    x1 = x.astype(bf16)
    r1 = x - x1.astype(f32)
    x2 = r1.astype(bf16)
    r2 = r1 - x2.astype(f32)
    return x1, x2, r2.astype(bf16)


def _dot3(x, m01):
    x1, x2, x3 = _split3(x)
    return _dot(x1, m01) + _dot(x2, m01) + _dot(x3, m01)


def _iota(shape, dim):
    return lax.broadcasted_iota(jnp.int32, shape, dim)


def _segment_ones(width, seg):
    r = _iota((width, width), 0) // seg
    c = _iota((width, width), 1) // seg
    return jnp.where(r == c, 1.0, 0.0).astype(bf16)


def _head_mean_sq(t, seg):
    width = min(t.shape[1], MXU_WIDTH)
    ones = _segment_ones(width, seg)
    sq = (t * t).astype(bf16)
    sums = [_dot(sq[:, c0:c0 + width], ones) for c0 in range(0, t.shape[1], width)]
    return jnp.concatenate(sums, axis=1) * (1.0 / seg)


def _rope(t, cos, sin_lo, sin_hi):
    half = ROPE_DIM // 2
    return (t * cos + pltpu.roll(t, LANES - half, 1) * sin_lo
            + pltpu.roll(t, half, 1) * sin_hi)


def _scheduled_after(x, anchor):
    folded = anchor[:, 0:LANES]
    for c0 in range(LANES, anchor.shape[1], LANES):
        folded = folded + anchor[:, c0:c0 + LANES]
    bits = pltpu.bitcast(jnp.broadcast_to(folded, (F32_SUBLANES, LANES)), jnp.uint32)
    sixteen = jnp.full(bits.shape, 16, jnp.uint32)
    zero = lax.shift_right_logical(lax.shift_right_logical(bits, sixteen), sixteen)
    zero = jnp.concatenate([zero.astype(f32)] * 2, axis=0).astype(bf16)
    head = x[0:BF16_SUBLANES, :] + jnp.concatenate([zero] * (x.shape[1] // LANES), axis=1)
    return jnp.concatenate([head, x[BF16_SUBLANES:, :]], axis=0)


def _rms_rows(x, w):
    ms = jnp.mean(x * x, axis=-1, keepdims=True)
    return (x * lax.rsqrt(ms + EPS)) * w


def _in_proj_body(pos_blocks, x_ref, xp_ref, xn_ref, nw_ref,
                  wq_ref, wk_ref, wvdt_ref, wz_ref, wxbc_ref, wqx_ref, qnw_ref, knw_ref,
                  xqnw_ref, cos_ref, sinlo_ref, sinhi_ref, cw_ref, cb_ref, dtb_ref, arow_ref,
                  q_ref, k_ref, v_ref, z_ref, xc_ref, qx_ref,
                  rows_ref, rowsl_ref, cols_ref, dtT_ref, conv_s):
    i = pl.program_id(0)
    tm = x_ref.shape[0]
    si = i % pos_blocks
    nw = nw_ref[...]
    h = _rms_rows(x_ref[...], nw).astype(bf16)

    cos = cos_ref[...]
    sin_lo = sinlo_ref[...]
    sin_hi = sinhi_ref[...]

    pq = _dot(h, wq_ref[...])
    qn = pq * lax.rsqrt(_head_mean_sq(pq, ATT_HEAD_DIM) + EPS) * qnw_ref[...]
    for c in range(ATT_Q_W // LANES):
        sl = slice(c * LANES, (c + 1) * LANES)
        q_ref[:, sl] = (_rope(qn[:, sl], cos, sin_lo, sin_hi)
                        * (ATT_HEAD_DIM ** -0.5 * LOG2E)).astype(bf16)

    pk = _dot(h, wk_ref[...])
    kn = pk * lax.rsqrt(_head_mean_sq(pk, ATT_HEAD_DIM) + EPS) * knw_ref[...]
    for c in range(KDUP_W // LANES):
        sl = slice(c * LANES, (c + 1) * LANES)
        k_ref[:, sl] = _rope(kn[:, sl], cos, sin_lo, sin_hi).astype(bf16)

    pvdt = _dot(h, wvdt_ref[...])
    v_ref[...] = pvdt[:, 0:ATT_KV_W].T.astype(bf16)
    pz = _dot(h, wz_ref[...])
    z_ref[...] = (pz / (1.0 + jnp.exp2(pz * (-LOG2E)))).astype(bf16)

    pqx = _dot(h, wqx_ref[...])
    for c in range(XATT_HEADS):
        sl = slice(c * XATT_HEAD_DIM, (c + 1) * XATT_HEAD_DIM)
        t = pqx[:, sl]
        tn = t * lax.rsqrt(jnp.mean(t * t, axis=-1, keepdims=True) + EPS) * xqnw_ref[...]
        qx_ref[:, sl] = (tn * (XATT_HEAD_DIM ** -0.5 * LOG2E)).astype(bf16)

    hp = _rms_rows(xp_ref[...], nw)
    hn = _rms_rows(xn_ref[...], nw)
    hp = jnp.where(si > 0, hp, jnp.zeros_like(hp)).astype(bf16)
    hn = jnp.where(si < pos_blocks - 1, hn, jnp.zeros_like(hn)).astype(bf16)
    h_ext = jnp.concatenate([hp, h, hn], axis=0)
    conv_s[...] = _dot(h_ext, wxbc_ref[...])
    sub = F32_SUBLANES
    nt = tm // sub
    t0 = CONV_HALO // sub
    u3 = conv_s[...].reshape((tm + 2 * CONV_HALO) // sub, sub, SSD_XBC_W)
    row_in_tile = _iota((1, sub, SSD_XBC_W), 1)
    acc = jnp.broadcast_to(cb_ref[...].reshape(1, 1, SSD_XBC_W), (nt, sub, SSD_XBC_W))
    for k in range(SSD_CONV):
        d = k - SSD_CONV // 2
        wk = cw_ref[k:k + 1, :].reshape(1, 1, SSD_XBC_W)
        if d == 0:
            shifted = u3[t0:t0 + nt]
        elif d > 0:
            rot = pltpu.roll(u3, sub - d, 1)
            shifted = jnp.where(row_in_tile < sub - d, rot[t0:t0 + nt], rot[t0 + 1:t0 + nt + 1])
        else:
            rot = pltpu.roll(u3, -d, 1)
            shifted = jnp.where(row_in_tile >= -d, rot[t0:t0 + nt], rot[t0 - 1:t0 + nt - 1])
        acc = acc + shifted * wk
    xc = acc / (1.0 + jnp.exp2(acc * (-LOG2E)))
    xc_ref[...] = xc.reshape(tm, SSD_XBC_W).astype(bf16)

    Q = SSD_CHUNK
    dt = pvdt[:, ATT_KV_W:ATT_KV_W + DT_PAD_W] + dtb_ref[...]
    dt = jnp.maximum(dt, 0.0) + jnp.log1p(jnp.exp(-jnp.abs(dt)))
    a = dt * arow_ref[...]
    li = _iota((Q, Q), 0)
    lj = _iota((Q, Q), 1)
    incl_le = jnp.where(li <= lj, 1.0, 0.0).astype(bf16)
    incl_ge = jnp.where(li >= lj, 1.0, 0.0).astype(bf16)
    pad_rows = jnp.zeros((Q - 2 * SSD_HEADS, Q), f32)
    for ch in range(tm // Q):
        r = slice(ch * Q, (ch + 1) * Q)
        aT = a[r].T
        dtT = dt[r].T
        pre = _dot3(aT[0:SSD_HEADS], incl_le)
        suf = _dot3(aT[SSD_HEADS:2 * SSD_HEADS], incl_ge)
        rows = jnp.concatenate([pre, suf, pad_rows], axis=0) * LOG2E
        rows_ref[r, :] = rows
        rowsl_ref[r, :] = rows - jnp.log(dtT) * LOG2E
        cols_ref[r, :] = rows.T
        dtT_ref[r, :] = dtT


def _in_proj(x2, norm_w, weights, qnw, knw, xqnw, cos_t, sinlo_t, sinhi_t,
             conv_w8, conv_b, dt_bias_row, a_row, seq, tm):
    tokens = x2.shape[0]
    pos_blocks = seq // tm
    hpt = tm // CONV_HALO
    n_halo = tokens // CONV_HALO
    row = lambda i: (i, 0)
    const = lambda i: (0, 0)
    pos = lambda i: (i % pos_blocks, 0)
    prev = lambda i: (jnp.maximum(i * hpt - 1, 0), 0)
    nxt = lambda i: (jnp.minimum((i + 1) * hpt, n_halo - 1), 0)
    out_widths = [ATT_Q_W, KDUP_W, KDUP_W, SSD_WIDTH, SSD_XBC_W, XATT_W,
                  LANES, LANES, LANES, LANES]
    out_dtypes = [bf16, bf16, bf16, bf16, bf16, bf16, f32, f32, f32, f32]
    out_specs = [pl.BlockSpec((tm, w), row) for w in out_widths]
    out_shape = [jax.ShapeDtypeStruct((tokens, w), d) for w, d in zip(out_widths, out_dtypes)]
    out_specs[2] = pl.BlockSpec((ATT_KV_W, tm), lambda i: (i // pos_blocks, i % pos_blocks))
    out_shape[2] = jax.ShapeDtypeStruct((tokens // seq * ATT_KV_W, seq), bf16)
    return pl.pallas_call(
        functools.partial(_in_proj_body, pos_blocks),
        grid=(tokens // tm,),
        in_specs=[
            pl.BlockSpec((tm, D_MODEL), row),
            pl.BlockSpec((CONV_HALO, D_MODEL), prev),
            pl.BlockSpec((CONV_HALO, D_MODEL), nxt),
            pl.BlockSpec((1, D_MODEL), const),
            *[pl.BlockSpec(w.shape, const, pipeline_mode=pl.Buffered(1)) for w in weights],
            pl.BlockSpec((1, ATT_Q_W), const),
            pl.BlockSpec((1, KDUP_W), const),
            pl.BlockSpec((1, XATT_HEAD_DIM), const),
            pl.BlockSpec((tm, LANES), pos),
            pl.BlockSpec((tm, LANES), pos),
            pl.BlockSpec((tm, LANES), pos),
            pl.BlockSpec((8, SSD_XBC_W), const),
            pl.BlockSpec((1, SSD_XBC_W), const),
            pl.BlockSpec((1, DT_PAD_W), const),
            pl.BlockSpec((1, DT_PAD_W), const),
        ],
        out_specs=out_specs,
        out_shape=out_shape,
        scratch_shapes=[pltpu.VMEM((tm + 2 * CONV_HALO, SSD_XBC_W), f32)],
        compiler_params=pltpu.CompilerParams(
            dimension_semantics=("arbitrary",), vmem_limit_bytes=VMEM_LIMIT),
        name="in_proj",
    )(x2, x2, x2, norm_w, *weights, qnw, knw, xqnw, cos_t, sinlo_t, sinhi_t,
      conv_w8, conv_b, dt_bias_row, a_row)


def _attn_kernel(sink_ref, mfirst_ref, mmid_ref, mlast_ref, q_ref, kp_ref, kc_ref, kn_ref,
                 vp_ref, vc_ref, vn_ref, o_ref):
    blk = ATT_BLOCK
    per = q_ref.shape[0] // blk
    heads_per_kv = ATT_HEADS // ATT_KV_HEADS
    hpc = ATT_HEADS_PER_CHAIN
    cols = hpc * blk
    low_lanes = _iota((blk, LANES), 1) < ATT_HEAD_DIM
    head_of_col = _iota((1, cols), 1) // blk
    masks = [mfirst_ref] + [mmid_ref] * (per - 2) + [mlast_ref]
    half = LANES // 2

    kblocks, vblocks = [], []
    for g in range(ATT_KV_HEADS):
        gs = slice(g * LANES, (g + 1) * LANES)
        vs = slice(g * ATT_HEAD_DIM, (g + 1) * ATT_HEAD_DIM)
        kblocks.append([kp_ref[:, gs]] + [kc_ref[i * blk:(i + 1) * blk, gs] for i in range(per)]
                       + [kn_ref[:, gs]])
        vblocks.append([vp_ref[vs, :]] + [vc_ref[vs, i * blk:(i + 1) * blk] for i in range(per)]
                       + [vn_ref[vs, :]])
    ones_rows = jnp.ones((LANES - ATT_HEAD_DIM, 3 * blk), bf16)

    def scores(g, t, h0):
        qrows = slice(t * blk, (t + 1) * blk)
        kcat = jnp.concatenate(kblocks[g][t:t + 3], axis=0)
        qs = []
        sink = jnp.zeros((1, cols), f32)
        for j in range(hpc):
            hd = h0 + j
            pair = q_ref[qrows, (hd // 2) * LANES:(hd // 2 + 1) * LANES]
            keep = low_lanes if hd % 2 == 0 else jnp.logical_not(low_lanes)
            qs.append(jnp.where(keep, pair, jnp.zeros_like(pair)))
            sink = jnp.where(head_of_col == j, sink_ref[hd] * LOG2E, sink)
        qg = jnp.concatenate(qs, axis=0)
        raw = _dot_nt(kcat, qg)
        st = raw + masks[t][:, 0:cols]
        m = jnp.maximum(jnp.max(st, axis=0, keepdims=True), sink)
        return st, m, sink, raw[3 * blk - 1:3 * blk, :]

    def finish(g, t, h0, st, m, sink, tail_next):
        qrows = slice(t * blk, (t + 1) * blk)
        vt = jnp.concatenate(
            [jnp.concatenate(vblocks[g][t:t + 3], axis=1), ones_rows], axis=0)
        if tail_next is not None:
            vt = _scheduled_after(vt, tail_next)
        p = jnp.exp2(st - m).astype(bf16)
        ov = _dot(vt, p)
        den = ov[ATT_HEAD_DIM:ATT_HEAD_DIM + 1] + jnp.exp2(sink - m)
        ot = ov[0:ATT_HEAD_DIM] / den
        for jp in range(hpc // 2):
            even = ot[:, (2 * jp) * blk:(2 * jp + 1) * blk]
            odd = ot[:, (2 * jp + 1) * blk:(2 * jp + 2) * blk]
            pr = h0 // 2 + jp
            o_ref[qrows, pr * LANES:(pr + 1) * LANES] = (
                jnp.concatenate([even, odd], axis=0).T.astype(bf16))

    chains = [(g, t, g * heads_per_kv + c * hpc) for g in range(ATT_KV_HEADS)
              for t in range(per) for c in range(heads_per_kv // hpc)]
    nxt = scores(*chains[0])
    for idx, ch in enumerate(chains):
        st, m, sink, _ = nxt
        nxt = scores(*chains[idx + 1]) if idx + 1 < len(chains) else None
        finish(*ch, st, m, sink, None if nxt is None else nxt[3])


def _attn_masks():
    blk = ATT_BLOCK
    cols = (ATT_HEADS // ATT_KV_HEADS) * blk
    key = np.arange(3 * blk)[:, None]
    qi = np.arange(cols)[None, :] % blk
    band = (key >= qi) & (key <= qi + 2 * WINDOW)
    variants = [band & (key >= blk), band, band & (key < 2 * blk)]
    return jnp.asarray(np.where(np.stack(variants), 0.0, NEG_BIG), dtype=f32)


def _attention(sink, q, kdup, vt, batch, seq):
    blk = ATT_BLOCK
    nb = seq // blk
    per = min(ATT_BLOCKS_PER_STEP, nb)
    ns = nb // per
    cols = (ATT_HEADS // ATT_KV_HEADS) * blk
    masks = _attn_masks()
    cur = lambda b, s: (b * ns + s, 0)
    prev = lambda b, s: (b * nb + jnp.maximum(per * s - 1, 0), 0)
    nxt = lambda b, s: (b * nb + jnp.minimum(per * s + per, nb - 1), 0)
    vcur = lambda b, s: (b, s)
    vprev = lambda b, s: (b, jnp.maximum(per * s - 1, 0))
    vnxt = lambda b, s: (b, jnp.minimum(per * s + per, nb - 1))
    mfirst = lambda b, s: (jnp.where(s == 0, 0, 1), 0, 0)
    mmid = lambda b, s: (1, 0, 0)
    mlast = lambda b, s: (jnp.where(s == ns - 1, 2, 1), 0, 0)
    return pl.pallas_call(
        _attn_kernel,
        grid=(batch, ns),
        in_specs=[
            pl.BlockSpec(memory_space=pltpu.SMEM),
            pl.BlockSpec((None, 3 * blk, cols), mfirst),
            pl.BlockSpec((None, 3 * blk, cols), mmid),
            pl.BlockSpec((None, 3 * blk, cols), mlast),
            pl.BlockSpec((per * blk, ATT_Q_W), cur),
            pl.BlockSpec((blk, KDUP_W), prev),
            pl.BlockSpec((per * blk, KDUP_W), cur),
            pl.BlockSpec((blk, KDUP_W), nxt),
            pl.BlockSpec((ATT_KV_W, blk), vprev),
            pl.BlockSpec((ATT_KV_W, per * blk), vcur),
            pl.BlockSpec((ATT_KV_W, blk), vnxt),
        ],
        out_specs=pl.BlockSpec((per * blk, ATT_Q_W), cur),
        out_shape=jax.ShapeDtypeStruct((batch * seq, ATT_Q_W), bf16),
        compiler_params=pltpu.CompilerParams(
            dimension_semantics=("arbitrary", "arbitrary"), vmem_limit_bytes=VMEM_LIMIT),
        name="attention",
    )(sink, masks, masks, masks, q, kdup, kdup, kdup, vt, vt, vt)


def _ssd_kernel(xc_ref, rows_ref, rowsl_ref, cols_ref, dtT_ref, dmat_ref, y_ref,
                hf_s, sb_s, hf_run, hb_run):
    ph = pl.program_id(1)
    step = pl.program_id(2)
    nsteps = pl.num_programs(2)
    Q = SSD_CHUNK
    per = xc_ref.shape[0] // Q
    heads_per_group = SSD_HEADS // SSD_GROUPS
    low_lanes = _iota((Q, LANES), 1) < SSD_HEAD_DIM
    low_row = _iota((1, LANES), 1) < SSD_HEAD_DIM
    li = _iota((Q, Q), 0)
    si = _iota((Q, Q), 1)

    def chunk_states(c, rs):
        rows = rows_ref[rs, :]
        dtT = dtT_ref[rs, :]
        pre = rows[0:SSD_HEADS]
        suf = rows[SSD_HEADS:2 * SSD_HEADS]
        pre_end = jnp.broadcast_to(pre[:, Q - 1:Q], (SSD_HEADS, Q))
        suf_end = jnp.broadcast_to(suf[:, 0:1], (SSD_HEADS, Q))
        wf = jnp.exp2(pre_end - pre) * dtT[0:SSD_HEADS]
        wb = jnp.exp2(suf_end - suf) * dtT[SSD_HEADS:2 * SSD_HEADS]
        decf = jnp.exp2(pre_end)

        for g in range(SSD_GROUPS):
            bT = xc_ref[rs, C_B + g * SSD_STATE:C_B + (g + 1) * SSD_STATE].astype(f32).T
            for m in range(heads_per_group // 2):
                h0 = g * heads_per_group + 2 * m
                h1 = h0 + 1
                lhs = jnp.concatenate(
                    [bT * wf[h0:h0 + 1], bT * wb[h0:h0 + 1],
                     bT * wf[h1:h1 + 1], bT * wb[h1:h1 + 1]], axis=0).astype(bf16)
                cs = slice(h0 * SSD_HEAD_DIM, h0 * SSD_HEAD_DIM + LANES)
                res = _dot(lhs, xc_ref[rs, cs])
                s_f = jnp.where(low_lanes, res[0:Q], res[2 * Q:3 * Q])
                s_b = jnp.where(low_lanes, res[Q:2 * Q], res[3 * Q:4 * Q])
                dec = jnp.where(low_row, decf[h0:h0 + 1], decf[h1:h1 + 1])
                prev = hf_run[:, cs]
                hf_s[c, :, cs] = prev.astype(bf16)
                hf_run[:, cs] = dec * prev + s_f
                sb_s[c, :, cs] = s_b.astype(bf16)

    def chunk_outputs(c, rs):
        rows = rows_ref[rs, :]
        suf = rows[SSD_HEADS:2 * SSD_HEADS]
        cols = cols_ref[rs, :]
        rowsl = rowsl_ref[rs, :]
        decb = jnp.exp2(jnp.broadcast_to(suf[:, 0:1], (SSD_HEADS, Q)))
        lower = li >= si
        upper = si >= li

        for g in range(SSD_GROUPS):
            bg = xc_ref[rs, C_B + g * SSD_STATE:C_B + (g + 1) * SSD_STATE]
            cg = xc_ref[rs, C_C + g * SSD_STATE:C_C + (g + 1) * SSD_STATE]
            cbm = _dot_nt(cg, bg).astype(bf16)
            for m in range(heads_per_group // 2):
                h0 = g * heads_per_group + 2 * m
                parts = []
                for hd in (h0, h0 + 1):
                    hb = SSD_HEADS + hd
                    colf = jnp.broadcast_to(cols[:, hd:hd + 1], (Q, Q))
                    colb = jnp.broadcast_to(cols[:, hb:hb + 1], (Q, Q))
                    ef = jnp.exp2(jnp.where(lower, colf - rowsl[hd:hd + 1], NEG_BIG)).astype(bf16)
                    eb = jnp.exp2(jnp.where(upper, colb - rowsl[hb:hb + 1], NEG_BIG)).astype(bf16)
                    gm = cbm * (ef + eb) + dmat_ref[hd]
                    parts.append(jnp.concatenate(
                        [gm, cg * jnp.exp2(colf).astype(bf16), cg * jnp.exp2(colb).astype(bf16)],
                        axis=1))
                lhs = jnp.concatenate(parts, axis=0)
                cs = slice(h0 * SSD_HEAD_DIM, h0 * SSD_HEAD_DIM + LANES)
                hb_prev = hb_run[:, cs]
                rhs = jnp.concatenate(
                    [xc_ref[rs, cs], hf_s[c, :, cs], hb_prev.astype(bf16)], axis=0)
                res = _dot(lhs, rhs)
                y_ref[rs, cs] = jnp.where(low_lanes, res[0:Q], res[Q:2 * Q]).astype(bf16)
                dec = jnp.where(low_row, decb[h0:h0 + 1], decb[h0 + 1:h0 + 2])
                hb_run[:, cs] = dec * hb_prev + sb_s[c, :, cs].astype(f32)

    @pl.when(ph == 0)
    def _phase0():
        @pl.when(step == 0)
        def _():
            hf_run[...] = jnp.zeros_like(hf_run)

        for sub in range(per):
            chunk_states(step * per + sub, slice(sub * Q, (sub + 1) * Q))

    @pl.when(ph == 1)
    def _phase1():
        @pl.when(step == 0)
        def _():
            hb_run[...] = jnp.zeros_like(hb_run)

        back_step = nsteps - 1 - step
        for sub in reversed(range(per)):
            chunk_outputs(back_step * per + sub, slice(sub * Q, (sub + 1) * Q))


def _ssd(dmat, xc, rows, rowsl, cols, dtT, batch, seq):
    Q = SSD_CHUNK
    nc = seq // Q
    rb = min(SSD_CHUNKS_PER_STEP, nc) * Q
    ns = seq // rb
    last = ns - 1

    def both_map(b, ph, s):
        return (b * ns + s * (1 - ph) + (last - s) * ph, 0)

    def fwd_map(b, ph, s):
        return (b * ns + s * (1 - ph) + last * ph, 0)

    def back_map(b, ph, s):
        return (b * ns + last - s * ph, 0)

    return pl.pallas_call(
        _ssd_kernel,
        grid=(batch, 2, ns),
        in_specs=[
            pl.BlockSpec((rb, SSD_XBC_W), both_map),
            pl.BlockSpec((rb, LANES), both_map),
            pl.BlockSpec((rb, LANES), back_map),
            pl.BlockSpec((rb, LANES), back_map),
            pl.BlockSpec((rb, LANES), fwd_map),
            pl.BlockSpec((SSD_HEADS, Q, Q), lambda b, ph, s: (0, 0, 0)),
        ],
        out_specs=pl.BlockSpec((rb, SSD_WIDTH), back_map),
        out_shape=jax.ShapeDtypeStruct((batch * seq, SSD_WIDTH), bf16),
        scratch_shapes=[
            pltpu.VMEM((nc, SSD_STATE, SSD_WIDTH), bf16),
            pltpu.VMEM((nc, SSD_STATE, SSD_WIDTH), bf16),
            pltpu.VMEM((SSD_STATE, SSD_WIDTH), f32),
            pltpu.VMEM((SSD_STATE, SSD_WIDTH), f32),
        ],
        compiler_params=pltpu.CompilerParams(
            dimension_semantics=("arbitrary", "arbitrary", "arbitrary"),
            vmem_limit_bytes=VMEM_LIMIT),
        name="ssd",
    )(xc, rows, rowsl, cols, dtT, dmat)


def _mem_kv_kernel(mem_ref, nw_ref, w_ref, knw_ref, k_ref, v_ref):
    h = _rms_rows(mem_ref[...], nw_ref[...]).astype(bf16)
    kv = _dot(h, w_ref[...])
    for c in range(XATT_HEADS):
        sl = slice(c * XATT_HEAD_DIM, (c + 1) * XATT_HEAD_DIM)
        t = kv[:, sl]
        k_ref[:, sl] = (t * lax.rsqrt(jnp.mean(t * t, axis=-1, keepdims=True) + EPS)
                        * knw_ref[...]).astype(bf16)
    v_ref[...] = kv[:, XATT_W:].astype(bf16)


def _xattn_kernel(q_ref, mem_ref, nw_ref, w_ref, knw_ref, o_ref, k_ref, v_ref):
    @pl.when(pl.program_id(1) == 0)
    def _():
        _mem_kv_kernel(mem_ref, nw_ref, w_ref, knw_ref, k_ref, v_ref)

    for hd in range(XATT_HEADS):
        sl = slice(hd * XATT_HEAD_DIM, (hd + 1) * XATT_HEAD_DIM)
        s = _dot_nt(q_ref[:, sl], k_ref[:, sl])
        m = jnp.max(s, axis=-1, keepdims=True)
        p = jnp.exp2(s - m)
        den = jnp.sum(p, axis=-1, keepdims=True)
        o_ref[:, sl] = (_dot(p.astype(bf16), v_ref[:, sl]) / den).astype(bf16)


def _xattn(qx, mem2, mem_norm_w, w_kv, xk_norm_w, batch, seq, mem_len, tq):
    nq = seq // tq
    qmap = lambda b, i: (b * nq + i, 0)
    mmap = lambda b, i: (b, 0)
    const = lambda b, i: (0, 0)
    return pl.pallas_call(
        _xattn_kernel,
        grid=(batch, nq),
        in_specs=[
            pl.BlockSpec((tq, XATT_W), qmap),
            pl.BlockSpec((mem_len, D_MODEL), mmap),
            pl.BlockSpec((1, D_MODEL), const),
            pl.BlockSpec((D_MODEL, 2 * XATT_W), const),
            pl.BlockSpec((1, XATT_HEAD_DIM), const),
        ],
        out_specs=pl.BlockSpec((tq, XATT_W), qmap),
        out_shape=jax.ShapeDtypeStruct((batch * seq, XATT_W), bf16),
        scratch_shapes=[pltpu.VMEM((mem_len, XATT_W), bf16),
                        pltpu.VMEM((mem_len, XATT_W), bf16)],
        compiler_params=pltpu.CompilerParams(
            dimension_semantics=("arbitrary", "arbitrary"), vmem_limit_bytes=VMEM_LIMIT),
        name="xattn",
    )(qx, mem2, mem_norm_w, w_kv, xk_norm_w)


FF_CHUNK = 1024


def _out_mlp_kernel(x_ref, a_ref, s_ref, z_ref, c_ref, wo_ref, snw_ref, nw_ref, wu_ref, wd_ref,
                    o_ref):
    y = s_ref[...].astype(f32) * z_ref[...].astype(f32)
    gw = SSD_WIDTH // SSD_GROUPS
    mix = (_dot(a_ref[...], wo_ref[0:ATT_Q_W, :])
           + _dot(c_ref[...], wo_ref[ATT_Q_W + SSD_WIDTH:, :]))
    for g in range(SSD_GROUPS):
        yg = y[:, g * gw:(g + 1) * gw]
        ms = jnp.mean(yg * yg, axis=-1, keepdims=True)
        yn = (yg * lax.rsqrt(ms + EPS) * snw_ref[:, g * gw:(g + 1) * gw]).astype(bf16)
        mix = mix + _dot(yn, wo_ref[ATT_Q_W + g * gw:ATT_Q_W + (g + 1) * gw, :])
    x1 = x_ref[...] + mix
    h = _rms_rows(x1, nw_ref[...]).astype(bf16)
    acc = x1
    for j in range(D_FF // FF_CHUNK):
        u = _dot(h, wu_ref[:, j * FF_CHUNK:(j + 1) * FF_CHUNK])
        r = jnp.maximum(u, 0.0)
        acc = acc + _dot((r * r).astype(bf16), wd_ref[j * FF_CHUNK:(j + 1) * FF_CHUNK, :])
    o_ref[...] = acc


def _out_mlp(x2, attn, ssd, z, xatt, w_out, ssd_norm_w, norm_w, w_up, w_down, tm):
    tokens = x2.shape[0]
    n_tiles = tokens // tm
    row = lambda i: (i, 0)
    const = lambda i: (0, 0)
    resident = functools.partial(pl.BlockSpec, index_map=const, pipeline_mode=pl.Buffered(1))
    return pl.pallas_call(
        _out_mlp_kernel,
        grid=(n_tiles,),
        in_specs=[
            pl.BlockSpec((tm, D_MODEL), row),
            pl.BlockSpec((tm, ATT_Q_W), row),
            pl.BlockSpec((tm, SSD_WIDTH), row),
            pl.BlockSpec((tm, SSD_WIDTH), row),
            pl.BlockSpec((tm, XATT_W), row),
            resident((D_MIX, D_MODEL)),
            pl.BlockSpec((1, SSD_WIDTH), const),
            pl.BlockSpec((1, D_MODEL), const),
            resident((D_MODEL, D_FF)),
            resident((D_FF, D_MODEL)),
        ],
        out_specs=pl.BlockSpec((tm, D_MODEL), row),
        out_shape=jax.ShapeDtypeStruct((tokens, D_MODEL), f32),
        compiler_params=pltpu.CompilerParams(
            dimension_semantics=("arbitrary",), vmem_limit_bytes=VMEM_LIMIT),
        name="out_mlp",
    )(x2, attn, ssd, z, xatt, w_out, ssd_norm_w, norm_w, w_up, w_down)


def _rope_tables(seq):
    half = ROPE_DIM // 2
    inv = ROPE_THETA ** (-jnp.arange(0, ROPE_DIM, 2, dtype=f32) / ROPE_DIM)
    ang = jnp.arange(seq, dtype=f32)[:, None] * inv[None, :]
    cos = jnp.cos(ang)
    sin = jnp.sin(ang)
    ones = jnp.ones((seq, ATT_HEAD_DIM - ROPE_DIM), f32)
    zeros_h = jnp.zeros((seq, half), f32)
    zeros_p = jnp.zeros((seq, ATT_HEAD_DIM - ROPE_DIM), f32)
    cos_head = jnp.concatenate([cos, cos, ones], axis=1)
    lo_head = jnp.concatenate([-sin, zeros_h, zeros_p], axis=1)
    hi_head = jnp.concatenate([zeros_h, sin, zeros_p], axis=1)
    rep = LANES // ATT_HEAD_DIM
    return (jnp.tile(cos_head, (1, rep)), jnp.tile(lo_head, (1, rep)),
            jnp.tile(hi_head, (1, rep)))


def _layer(x, mem, norm_mix_w, w_in, q_norm_w, k_norm_w, attn_sink, conv_w, conv_b,
           dt_bias_f, dt_bias_b, a_log_f, a_log_b, ssd_d, ssd_norm_w, mem_norm_w,
           w_mem_kv, xq_norm_w, xk_norm_w, w_out, norm_mlp_w, w_mlp_up, w_mlp_down, tables):
    batch, seq, _ = x.shape
    mem_len = mem.shape[1]
    tokens = batch * seq
    tm = 512

    s = np.cumsum([ATT_Q_W, ATT_KV_W, ATT_KV_W, SSD_WIDTH, SSD_XBC_W, SSD_DT_W, XATT_W])
    piece = lambda lo, hi: w_in[:, lo:hi].astype(bf16)
    w_q, w_k, w_v = piece(0, s[0]), piece(s[0], s[1]), piece(s[1], s[2])
    w_z, w_xbc, w_dt, w_qx = (piece(s[2], s[3]), piece(s[3], s[4]),
                              piece(s[4], s[5]), piece(s[5], s[6]))
    hd = ATT_HEAD_DIM
    dup = lambda w: jnp.concatenate(
        [w[:, g * hd:(g + 1) * hd] for g in range(ATT_KV_HEADS) for _ in range(2)], axis=1)
    w_vdt = jnp.concatenate(
        [w_v, w_dt, jnp.zeros((D_MODEL, DT_PAD_W - SSD_DT_W), bf16)], axis=1)
    weights = (w_q, dup(w_k), w_vdt, w_z, w_xbc, w_qx)

    pad = jnp.zeros((DT_PAD_W - SSD_DT_W,), f32)
    dt_bias_row = jnp.concatenate([dt_bias_f, dt_bias_b, pad]).reshape(1, DT_PAD_W)
    a_row = jnp.concatenate([-jnp.exp(a_log_f), -jnp.exp(a_log_b), pad]).reshape(1, DT_PAD_W)
    conv_w8 = jnp.concatenate([conv_w, jnp.zeros((8 - SSD_CONV, SSD_XBC_W), f32)], axis=0)
    dmat = (ssd_d.astype(f32)[:, None, None] * jnp.eye(SSD_CHUNK, dtype=f32)).astype(bf16)

    x2 = x.reshape(tokens, D_MODEL)
    cos_t, sinlo_t, sinhi_t = tables
    q, kdup, vt, z, xc, qx, rows, rowsl, cols, dtT = _in_proj(
        x2, norm_mix_w.reshape(1, D_MODEL), weights,
        jnp.tile(q_norm_w, ATT_HEADS).reshape(1, ATT_Q_W),
        jnp.tile(k_norm_w, KDUP_W // hd).reshape(1, KDUP_W),
        xq_norm_w.reshape(1, XATT_HEAD_DIM), cos_t, sinlo_t, sinhi_t,
        conv_w8, conv_b.reshape(1, SSD_XBC_W), dt_bias_row, a_row, seq, tm)

    attn = _attention(attn_sink.astype(f32), q, kdup, vt, batch, seq)
    ssd = _ssd(dmat, xc, rows, rowsl, cols, dtT, batch, seq)

    xatt = _xattn(qx, mem.reshape(batch * mem_len, D_MODEL), mem_norm_w.reshape(1, D_MODEL),
                  w_mem_kv.astype(bf16), xk_norm_w.reshape(1, XATT_HEAD_DIM),
                  batch, seq, mem_len, min(XATT_TQ, seq))

    out = _out_mlp(x2, attn, ssd, z, xatt, w_out.astype(bf16),
                   ssd_norm_w.reshape(1, SSD_WIDTH), norm_mlp_w.reshape(1, D_MODEL),
                   w_mlp_up.astype(bf16), w_mlp_down.astype(bf16), tm)
    return out.reshape(batch, seq, D_MODEL)


def kernel(x, mem, norm_mix_w, w_in, q_norm_w, k_norm_w, attn_sink, conv_w, conv_b,
           dt_bias_f, dt_bias_b, a_log_f, a_log_b, ssd_d, ssd_norm_w, mem_norm_w,
           w_mem_kv, xq_norm_w, xk_norm_w, w_out, norm_mlp_w, w_mlp_up, w_mlp_down):
    depth = w_in.shape[0]
    tables = _rope_tables(x.shape[1])
    for i in range(depth):
        x = _layer(x, mem, norm_mix_w[i], w_in[i], q_norm_w[i], k_norm_w[i], attn_sink[i],
                   conv_w[i], conv_b[i], dt_bias_f[i], dt_bias_b[i], a_log_f[i], a_log_b[i],
                   ssd_d[i], ssd_norm_w[i], mem_norm_w[i], w_mem_kv[i], xq_norm_w[i],
                   xk_norm_w[i], w_out[i], norm_mlp_w[i], w_mlp_up[i], w_mlp_down[i], tables)
    return x
```

```python
import functools

import numpy as np
import jax
import jax.numpy as jnp
from jax import lax
from jax.experimental import pallas as pl
from jax.experimental.pallas import tpu as pltpu

D_MODEL = 1024
EPS = 1e-6

ATT_HEADS = 8
ATT_KV_HEADS = 2
ATT_HEAD_DIM = 64
ATT_Q_W = ATT_HEADS * ATT_HEAD_DIM
ATT_KV_W = ATT_KV_HEADS * ATT_HEAD_DIM
WINDOW = 128
ATT_BLOCK = 128
ATT_BLOCKS_PER_STEP = 16
ATT_HEADS_PER_CHAIN = 4
ROPE_THETA = 500000.0
ROPE_DIM = ATT_HEAD_DIM // 4

SSD_HEADS = 16
SSD_HEAD_DIM = 64
SSD_WIDTH = SSD_HEADS * SSD_HEAD_DIM
SSD_GROUPS = 2
SSD_STATE = 128
SSD_CONV = 5
SSD_CHUNK = 128
SSD_CHUNKS_PER_STEP = 16
SSD_XBC_W = SSD_WIDTH + 2 * SSD_GROUPS * SSD_STATE
SSD_DT_W = 2 * SSD_HEADS

XATT_HEADS = 4
XATT_HEAD_DIM = 128
XATT_W = XATT_HEADS * XATT_HEAD_DIM
XATT_TQ = 1024

D_FF = 4 * D_MODEL
D_MIX = 2 * D_MODEL

LANES = 128
MXU_WIDTH = 256
BF16_SUBLANES = 16
F32_SUBLANES = 8
VMEM_LIMIT = 56 * 1024 * 1024

KDUP_W = 2 * ATT_KV_W
DT_PAD_W = LANES
CONV_HALO = BF16_SUBLANES
C_B = SSD_WIDTH
C_C = SSD_WIDTH + SSD_GROUPS * SSD_STATE

NEG_BIG = -1e30
LOG2E = 1.4426950408889634

bf16 = jnp.bfloat16
f32 = jnp.float32


def _dot(a, b):
    return jnp.dot(a, b, preferred_element_type=f32)


def _dot_nt(a, b):
    return lax.dot_general(a, b, (((1,), (1,)), ((), ())), preferred_element_type=f32)


def _split3(x):
    x1 = x.astype(bf16)
    r1 = x - x1.astype(f32)
    x2 = r1.astype(bf16)
    r2 = r1 - x2.astype(f32)
    return x1, x2, r2.astype(bf16)


def _dot3(x, m01):
    x1, x2, x3 = _split3(x)
    return _dot(x1, m01) + _dot(x2, m01) + _dot(x3, m01)


def _iota(shape, dim):
    return lax.broadcasted_iota(jnp.int32, shape, dim)


def _segment_ones(width, seg):
    r = _iota((width, width), 0) // seg
    c = _iota((width, width), 1) // seg
    return jnp.where(r == c, 1.0, 0.0).astype(bf16)


def _head_mean_sq(t, seg):
    width = min(t.shape[1], MXU_WIDTH)
    ones = _segment_ones(width, seg)
    sq = (t * t).astype(bf16)
    sums = [_dot(sq[:, c0:c0 + width], ones) for c0 in range(0, t.shape[1], width)]
    return jnp.concatenate(sums, axis=1) * (1.0 / seg)


def _rope(t, cos, sin_lo, sin_hi):
    half = ROPE_DIM // 2
    return (t * cos + pltpu.roll(t, LANES - half, 1) * sin_lo
            + pltpu.roll(t, half, 1) * sin_hi)


def _scheduled_after(x, anchor):
    folded = anchor[:, 0:LANES]
    for c0 in range(LANES, anchor.shape[1], LANES):
        folded = folded + anchor[:, c0:c0 + LANES]
    bits = pltpu.bitcast(jnp.broadcast_to(folded, (F32_SUBLANES, LANES)), jnp.uint32)
    sixteen = jnp.full(bits.shape, 16, jnp.uint32)
    zero = lax.shift_right_logical(lax.shift_right_logical(bits, sixteen), sixteen)
    zero = jnp.concatenate([zero.astype(f32)] * 2, axis=0).astype(bf16)
    head = x[0:BF16_SUBLANES, :] + jnp.concatenate([zero] * (x.shape[1] // LANES), axis=1)
    return jnp.concatenate([head, x[BF16_SUBLANES:, :]], axis=0)


def _rms_rows(x, w):
    ms = jnp.mean(x * x, axis=-1, keepdims=True)
    return (x * lax.rsqrt(ms + EPS)) * w


def _in_proj_body(pos_blocks, x_ref, xp_ref, xn_ref, nw_ref,
                  wq_ref, wk_ref, wvdt_ref, wz_ref, wxbc_ref, wqx_ref, qnw_ref, knw_ref,
                  xqnw_ref, cos_ref, sinlo_ref, sinhi_ref, cw_ref, cb_ref, dtb_ref, arow_ref,
                  q_ref, k_ref, v_ref, z_ref, xc_ref, qx_ref,
                  rows_ref, rowsl_ref, cols_ref, dtT_ref, conv_s):
    i = pl.program_id(0)
    tm = x_ref.shape[0]
    si = i % pos_blocks
    nw = nw_ref[...]
    h = _rms_rows(x_ref[...], nw).astype(bf16)

    cos = cos_ref[...]
    sin_lo = sinlo_ref[...]
    sin_hi = sinhi_ref[...]

    pq = _dot(h, wq_ref[...])
    qn = pq * lax.rsqrt(_head_mean_sq(pq, ATT_HEAD_DIM) + EPS) * qnw_ref[...]
    for c in range(ATT_Q_W // LANES):
        sl = slice(c * LANES, (c + 1) * LANES)
        q_ref[:, sl] = (_rope(qn[:, sl], cos, sin_lo, sin_hi)
                        * (ATT_HEAD_DIM ** -0.5 * LOG2E)).astype(bf16)

    pk = _dot(h, wk_ref[...])
    kn = pk * lax.rsqrt(_head_mean_sq(pk, ATT_HEAD_DIM) + EPS) * knw_ref[...]
    for c in range(KDUP_W // LANES):
        sl = slice(c * LANES, (c + 1) * LANES)
        k_ref[:, sl] = _rope(kn[:, sl], cos, sin_lo, sin_hi).astype(bf16)

    pvdt = _dot(h, wvdt_ref[...])
    v_ref[...] = pvdt[:, 0:ATT_KV_W].T.astype(bf16)
    pz = _dot(h, wz_ref[...])
    z_ref[...] = (pz / (1.0 + jnp.exp2(pz * (-LOG2E)))).astype(bf16)

    pqx = _dot(h, wqx_ref[...])
    for c in range(XATT_HEADS):
        sl = slice(c * XATT_HEAD_DIM, (c + 1) * XATT_HEAD_DIM)
        t = pqx[:, sl]
        tn = t * lax.rsqrt(jnp.mean(t * t, axis=-1, keepdims=True) + EPS) * xqnw_ref[...]
        qx_ref[:, sl] = (tn * (XATT_HEAD_DIM ** -0.5 * LOG2E)).astype(bf16)

    hp = _rms_rows(xp_ref[...], nw)
    hn = _rms_rows(xn_ref[...], nw)
    hp = jnp.where(si > 0, hp, jnp.zeros_like(hp)).astype(bf16)
    hn = jnp.where(si < pos_blocks - 1, hn, jnp.zeros_like(hn)).astype(bf16)
    h_ext = jnp.concatenate([hp, h, hn], axis=0)
    conv_s[...] = _dot(h_ext, wxbc_ref[...])
    sub = F32_SUBLANES
    nt = tm // sub
    t0 = CONV_HALO // sub
    u3 = conv_s[...].reshape((tm + 2 * CONV_HALO) // sub, sub, SSD_XBC_W)
    row_in_tile = _iota((1, sub, SSD_XBC_W), 1)
    acc = jnp.broadcast_to(cb_ref[...].reshape(1, 1, SSD_XBC_W), (nt, sub, SSD_XBC_W))
    for k in range(SSD_CONV):
        d = k - SSD_CONV // 2
        wk = cw_ref[k:k + 1, :].reshape(1, 1, SSD_XBC_W)
        if d == 0:
            shifted = u3[t0:t0 + nt]
        elif d > 0:
            rot = pltpu.roll(u3, sub - d, 1)
            shifted = jnp.where(row_in_tile < sub - d, rot[t0:t0 + nt], rot[t0 + 1:t0 + nt + 1])
        else:
            rot = pltpu.roll(u3, -d, 1)
            shifted = jnp.where(row_in_tile >= -d, rot[t0:t0 + nt], rot[t0 - 1:t0 + nt - 1])
        acc = acc + shifted * wk
    xc = acc / (1.0 + jnp.exp2(acc * (-LOG2E)))
    xc_ref[...] = xc.reshape(tm, SSD_XBC_W).astype(bf16)

    Q = SSD_CHUNK
    dt = pvdt[:, ATT_KV_W:ATT_KV_W + DT_PAD_W] + dtb_ref[...]
    dt = jnp.maximum(dt, 0.0) + jnp.log1p(jnp.exp(-jnp.abs(dt)))
    a = dt * arow_ref[...]
    li = _iota((Q, Q), 0)
    lj = _iota((Q, Q), 1)
    incl_le = jnp.where(li <= lj, 1.0, 0.0).astype(bf16)
    incl_ge = jnp.where(li >= lj, 1.0, 0.0).astype(bf16)
    pad_rows = jnp.zeros((Q - 2 * SSD_HEADS, Q), f32)
    for ch in range(tm // Q):
        r = slice(ch * Q, (ch + 1) * Q)
        aT = a[r].T
        dtT = dt[r].T
        pre = _dot3(aT[0:SSD_HEADS], incl_le)
        suf = _dot3(aT[SSD_HEADS:2 * SSD_HEADS], incl_ge)
        rows = jnp.concatenate([pre, suf, pad_rows], axis=0) * LOG2E
        rows_ref[r, :] = rows
        rowsl_ref[r, :] = rows - jnp.log(dtT) * LOG2E
        cols_ref[r, :] = rows.T
        dtT_ref[r, :] = dtT


def _in_proj(x2, norm_w, weights, qnw, knw, xqnw, cos_t, sinlo_t, sinhi_t,
             conv_w8, conv_b, dt_bias_row, a_row, seq, tm):
    tokens = x2.shape[0]
    pos_blocks = seq // tm
    hpt = tm // CONV_HALO
    n_halo = tokens // CONV_HALO
    row = lambda i: (i, 0)
    const = lambda i: (0, 0)
    pos = lambda i: (i % pos_blocks, 0)
    prev = lambda i: (jnp.maximum(i * hpt - 1, 0), 0)
    nxt = lambda i: (jnp.minimum((i + 1) * hpt, n_halo - 1), 0)
    out_widths = [ATT_Q_W, KDUP_W, KDUP_W, SSD_WIDTH, SSD_XBC_W, XATT_W,
                  LANES, LANES, LANES, LANES]
    out_dtypes = [bf16, bf16, bf16, bf16, bf16, bf16, f32, f32, f32, f32]
    out_specs = [pl.BlockSpec((tm, w), row) for w in out_widths]
    out_shape = [jax.ShapeDtypeStruct((tokens, w), d) for w, d in zip(out_widths, out_dtypes)]
    out_specs[2] = pl.BlockSpec((ATT_KV_W, tm), lambda i: (i // pos_blocks, i % pos_blocks))
    out_shape[2] = jax.ShapeDtypeStruct((tokens // seq * ATT_KV_W, seq), bf16)
    return pl.pallas_call(
        functools.partial(_in_proj_body, pos_blocks),
        grid=(tokens // tm,),
        in_specs=[
            pl.BlockSpec((tm, D_MODEL), row),
            pl.BlockSpec((CONV_HALO, D_MODEL), prev),
            pl.BlockSpec((CONV_HALO, D_MODEL), nxt),
            pl.BlockSpec((1, D_MODEL), const),
            *[pl.BlockSpec(w.shape, const, pipeline_mode=pl.Buffered(1)) for w in weights],
            pl.BlockSpec((1, ATT_Q_W), const),
            pl.BlockSpec((1, KDUP_W), const),
            pl.BlockSpec((1, XATT_HEAD_DIM), const),
            pl.BlockSpec((tm, LANES), pos),
            pl.BlockSpec((tm, LANES), pos),
            pl.BlockSpec((tm, LANES), pos),
            pl.BlockSpec((8, SSD_XBC_W), const),
            pl.BlockSpec((1, SSD_XBC_W), const),
            pl.BlockSpec((1, DT_PAD_W), const),
            pl.BlockSpec((1, DT_PAD_W), const),
        ],
        out_specs=out_specs,
        out_shape=out_shape,
        scratch_shapes=[pltpu.VMEM((tm + 2 * CONV_HALO, SSD_XBC_W), f32)],
        compiler_params=pltpu.CompilerParams(
            dimension_semantics=("arbitrary",), vmem_limit_bytes=VMEM_LIMIT),
        name="in_proj",
    )(x2, x2, x2, norm_w, *weights, qnw, knw, xqnw, cos_t, sinlo_t, sinhi_t,
      conv_w8, conv_b, dt_bias_row, a_row)


def _attn_kernel(sink_ref, mfirst_ref, mmid_ref, mlast_ref, q_ref, kp_ref, kc_ref, kn_ref,
                 vp_ref, vc_ref, vn_ref, o_ref):
    blk = ATT_BLOCK
    per = q_ref.shape[0] // blk
    heads_per_kv = ATT_HEADS // ATT_KV_HEADS
    hpc = ATT_HEADS_PER_CHAIN
    cols = hpc * blk
    low_lanes = _iota((blk, LANES), 1) < ATT_HEAD_DIM
    head_of_col = _iota((1, cols), 1) // blk
    masks = [mfirst_ref] + [mmid_ref] * (per - 2) + [mlast_ref]

    kblocks, vblocks = [], []
    for g in range(ATT_KV_HEADS):
        gs = slice(g * LANES, (g + 1) * LANES)
        vs = slice(g * ATT_HEAD_DIM, (g + 1) * ATT_HEAD_DIM)
        kblocks.append([kp_ref[:, gs]] + [kc_ref[i * blk:(i + 1) * blk, gs] for i in range(per)]
                       + [kn_ref[:, gs]])
        vblocks.append([vp_ref[vs, :]] + [vc_ref[vs, i * blk:(i + 1) * blk] for i in range(per)]
                       + [vn_ref[vs, :]])
    ones_rows = jnp.ones((LANES - ATT_HEAD_DIM, 3 * blk), bf16)

    def scores(g, t, h0):
        qrows = slice(t * blk, (t + 1) * blk)
        kcat = jnp.concatenate(kblocks[g][t:t + 3], axis=0)
        qs = []
        sink = jnp.zeros((1, cols), f32)
        for j in range(hpc):
            hd = h0 + j
            pair = q_ref[qrows, (hd // 2) * LANES:(hd // 2 + 1) * LANES]
            keep = low_lanes if hd % 2 == 0 else jnp.logical_not(low_lanes)
            qs.append(jnp.where(keep, pair, jnp.zeros_like(pair)))
            sink = jnp.where(head_of_col == j, sink_ref[hd] * LOG2E, sink)
        qg = jnp.concatenate(qs, axis=0)
        raw = _dot_nt(kcat, qg)
        st = jnp.concatenate(
            [raw[0:blk] + masks[t][0:blk, 0:cols], raw[blk:2 * blk],
             raw[2 * blk:] + masks[t][2 * blk:, 0:cols]], axis=0)
        m = jnp.maximum(jnp.max(st, axis=0, keepdims=True), sink)
        return st, m, sink, raw[3 * blk - 1:3 * blk, :]

    def finish(g, t, h0, st, m, sink, tail_next):
        qrows = slice(t * blk, (t + 1) * blk)
        vt = jnp.concatenate(
            [jnp.concatenate(vblocks[g][t:t + 3], axis=1), ones_rows], axis=0)
        if tail_next is not None:
            vt = _scheduled_after(vt, tail_next)
        p = jnp.exp2(st - m).astype(bf16)
        ov = _dot(vt, p)
        den = ov[ATT_HEAD_DIM:ATT_HEAD_DIM + 1] + jnp.exp2(sink - m)
        ot = ov[0:ATT_HEAD_DIM] / den
        for jp in range(hpc // 2):
            even = ot[:, (2 * jp) * blk:(2 * jp + 1) * blk]
            odd = ot[:, (2 * jp + 1) * blk:(2 * jp + 2) * blk]
            pr = h0 // 2 + jp
            o_ref[qrows, pr * LANES:(pr + 1) * LANES] = (
                jnp.concatenate([even, odd], axis=0).T.astype(bf16))

    chains = [(g, t, g * heads_per_kv + c * hpc) for g in range(ATT_KV_HEADS)
              for t in range(per) for c in range(heads_per_kv // hpc)]
    nxt = scores(*chains[0])
    for idx, ch in enumerate(chains):
        st, m, sink, _ = nxt
        nxt = scores(*chains[idx + 1]) if idx + 1 < len(chains) else None
        finish(*ch, st, m, sink, None if nxt is None else nxt[3])


def _attn_masks():
    blk = ATT_BLOCK
    cols = (ATT_HEADS // ATT_KV_HEADS) * blk
    key = np.arange(3 * blk)[:, None]
    qi = np.arange(cols)[None, :] % blk
    band = (key >= qi) & (key <= qi + 2 * WINDOW)
    variants = [band & (key >= blk), band, band & (key < 2 * blk)]
    return jnp.asarray(np.where(np.stack(variants), 0.0, NEG_BIG), dtype=f32)


def _attention(sink, q, kdup, vt, batch, seq):
    blk = ATT_BLOCK
    nb = seq // blk
    per = min(ATT_BLOCKS_PER_STEP, nb)
    ns = nb // per
    cols = (ATT_HEADS // ATT_KV_HEADS) * blk
    masks = _attn_masks()
    cur = lambda b, s: (b * ns + s, 0)
    prev = lambda b, s: (b * nb + jnp.maximum(per * s - 1, 0), 0)
    nxt = lambda b, s: (b * nb + jnp.minimum(per * s + per, nb - 1), 0)
    vcur = lambda b, s: (b, s)
    vprev = lambda b, s: (b, jnp.maximum(per * s - 1, 0))
    vnxt = lambda b, s: (b, jnp.minimum(per * s + per, nb - 1))
    mfirst = lambda b, s: (jnp.where(s == 0, 0, 1), 0, 0)
    mmid = lambda b, s: (1, 0, 0)
    mlast = lambda b, s: (jnp.where(s == ns - 1, 2, 1), 0, 0)
    return pl.pallas_call(
        _attn_kernel,
        grid=(batch, ns),
        in_specs=[
            pl.BlockSpec(memory_space=pltpu.SMEM),
            pl.BlockSpec((None, 3 * blk, cols), mfirst),
            pl.BlockSpec((None, 3 * blk, cols), mmid),
            pl.BlockSpec((None, 3 * blk, cols), mlast),
            pl.BlockSpec((per * blk, ATT_Q_W), cur),
            pl.BlockSpec((blk, KDUP_W), prev),
            pl.BlockSpec((per * blk, KDUP_W), cur),
            pl.BlockSpec((blk, KDUP_W), nxt),
            pl.BlockSpec((ATT_KV_W, blk), vprev),
            pl.BlockSpec((ATT_KV_W, per * blk), vcur),
            pl.BlockSpec((ATT_KV_W, blk), vnxt),
        ],
        out_specs=pl.BlockSpec((per * blk, ATT_Q_W), cur),
        out_shape=jax.ShapeDtypeStruct((batch * seq, ATT_Q_W), bf16),
        compiler_params=pltpu.CompilerParams(
            dimension_semantics=("arbitrary", "arbitrary"), vmem_limit_bytes=VMEM_LIMIT),
        name="attention",
    )(sink, masks, masks, masks, q, kdup, kdup, kdup, vt, vt, vt)


def _ssd_kernel(xc_ref, rows_ref, rowsl_ref, cols_ref, dtT_ref, dmat_ref, y_ref,
                hf_s, sb_s, hf_run, hb_run):
    ph = pl.program_id(1)
    step = pl.program_id(2)
    nsteps = pl.num_programs(2)
    Q = SSD_CHUNK
    per = xc_ref.shape[0] // Q
    heads_per_group = SSD_HEADS // SSD_GROUPS
    low_lanes = _iota((Q, LANES), 1) < SSD_HEAD_DIM
    low_row = _iota((1, LANES), 1) < SSD_HEAD_DIM
    li = _iota((Q, Q), 0)
    si = _iota((Q, Q), 1)

    def chunk_states(c, rs):
        rows = rows_ref[rs, :]
        dtT = dtT_ref[rs, :]
        pre = rows[0:SSD_HEADS]
        suf = rows[SSD_HEADS:2 * SSD_HEADS]
        pre_end = jnp.broadcast_to(pre[:, Q - 1:Q], (SSD_HEADS, Q))
        suf_end = jnp.broadcast_to(suf[:, 0:1], (SSD_HEADS, Q))
        wf = jnp.exp2(pre_end - pre) * dtT[0:SSD_HEADS]
        wb = jnp.exp2(suf_end - suf) * dtT[SSD_HEADS:2 * SSD_HEADS]
        decf = jnp.exp2(pre_end)

        for g in range(SSD_GROUPS):
            bT = xc_ref[rs, C_B + g * SSD_STATE:C_B + (g + 1) * SSD_STATE].astype(f32).T
            for m in range(heads_per_group // 2):
                h0 = g * heads_per_group + 2 * m
                h1 = h0 + 1
                lhs = jnp.concatenate(
                    [bT * wf[h0:h0 + 1], bT * wb[h0:h0 + 1],
                     bT * wf[h1:h1 + 1], bT * wb[h1:h1 + 1]], axis=0).astype(bf16)
                cs = slice(h0 * SSD_HEAD_DIM, h0 * SSD_HEAD_DIM + LANES)
                res = _dot(lhs, xc_ref[rs, cs])
                s_f = jnp.where(low_lanes, res[0:Q], res[2 * Q:3 * Q])
                s_b = jnp.where(low_lanes, res[Q:2 * Q], res[3 * Q:4 * Q])
                dec = jnp.where(low_row, decf[h0:h0 + 1], decf[h1:h1 + 1])
                prev = hf_run[:, cs]
                hf_s[c, :, cs] = prev.astype(bf16)
                hf_run[:, cs] = dec * prev + s_f
                sb_s[c, :, cs] = s_b.astype(bf16)

    def chunk_outputs(c, rs):
        rows = rows_ref[rs, :]
        suf = rows[SSD_HEADS:2 * SSD_HEADS]
        cols = cols_ref[rs, :]
        rowsl = rowsl_ref[rs, :]
        decb = jnp.exp2(jnp.broadcast_to(suf[:, 0:1], (SSD_HEADS, Q)))
        lower = li >= si
        upper = si >= li

        for g in range(SSD_GROUPS):
            bg = xc_ref[rs, C_B + g * SSD_STATE:C_B + (g + 1) * SSD_STATE]
            cg = xc_ref[rs, C_C + g * SSD_STATE:C_C + (g + 1) * SSD_STATE]
            cbm = _dot_nt(cg, bg).astype(bf16)
            for m in range(heads_per_group // 2):
                h0 = g * heads_per_group + 2 * m
                parts = []
                for hd in (h0, h0 + 1):
                    hb = SSD_HEADS + hd
                    colf = jnp.broadcast_to(cols[:, hd:hd + 1], (Q, Q))
                    colb = jnp.broadcast_to(cols[:, hb:hb + 1], (Q, Q))
                    ef = jnp.exp2(jnp.where(lower, colf - rowsl[hd:hd + 1], NEG_BIG)).astype(bf16)
                    eb = jnp.exp2(jnp.where(upper, colb - rowsl[hb:hb + 1], NEG_BIG)).astype(bf16)
                    gm = cbm * (ef + eb) + dmat_ref[hd]
                    parts.append(jnp.concatenate(
                        [gm, cg * jnp.exp2(colf).astype(bf16), cg * jnp.exp2(colb).astype(bf16)],
                        axis=1))
                lhs = jnp.concatenate(parts, axis=0)
                cs = slice(h0 * SSD_HEAD_DIM, h0 * SSD_HEAD_DIM + LANES)
                hb_prev = hb_run[:, cs]
                rhs = jnp.concatenate(
                    [xc_ref[rs, cs], hf_s[c, :, cs], hb_prev.astype(bf16)], axis=0)
                res = _dot(lhs, rhs)
                y_ref[rs, cs] = jnp.where(low_lanes, res[0:Q], res[Q:2 * Q]).astype(bf16)
                dec = jnp.where(low_row, decb[h0:h0 + 1], decb[h0 + 1:h0 + 2])
                hb_run[:, cs] = dec * hb_prev + sb_s[c, :, cs].astype(f32)

    @pl.when(ph == 0)
    def _phase0():
        @pl.when(step == 0)
        def _():
            hf_run[...] = jnp.zeros_like(hf_run)

        for sub in range(per):
            chunk_states(step * per + sub, slice(sub * Q, (sub + 1) * Q))

    @pl.when(ph == 1)
    def _phase1():
        @pl.when(step == 0)
        def _():
            hb_run[...] = jnp.zeros_like(hb_run)

        back_step = nsteps - 1 - step
        for sub in reversed(range(per)):
            chunk_outputs(back_step * per + sub, slice(sub * Q, (sub + 1) * Q))


def _ssd(dmat, xc, rows, rowsl, cols, dtT, batch, seq):
    Q = SSD_CHUNK
    nc = seq // Q
    rb = min(SSD_CHUNKS_PER_STEP, nc) * Q
    ns = seq // rb
    last = ns - 1

    def both_map(b, ph, s):
        return (b * ns + s * (1 - ph) + (last - s) * ph, 0)

    def fwd_map(b, ph, s):
        return (b * ns + s * (1 - ph) + last * ph, 0)

    def back_map(b, ph, s):
        return (b * ns + last - s * ph, 0)

    return pl.pallas_call(
        _ssd_kernel,
        grid=(batch, 2, ns),
        in_specs=[
            pl.BlockSpec((rb, SSD_XBC_W), both_map),
            pl.BlockSpec((rb, LANES), both_map),
            pl.BlockSpec((rb, LANES), back_map),
            pl.BlockSpec((rb, LANES), back_map),
            pl.BlockSpec((rb, LANES), fwd_map),
            pl.BlockSpec((SSD_HEADS, Q, Q), lambda b, ph, s: (0, 0, 0)),
        ],
        out_specs=pl.BlockSpec((rb, SSD_WIDTH), back_map),
        out_shape=jax.ShapeDtypeStruct((batch * seq, SSD_WIDTH), bf16),
        scratch_shapes=[
            pltpu.VMEM((nc, SSD_STATE, SSD_WIDTH), bf16),
            pltpu.VMEM((nc, SSD_STATE, SSD_WIDTH), bf16),
            pltpu.VMEM((SSD_STATE, SSD_WIDTH), f32),
            pltpu.VMEM((SSD_STATE, SSD_WIDTH), f32),
        ],
        compiler_params=pltpu.CompilerParams(
            dimension_semantics=("arbitrary", "arbitrary", "arbitrary"),
            vmem_limit_bytes=VMEM_LIMIT),
        name="ssd",
    )(xc, rows, rowsl, cols, dtT, dmat)


def _mem_kv(mem_ref, nw_ref, w_ref, knw_ref, k_ref, v_ref):
    h = _rms_rows(mem_ref[...], nw_ref[...]).astype(bf16)
    kv = _dot(h, w_ref[...])
    for c in range(XATT_HEADS):
        sl = slice(c * XATT_HEAD_DIM, (c + 1) * XATT_HEAD_DIM)
        t = kv[:, sl]
        k_ref[:, sl] = (t * lax.rsqrt(jnp.mean(t * t, axis=-1, keepdims=True) + EPS)
                        * knw_ref[...]).astype(bf16)
    v_ref[...] = kv[:, XATT_W:].astype(bf16)


def _xattn_kernel(q_ref, mem_ref, nw_ref, w_ref, knw_ref, o_ref, k_ref, v_ref):
    @pl.when(pl.program_id(1) == 0)
    def _():
        _mem_kv(mem_ref, nw_ref, w_ref, knw_ref, k_ref, v_ref)

    for hd in range(XATT_HEADS):
        sl = slice(hd * XATT_HEAD_DIM, (hd + 1) * XATT_HEAD_DIM)
        s = _dot_nt(q_ref[:, sl], k_ref[:, sl])
        m = jnp.max(s, axis=-1, keepdims=True)
        p = jnp.exp2(s - m)
        den = jnp.sum(p, axis=-1, keepdims=True)
        o_ref[:, sl] = (_dot(p.astype(bf16), v_ref[:, sl]) / den).astype(bf16)


def _xattn(qx, mem2, mem_norm_w, w_kv, xk_norm_w, batch, seq, mem_len, tq):
    nq = seq // tq
    qmap = lambda b, i: (b * nq + i, 0)
    mmap = lambda b, i: (b, 0)
    const = lambda b, i: (0, 0)
    return pl.pallas_call(
        _xattn_kernel,
        grid=(batch, nq),
        in_specs=[
            pl.BlockSpec((tq, XATT_W), qmap),
            pl.BlockSpec((mem_len, D_MODEL), mmap),
            pl.BlockSpec((1, D_MODEL), const),
            pl.BlockSpec((D_MODEL, 2 * XATT_W), const),
            pl.BlockSpec((1, XATT_HEAD_DIM), const),
        ],
        out_specs=pl.BlockSpec((tq, XATT_W), qmap),
        out_shape=jax.ShapeDtypeStruct((batch * seq, XATT_W), bf16),
        scratch_shapes=[pltpu.VMEM((mem_len, XATT_W), bf16),
                        pltpu.VMEM((mem_len, XATT_W), bf16)],
        compiler_params=pltpu.CompilerParams(
            dimension_semantics=("arbitrary", "arbitrary"), vmem_limit_bytes=VMEM_LIMIT),
        name="xattn",
    )(qx, mem2, mem_norm_w, w_kv, xk_norm_w)


FF_CHUNK = 1024


def _out_mlp_kernel(x_ref, a_ref, s_ref, z_ref, c_ref, wo_ref, snw_ref, nw_ref, wu_ref, wd_ref,
                    o_ref):
    y = s_ref[...].astype(f32) * z_ref[...].astype(f32)
    gw = SSD_WIDTH // SSD_GROUPS
    mix = (_dot(a_ref[...], wo_ref[0:ATT_Q_W, :])
           + _dot(c_ref[...], wo_ref[ATT_Q_W + SSD_WIDTH:, :]))
    for g in range(SSD_GROUPS):
        yg = y[:, g * gw:(g + 1) * gw]
        ms = jnp.mean(yg * yg, axis=-1, keepdims=True)
        yn = (yg * lax.rsqrt(ms + EPS) * snw_ref[:, g * gw:(g + 1) * gw]).astype(bf16)
        mix = mix + _dot(yn, wo_ref[ATT_Q_W + g * gw:ATT_Q_W + (g + 1) * gw, :])
    x1 = x_ref[...] + mix
    h = _rms_rows(x1, nw_ref[...]).astype(bf16)
    acc = x1
    for j in range(D_FF // FF_CHUNK):
        u = _dot(h, wu_ref[:, j * FF_CHUNK:(j + 1) * FF_CHUNK])
        r = jnp.maximum(u, 0.0)
        acc = acc + _dot((r * r).astype(bf16), wd_ref[j * FF_CHUNK:(j + 1) * FF_CHUNK, :])
    o_ref[...] = acc


def _out_mlp(x2, attn, ssd, z, xatt, w_out, ssd_norm_w, norm_w, w_up, w_down, tm):
    tokens = x2.shape[0]
    row = lambda i: (i, 0)
    const = lambda i: (0, 0)
    resident = functools.partial(pl.BlockSpec, index_map=const, pipeline_mode=pl.Buffered(1))
    return pl.pallas_call(
        _out_mlp_kernel,
        grid=(tokens // tm,),
        in_specs=[
            pl.BlockSpec((tm, D_MODEL), row),
            pl.BlockSpec((tm, ATT_Q_W), row),
            pl.BlockSpec((tm, SSD_WIDTH), row),
            pl.BlockSpec((tm, SSD_WIDTH), row),
            pl.BlockSpec((tm, XATT_W), row),
            resident((D_MIX, D_MODEL)),
            pl.BlockSpec((1, SSD_WIDTH), const),
            pl.BlockSpec((1, D_MODEL), const),
            resident((D_MODEL, D_FF)),
            resident((D_FF, D_MODEL)),
        ],
        out_specs=pl.BlockSpec((tm, D_MODEL), row),
        out_shape=jax.ShapeDtypeStruct((tokens, D_MODEL), f32),
        compiler_params=pltpu.CompilerParams(
            dimension_semantics=("arbitrary",), vmem_limit_bytes=VMEM_LIMIT),
        name="out_mlp",
    )(x2, attn, ssd, z, xatt, w_out, ssd_norm_w, norm_w, w_up, w_down)


def _rope_tables(seq):
    half = ROPE_DIM // 2
    inv = ROPE_THETA ** (-jnp.arange(0, ROPE_DIM, 2, dtype=f32) / ROPE_DIM)
    ang = jnp.arange(seq, dtype=f32)[:, None] * inv[None, :]
    cos = jnp.cos(ang)
    sin = jnp.sin(ang)
    ones = jnp.ones((seq, ATT_HEAD_DIM - ROPE_DIM), f32)
    zeros_h = jnp.zeros((seq, half), f32)
    zeros_p = jnp.zeros((seq, ATT_HEAD_DIM - ROPE_DIM), f32)
    cos_head = jnp.concatenate([cos, cos, ones], axis=1)
    lo_head = jnp.concatenate([-sin, zeros_h, zeros_p], axis=1)
    hi_head = jnp.concatenate([zeros_h, sin, zeros_p], axis=1)
    rep = LANES // ATT_HEAD_DIM
    return (jnp.tile(cos_head, (1, rep)), jnp.tile(lo_head, (1, rep)),
            jnp.tile(hi_head, (1, rep)))


def _layer(x, mem, norm_mix_w, w_in, q_norm_w, k_norm_w, attn_sink, conv_w, conv_b,
           dt_bias_f, dt_bias_b, a_log_f, a_log_b, ssd_d, ssd_norm_w, mem_norm_w,
           w_mem_kv, xq_norm_w, xk_norm_w, w_out, norm_mlp_w, w_mlp_up, w_mlp_down, tables):
    batch, seq, _ = x.shape
    mem_len = mem.shape[1]
    tokens = batch * seq
    tm = 512

    s = np.cumsum([ATT_Q_W, ATT_KV_W, ATT_KV_W, SSD_WIDTH, SSD_XBC_W, SSD_DT_W, XATT_W])
    piece = lambda lo, hi: w_in[:, lo:hi].astype(bf16)
    w_q, w_k, w_v = piece(0, s[0]), piece(s[0], s[1]), piece(s[1], s[2])
    w_z, w_xbc, w_dt, w_qx = (piece(s[2], s[3]), piece(s[3], s[4]),
                              piece(s[4], s[5]), piece(s[5], s[6]))
    hd = ATT_HEAD_DIM
    dup = lambda w: jnp.concatenate(
        [w[:, g * hd:(g + 1) * hd] for g in range(ATT_KV_HEADS) for _ in range(2)], axis=1)
    w_vdt = jnp.concatenate(
        [w_v, w_dt, jnp.zeros((D_MODEL, DT_PAD_W - SSD_DT_W), bf16)], axis=1)
    weights = (w_q, dup(w_k), w_vdt, w_z, w_xbc, w_qx)

    pad = jnp.zeros((DT_PAD_W - SSD_DT_W,), f32)
    dt_bias_row = jnp.concatenate([dt_bias_f, dt_bias_b, pad]).reshape(1, DT_PAD_W)
    a_row = jnp.concatenate([-jnp.exp(a_log_f), -jnp.exp(a_log_b), pad]).reshape(1, DT_PAD_W)
    conv_w8 = jnp.concatenate([conv_w, jnp.zeros((8 - SSD_CONV, SSD_XBC_W), f32)], axis=0)
    dmat = (ssd_d.astype(f32)[:, None, None] * jnp.eye(SSD_CHUNK, dtype=f32)).astype(bf16)

    x2 = x.reshape(tokens, D_MODEL)
    cos_t, sinlo_t, sinhi_t = tables
    q, kdup, vt, z, xc, qx, rows, rowsl, cols, dtT = _in_proj(
        x2, norm_mix_w.reshape(1, D_MODEL), weights,
        jnp.tile(q_norm_w, ATT_HEADS).reshape(1, ATT_Q_W),
        jnp.tile(k_norm_w, KDUP_W // hd).reshape(1, KDUP_W),
        xq_norm_w.reshape(1, XATT_HEAD_DIM), cos_t, sinlo_t, sinhi_t,
        conv_w8, conv_b.reshape(1, SSD_XBC_W), dt_bias_row, a_row, seq, tm)

    attn = _attention(attn_sink.astype(f32), q, kdup, vt, batch, seq)
    ssd = _ssd(dmat, xc, rows, rowsl, cols, dtT, batch, seq)

    xatt = _xattn(qx, mem.reshape(batch * mem_len, D_MODEL), mem_norm_w.reshape(1, D_MODEL),
                  w_mem_kv.astype(bf16), xk_norm_w.reshape(1, XATT_HEAD_DIM),
                  batch, seq, mem_len, min(XATT_TQ, seq))

    out = _out_mlp(x2, attn, ssd, z, xatt, w_out.astype(bf16),
                   ssd_norm_w.reshape(1, SSD_WIDTH), norm_mlp_w.reshape(1, D_MODEL),
                   w_mlp_up.astype(bf16), w_mlp_down.astype(bf16), tm)
    return out.reshape(batch, seq, D_MODEL)


def kernel(x, mem, norm_mix_w, w_in, q_norm_w, k_norm_w, attn_sink, conv_w, conv_b,
           dt_bias_f, dt_bias_b, a_log_f, a_log_b, ssd_d, ssd_norm_w, mem_norm_w,
           w_mem_kv, xq_norm_w, xk_norm_w, w_out, norm_mlp_w, w_mlp_up, w_mlp_down):
    depth = w_in.shape[0]
    tables = _rope_tables(x.shape[1])
    for i in range(depth):
        x = _layer(x, mem, norm_mix_w[i], w_in[i], q_norm_w[i], k_norm_w[i], attn_sink[i],
                   conv_w[i], conv_b[i], dt_bias_f[i], dt_bias_b[i], a_log_f[i], a_log_b[i],
                   ssd_d[i], ssd_norm_w[i], mem_norm_w[i], w_mem_kv[i], xq_norm_w[i],
                   xk_norm_w[i], w_out[i], norm_mlp_w[i], w_mlp_up[i], w_mlp_down[i], tables)
    return x
```

```python
import functools

import numpy as np
import jax
import jax.numpy as jnp
from jax import lax
from jax.experimental import pallas as pl
from jax.experimental.pallas import tpu as pltpu

D_MODEL = 1024
EPS = 1e-6

ATT_HEADS = 8
ATT_KV_HEADS = 2
ATT_HEAD_DIM = 64
ATT_Q_W = ATT_HEADS * ATT_HEAD_DIM
ATT_KV_W = ATT_KV_HEADS * ATT_HEAD_DIM
WINDOW = 128
ATT_BLOCK = 128
ATT_BLOCKS_PER_STEP = 16
ATT_HEADS_PER_CHAIN = 4
ROPE_THETA = 500000.0
ROPE_DIM = ATT_HEAD_DIM // 4

SSD_HEADS = 16
SSD_HEAD_DIM = 64
SSD_WIDTH = SSD_HEADS * SSD_HEAD_DIM
SSD_GROUPS = 2
SSD_STATE = 128
SSD_CONV = 5
SSD_CHUNK = 128
SSD_CHUNKS_PER_STEP = 16
SSD_XBC_W = SSD_WIDTH + 2 * SSD_GROUPS * SSD_STATE
SSD_DT_W = 2 * SSD_HEADS

XATT_HEADS = 4
XATT_HEAD_DIM = 128
XATT_W = XATT_HEADS * XATT_HEAD_DIM
XATT_TQ = 2048
TOKEN_TILE = 512

D_FF = 4 * D_MODEL
D_MIX = 2 * D_MODEL

LANES = 128
MXU_WIDTH = 256
BF16_SUBLANES = 16
F32_SUBLANES = 8
VMEM_LIMIT = 56 * 1024 * 1024

KDUP_W = 2 * ATT_KV_W
DT_PAD_W = LANES
CONV_HALO = BF16_SUBLANES
C_B = SSD_WIDTH
C_C = SSD_WIDTH + SSD_GROUPS * SSD_STATE

NEG_BIG = -1e30
LOG2E = 1.4426950408889634

bf16 = jnp.bfloat16
f32 = jnp.float32


def _dot(a, b):
    return jnp.dot(a, b, preferred_element_type=f32)


def _dot_nt(a, b):
    return lax.dot_general(a, b, (((1,), (1,)), ((), ())), preferred_element_type=f32)


def _split3(x):
    x1 = x.astype(bf16)
    r1 = x - x1.astype(f32)
    x2 = r1.astype(bf16)
    r2 = r1 - x2.astype(f32)
    return x1, x2, r2.astype(bf16)


def _dot3(x, m01):
    x1, x2, x3 = _split3(x)
    return _dot(x1, m01) + _dot(x2, m01) + _dot(x3, m01)


def _iota(shape, dim):
    return lax.broadcasted_iota(jnp.int32, shape, dim)


def _segment_ones(width, seg):
    r = _iota((width, width), 0) // seg
    c = _iota((width, width), 1) // seg
    return jnp.where(r == c, 1.0, 0.0).astype(bf16)


def _head_mean_sq(t, seg):
    width = min(t.shape[1], MXU_WIDTH)
    ones = _segment_ones(width, seg)
    sq = (t * t).astype(bf16)
    sums = [_dot(sq[:, c0:c0 + width], ones) for c0 in range(0, t.shape[1], width)]
    return jnp.concatenate(sums, axis=1) * (1.0 / seg)


def _rope(t, cos, sin_lo, sin_hi):
    half = ROPE_DIM // 2
    return (t * cos + pltpu.roll(t, LANES - half, 1) * sin_lo
            + pltpu.roll(t, half, 1) * sin_hi)


def _scheduled_after(x, anchor):
    folded = anchor[:, 0:LANES]
    for c0 in range(LANES, anchor.shape[1], LANES):
        folded = folded + anchor[:, c0:c0 + LANES]
    bits = pltpu.bitcast(jnp.broadcast_to(folded, (F32_SUBLANES, LANES)), jnp.uint32)
    sixteen = jnp.full(bits.shape, 16, jnp.uint32)
    zero = lax.shift_right_logical(lax.shift_right_logical(bits, sixteen), sixteen)
    zero = jnp.concatenate([zero.astype(f32)] * 2, axis=0).astype(bf16)
    head = x[0:BF16_SUBLANES, :] + jnp.concatenate([zero] * (x.shape[1] // LANES), axis=1)
    return jnp.concatenate([head, x[BF16_SUBLANES:, :]], axis=0)


def _rms_rows(x, w):
    ms = jnp.mean(x * x, axis=-1, keepdims=True)
    return (x * lax.rsqrt(ms + EPS)) * w


def _in_proj_body(pos_blocks, x_ref, xp_ref, xn_ref, nw_ref,
                  wq_ref, wk_ref, wvdt_ref, wz_ref, wxbc_ref, wqx_ref, qnw_ref, knw_ref,
                  xqnw_ref, cos_ref, sinlo_ref, sinhi_ref, cw_ref, cb_ref, dtb_ref, arow_ref,
                  q_ref, k_ref, v_ref, z_ref, xc_ref, qx_ref,
                  rows_ref, rowsl_ref, cols_ref, dtT_ref, conv_s):
    i = pl.program_id(0)
    tm = x_ref.shape[0]
    si = i % pos_blocks
    nw = nw_ref[...]
    h = _rms_rows(x_ref[...], nw).astype(bf16)

    cos = cos_ref[...]
    sin_lo = sinlo_ref[...]
    sin_hi = sinhi_ref[...]

    pq = _dot(h, wq_ref[...])
    qn = pq * lax.rsqrt(_head_mean_sq(pq, ATT_HEAD_DIM) + EPS) * qnw_ref[...]
    for c in range(ATT_Q_W // LANES):
        sl = slice(c * LANES, (c + 1) * LANES)
        q_ref[:, sl] = (_rope(qn[:, sl], cos, sin_lo, sin_hi)
                        * (ATT_HEAD_DIM ** -0.5 * LOG2E)).astype(bf16)

    pk = _dot(h, wk_ref[...])
    kn = pk * lax.rsqrt(_head_mean_sq(pk, ATT_HEAD_DIM) + EPS) * knw_ref[...]
    for c in range(KDUP_W // LANES):
        sl = slice(c * LANES, (c + 1) * LANES)
        k_ref[:, sl] = _rope(kn[:, sl], cos, sin_lo, sin_hi).astype(bf16)

    pvdt = _dot(h, wvdt_ref[...])
    v_ref[...] = pvdt[:, 0:ATT_KV_W].T.astype(bf16)
    pz = _dot(h, wz_ref[...])
    z_ref[...] = (pz / (1.0 + jnp.exp2(pz * (-LOG2E)))).astype(bf16)

    pqx = _dot(h, wqx_ref[...])
    for c in range(XATT_HEADS):
        sl = slice(c * XATT_HEAD_DIM, (c + 1) * XATT_HEAD_DIM)
        t = pqx[:, sl]
        tn = t * lax.rsqrt(jnp.mean(t * t, axis=-1, keepdims=True) + EPS) * xqnw_ref[...]
        qx_ref[:, sl] = (tn * (XATT_HEAD_DIM ** -0.5 * LOG2E)).astype(bf16)

    hp = _rms_rows(xp_ref[...], nw)
    hn = _rms_rows(xn_ref[...], nw)
    hp = jnp.where(si > 0, hp, jnp.zeros_like(hp)).astype(bf16)
    hn = jnp.where(si < pos_blocks - 1, hn, jnp.zeros_like(hn)).astype(bf16)
    h_ext = jnp.concatenate([hp, h, hn], axis=0)
    conv_s[...] = _dot(h_ext, wxbc_ref[...])
    sub = F32_SUBLANES
    nt = tm // sub
    t0 = CONV_HALO // sub
    u3 = conv_s[...].reshape((tm + 2 * CONV_HALO) // sub, sub, SSD_XBC_W)
    row_in_tile = _iota((1, sub, SSD_XBC_W), 1)
    acc = jnp.broadcast_to(cb_ref[...].reshape(1, 1, SSD_XBC_W), (nt, sub, SSD_XBC_W))
    for k in range(SSD_CONV):
        d = k - SSD_CONV // 2
        wk = cw_ref[k:k + 1, :].reshape(1, 1, SSD_XBC_W)
        if d == 0:
            shifted = u3[t0:t0 + nt]
        elif d > 0:
            rot = pltpu.roll(u3, sub - d, 1)
            shifted = jnp.where(row_in_tile < sub - d, rot[t0:t0 + nt], rot[t0 + 1:t0 + nt + 1])
        else:
            rot = pltpu.roll(u3, -d, 1)
            shifted = jnp.where(row_in_tile >= -d, rot[t0:t0 + nt], rot[t0 - 1:t0 + nt - 1])
        acc = acc + shifted * wk
    xc = acc / (1.0 + jnp.exp2(acc * (-LOG2E)))
    xc_ref[...] = xc.reshape(tm, SSD_XBC_W).astype(bf16)

    Q = SSD_CHUNK
    dt = pvdt[:, ATT_KV_W:ATT_KV_W + DT_PAD_W] + dtb_ref[...]
    dt = jnp.maximum(dt, 0.0) + jnp.log1p(jnp.exp(-jnp.abs(dt)))
    a = dt * arow_ref[...]
    li = _iota((Q, Q), 0)
    lj = _iota((Q, Q), 1)
    incl_le = jnp.where(li <= lj, 1.0, 0.0).astype(bf16)
    incl_ge = jnp.where(li >= lj, 1.0, 0.0).astype(bf16)
    pad_rows = jnp.zeros((Q - 2 * SSD_HEADS, Q), f32)
    for ch in range(tm // Q):
        r = slice(ch * Q, (ch + 1) * Q)
        aT = a[r].T
        dtT = dt[r].T
        pre = _dot3(aT[0:SSD_HEADS], incl_le)
        suf = _dot3(aT[SSD_HEADS:2 * SSD_HEADS], incl_ge)
        rows = jnp.concatenate([pre, suf, pad_rows], axis=0) * LOG2E
        rows_ref[r, :] = rows
        rowsl_ref[r, :] = rows - jnp.log(dtT) * LOG2E
        cols_ref[r, :] = rows.T
        dtT_ref[r, :] = dtT


def _in_proj(x2, norm_w, weights, qnw, knw, xqnw, cos_t, sinlo_t, sinhi_t,
             conv_w8, conv_b, dt_bias_row, a_row, seq, tm):
    tokens = x2.shape[0]
    pos_blocks = seq // tm
    hpt = tm // CONV_HALO
    n_halo = tokens // CONV_HALO
    row = lambda i: (i, 0)
    const = lambda i: (0, 0)
    pos = lambda i: (i % pos_blocks, 0)
    prev = lambda i: (jnp.maximum(i * hpt - 1, 0), 0)
    nxt = lambda i: (jnp.minimum((i + 1) * hpt, n_halo - 1), 0)
    out_widths = [ATT_Q_W, KDUP_W, KDUP_W, SSD_WIDTH, SSD_XBC_W, XATT_W,
                  LANES, LANES, LANES, LANES]
    out_dtypes = [bf16, bf16, bf16, bf16, bf16, bf16, f32, f32, f32, f32]
    out_specs = [pl.BlockSpec((tm, w), row) for w in out_widths]
    out_shape = [jax.ShapeDtypeStruct((tokens, w), d) for w, d in zip(out_widths, out_dtypes)]
    out_specs[2] = pl.BlockSpec((ATT_KV_W, tm), lambda i: (i // pos_blocks, i % pos_blocks))
    out_shape[2] = jax.ShapeDtypeStruct((tokens // seq * ATT_KV_W, seq), bf16)
    return pl.pallas_call(
        functools.partial(_in_proj_body, pos_blocks),
        grid=(tokens // tm,),
        in_specs=[
            pl.BlockSpec((tm, D_MODEL), row),
            pl.BlockSpec((CONV_HALO, D_MODEL), prev),
            pl.BlockSpec((CONV_HALO, D_MODEL), nxt),
            pl.BlockSpec((1, D_MODEL), const),
            *[pl.BlockSpec(w.shape, const, pipeline_mode=pl.Buffered(1)) for w in weights],
            pl.BlockSpec((1, ATT_Q_W), const),
            pl.BlockSpec((1, KDUP_W), const),
            pl.BlockSpec((1, XATT_HEAD_DIM), const),
            pl.BlockSpec((tm, LANES), pos),
            pl.BlockSpec((tm, LANES), pos),
            pl.BlockSpec((tm, LANES), pos),
            pl.BlockSpec((8, SSD_XBC_W), const),
            pl.BlockSpec((1, SSD_XBC_W), const),
            pl.BlockSpec((1, DT_PAD_W), const),
            pl.BlockSpec((1, DT_PAD_W), const),
        ],
        out_specs=out_specs,
        out_shape=out_shape,
        scratch_shapes=[pltpu.VMEM((tm + 2 * CONV_HALO, SSD_XBC_W), f32)],
        compiler_params=pltpu.CompilerParams(
            dimension_semantics=("arbitrary",), vmem_limit_bytes=VMEM_LIMIT),
        name="in_proj",
    )(x2, x2, x2, norm_w, *weights, qnw, knw, xqnw, cos_t, sinlo_t, sinhi_t,
      conv_w8, conv_b, dt_bias_row, a_row)


def _attn_kernel(sink_ref, mfirst_ref, mmid_ref, mlast_ref, q_ref, kp_ref, kc_ref, kn_ref,
                 vp_ref, vc_ref, vn_ref, o_ref):
    blk = ATT_BLOCK
    per = q_ref.shape[0] // blk
    heads_per_kv = ATT_HEADS // ATT_KV_HEADS
    hpc = ATT_HEADS_PER_CHAIN
    cols = hpc * blk
    low_lanes = _iota((blk, LANES), 1) < ATT_HEAD_DIM
    head_of_col = _iota((1, cols), 1) // blk
    masks = [mfirst_ref] + [mmid_ref] * (per - 2) + [mlast_ref]

    kblocks, vblocks = [], []
    for g in range(ATT_KV_HEADS):
        gs = slice(g * LANES, (g + 1) * LANES)
        vs = slice(g * ATT_HEAD_DIM, (g + 1) * ATT_HEAD_DIM)
        kblocks.append([kp_ref[:, gs]] + [kc_ref[i * blk:(i + 1) * blk, gs] for i in range(per)]
                       + [kn_ref[:, gs]])
        vblocks.append([vp_ref[vs, :]] + [vc_ref[vs, i * blk:(i + 1) * blk] for i in range(per)]
                       + [vn_ref[vs, :]])
    ones_rows = jnp.ones((LANES - ATT_HEAD_DIM, 3 * blk), bf16)

    def scores(g, t, h0):
        qrows = slice(t * blk, (t + 1) * blk)
        kcat = jnp.concatenate(kblocks[g][t:t + 3], axis=0)
        qs = []
        sink = jnp.zeros((1, cols), f32)
        for j in range(hpc):
            hd = h0 + j
            pair = q_ref[qrows, (hd // 2) * LANES:(hd // 2 + 1) * LANES]
            keep = low_lanes if hd % 2 == 0 else jnp.logical_not(low_lanes)
            qs.append(jnp.where(keep, pair, jnp.zeros_like(pair)))
            sink = jnp.where(head_of_col == j, sink_ref[hd] * LOG2E, sink)
        qg = jnp.concatenate(qs, axis=0)
        raw = _dot_nt(kcat, qg)
        st = jnp.concatenate(
            [raw[0:blk] + masks[t][0:blk, 0:cols], raw[blk:2 * blk],
             raw[2 * blk:] + masks[t][2 * blk:, 0:cols]], axis=0)
        m = jnp.maximum(jnp.max(st, axis=0, keepdims=True), sink)
        return st, m, sink, raw[3 * blk - 1:3 * blk, :]

    def finish(g, t, h0, st, m, sink, tail_next):
        qrows = slice(t * blk, (t + 1) * blk)
        vt = jnp.concatenate(
            [jnp.concatenate(vblocks[g][t:t + 3], axis=1), ones_rows], axis=0)
        if tail_next is not None:
            vt = _scheduled_after(vt, tail_next)
        p = jnp.exp2(st - m).astype(bf16)
        ov = _dot(vt, p)
        den = ov[ATT_HEAD_DIM:ATT_HEAD_DIM + 1] + jnp.exp2(sink - m)
        ot = ov[0:ATT_HEAD_DIM] / den
        for jp in range(hpc // 2):
            even = ot[:, (2 * jp) * blk:(2 * jp + 1) * blk]
            odd = ot[:, (2 * jp + 1) * blk:(2 * jp + 2) * blk]
            pr = h0 // 2 + jp
            o_ref[qrows, pr * LANES:(pr + 1) * LANES] = (
                jnp.concatenate([even, odd], axis=0).T.astype(bf16))

    chains = [(g, t, g * heads_per_kv + c * hpc) for g in range(ATT_KV_HEADS)
              for t in range(per) for c in range(heads_per_kv // hpc)]
    nxt = scores(*chains[0])
    for idx, ch in enumerate(chains):
        st, m, sink, _ = nxt
        nxt = scores(*chains[idx + 1]) if idx + 1 < len(chains) else None
        finish(*ch, st, m, sink, None if nxt is None else nxt[3])


def _attn_masks():
    blk = ATT_BLOCK
    cols = (ATT_HEADS // ATT_KV_HEADS) * blk
    key = np.arange(3 * blk)[:, None]
    qi = np.arange(cols)[None, :] % blk
    band = (key >= qi) & (key <= qi + 2 * WINDOW)
    variants = [band & (key >= blk), band, band & (key < 2 * blk)]
    return jnp.asarray(np.where(np.stack(variants), 0.0, NEG_BIG), dtype=f32)


def _attention(sink, q, kdup, vt, batch, seq):
    blk = ATT_BLOCK
    nb = seq // blk
    per = min(ATT_BLOCKS_PER_STEP, nb)
    ns = nb // per
    cols = (ATT_HEADS // ATT_KV_HEADS) * blk
    masks = _attn_masks()
    cur = lambda b, s: (b * ns + s, 0)
    prev = lambda b, s: (b * nb + jnp.maximum(per * s - 1, 0), 0)
    nxt = lambda b, s: (b * nb + jnp.minimum(per * s + per, nb - 1), 0)
    vcur = lambda b, s: (b, s)
    vprev = lambda b, s: (b, jnp.maximum(per * s - 1, 0))
    vnxt = lambda b, s: (b, jnp.minimum(per * s + per, nb - 1))
    mfirst = lambda b, s: (jnp.where(s == 0, 0, 1), 0, 0)
    mmid = lambda b, s: (1, 0, 0)
    mlast = lambda b, s: (jnp.where(s == ns - 1, 2, 1), 0, 0)
    return pl.pallas_call(
        _attn_kernel,
        grid=(batch, ns),
        in_specs=[
            pl.BlockSpec(memory_space=pltpu.SMEM),
            pl.BlockSpec((None, 3 * blk, cols), mfirst),
            pl.BlockSpec((None, 3 * blk, cols), mmid),
            pl.BlockSpec((None, 3 * blk, cols), mlast),
            pl.BlockSpec((per * blk, ATT_Q_W), cur),
            pl.BlockSpec((blk, KDUP_W), prev),
            pl.BlockSpec((per * blk, KDUP_W), cur),
            pl.BlockSpec((blk, KDUP_W), nxt),
            pl.BlockSpec((ATT_KV_W, blk), vprev),
            pl.BlockSpec((ATT_KV_W, per * blk), vcur),
            pl.BlockSpec((ATT_KV_W, blk), vnxt),
        ],
        out_specs=pl.BlockSpec((per * blk, ATT_Q_W), cur),
        out_shape=jax.ShapeDtypeStruct((batch * seq, ATT_Q_W), bf16),
        compiler_params=pltpu.CompilerParams(
            dimension_semantics=("arbitrary", "arbitrary"), vmem_limit_bytes=VMEM_LIMIT),
        name="attention",
    )(sink, masks, masks, masks, q, kdup, kdup, kdup, vt, vt, vt)


def _ssd_kernel(xc_ref, rows_ref, rowsl_ref, cols_ref, dtT_ref, dmat_ref, y_ref,
                hf_s, sb_s, hf_run, hb_run):
    ph = pl.program_id(1)
    step = pl.program_id(2)
    nsteps = pl.num_programs(2)
    Q = SSD_CHUNK
    per = xc_ref.shape[0] // Q
    heads_per_group = SSD_HEADS // SSD_GROUPS
    low_lanes = _iota((Q, LANES), 1) < SSD_HEAD_DIM
    low_row = _iota((1, LANES), 1) < SSD_HEAD_DIM
    li = _iota((Q, Q), 0)
    si = _iota((Q, Q), 1)

    def chunk_states(c, rs):
        rows = rows_ref[rs, :]
        dtT = dtT_ref[rs, :]
        pre = rows[0:SSD_HEADS]
        suf = rows[SSD_HEADS:2 * SSD_HEADS]
        pre_end = jnp.broadcast_to(pre[:, Q - 1:Q], (SSD_HEADS, Q))
        suf_end = jnp.broadcast_to(suf[:, 0:1], (SSD_HEADS, Q))
        wf = jnp.exp2(pre_end - pre) * dtT[0:SSD_HEADS]
        wb = jnp.exp2(suf_end - suf) * dtT[SSD_HEADS:2 * SSD_HEADS]
        decf = jnp.exp2(pre_end)

        for g in range(SSD_GROUPS):
            bT = xc_ref[rs, C_B + g * SSD_STATE:C_B + (g + 1) * SSD_STATE].astype(f32).T
            for m in range(heads_per_group // 2):
                h0 = g * heads_per_group + 2 * m
                h1 = h0 + 1
                lhs = jnp.concatenate(
                    [bT * wf[h0:h0 + 1], bT * wb[h0:h0 + 1],
                     bT * wf[h1:h1 + 1], bT * wb[h1:h1 + 1]], axis=0).astype(bf16)
                cs = slice(h0 * SSD_HEAD_DIM, h0 * SSD_HEAD_DIM + LANES)
                res = _dot(lhs, xc_ref[rs, cs])
                s_f = jnp.where(low_lanes, res[0:Q], res[2 * Q:3 * Q])
                s_b = jnp.where(low_lanes, res[Q:2 * Q], res[3 * Q:4 * Q])
                dec = jnp.where(low_row, decf[h0:h0 + 1], decf[h1:h1 + 1])
                prev = hf_run[:, cs]
                hf_s[c, :, cs] = prev.astype(bf16)
                hf_run[:, cs] = dec * prev + s_f
                sb_s[c, :, cs] = s_b.astype(bf16)

    def chunk_outputs(c, rs):
        rows = rows_ref[rs, :]
        suf = rows[SSD_HEADS:2 * SSD_HEADS]
        cols = cols_ref[rs, :]
        rowsl = rowsl_ref[rs, :]
        decb = jnp.exp2(jnp.broadcast_to(suf[:, 0:1], (SSD_HEADS, Q)))
        lower = li >= si
        upper = si >= li

        for g in range(SSD_GROUPS):
            bg = xc_ref[rs, C_B + g * SSD_STATE:C_B + (g + 1) * SSD_STATE]
            cg = xc_ref[rs, C_C + g * SSD_STATE:C_C + (g + 1) * SSD_STATE]
            cbm = _dot_nt(cg, bg).astype(bf16)
            for m in range(heads_per_group // 2):
                h0 = g * heads_per_group + 2 * m
                parts = []
                for hd in (h0, h0 + 1):
                    hb = SSD_HEADS + hd
                    colf = jnp.broadcast_to(cols[:, hd:hd + 1], (Q, Q))
                    colb = jnp.broadcast_to(cols[:, hb:hb + 1], (Q, Q))
                    ef = jnp.exp2(jnp.where(lower, colf - rowsl[hd:hd + 1], NEG_BIG)).astype(bf16)
                    eb = jnp.exp2(jnp.where(upper, colb - rowsl[hb:hb + 1], NEG_BIG)).astype(bf16)
                    gm = cbm * (ef + eb) + dmat_ref[hd]
                    parts.append(jnp.concatenate(
                        [gm, cg * jnp.exp2(colf).astype(bf16), cg * jnp.exp2(colb).astype(bf16)],
                        axis=1))
                lhs = jnp.concatenate(parts, axis=0)
                cs = slice(h0 * SSD_HEAD_DIM, h0 * SSD_HEAD_DIM + LANES)
                hb_prev = hb_run[:, cs]
                rhs = jnp.concatenate(
                    [xc_ref[rs, cs], hf_s[c, :, cs], hb_prev.astype(bf16)], axis=0)
                res = _dot(lhs, rhs)
                y_ref[rs, cs] = jnp.where(low_lanes, res[0:Q], res[Q:2 * Q]).astype(bf16)
                dec = jnp.where(low_row, decb[h0:h0 + 1], decb[h0 + 1:h0 + 2])
                hb_run[:, cs] = dec * hb_prev + sb_s[c, :, cs].astype(f32)

    @pl.when(ph == 0)
    def _phase0():
        @pl.when(step == 0)
        def _():
            hf_run[...] = jnp.zeros_like(hf_run)

        for sub in range(per):
            chunk_states(step * per + sub, slice(sub * Q, (sub + 1) * Q))

    @pl.when(ph == 1)
    def _phase1():
        @pl.when(step == 0)
        def _():
            hb_run[...] = jnp.zeros_like(hb_run)

        back_step = nsteps - 1 - step
        for sub in reversed(range(per)):
            chunk_outputs(back_step * per + sub, slice(sub * Q, (sub + 1) * Q))


def _ssd(dmat, xc, rows, rowsl, cols, dtT, batch, seq):
    Q = SSD_CHUNK
    nc = seq // Q
    rb = min(SSD_CHUNKS_PER_STEP, nc) * Q
    ns = seq // rb
    last = ns - 1

    def both_map(b, ph, s):
        return (b * ns + s * (1 - ph) + (last - s) * ph, 0)

    def fwd_map(b, ph, s):
        return (b * ns + s * (1 - ph) + last * ph, 0)

    def back_map(b, ph, s):
        return (b * ns + last - s * ph, 0)

    return pl.pallas_call(
        _ssd_kernel,
        grid=(batch, 2, ns),
        in_specs=[
            pl.BlockSpec((rb, SSD_XBC_W), both_map),
            pl.BlockSpec((rb, LANES), both_map),
            pl.BlockSpec((rb, LANES), back_map),
            pl.BlockSpec((rb, LANES), back_map),
            pl.BlockSpec((rb, LANES), fwd_map),
            pl.BlockSpec((SSD_HEADS, Q, Q), lambda b, ph, s: (0, 0, 0)),
        ],
        out_specs=pl.BlockSpec((rb, SSD_WIDTH), back_map),
        out_shape=jax.ShapeDtypeStruct((batch * seq, SSD_WIDTH), bf16),
        scratch_shapes=[
            pltpu.VMEM((nc, SSD_STATE, SSD_WIDTH), bf16),
            pltpu.VMEM((nc, SSD_STATE, SSD_WIDTH), bf16),
            pltpu.VMEM((SSD_STATE, SSD_WIDTH), f32),
            pltpu.VMEM((SSD_STATE, SSD_WIDTH), f32),
        ],
        compiler_params=pltpu.CompilerParams(
            dimension_semantics=("arbitrary", "arbitrary", "arbitrary"),
            vmem_limit_bytes=VMEM_LIMIT),
        name="ssd",
    )(xc, rows, rowsl, cols, dtT, dmat)


def _mem_kv(mem_ref, nw_ref, w_ref, knw_ref, k_ref, v_ref):
    h = _rms_rows(mem_ref[...], nw_ref[...]).astype(bf16)
    kv = _dot(h, w_ref[...])
    for c in range(XATT_HEADS):
        sl = slice(c * XATT_HEAD_DIM, (c + 1) * XATT_HEAD_DIM)
        t = kv[:, sl]
        k_ref[:, sl] = (t * lax.rsqrt(jnp.mean(t * t, axis=-1, keepdims=True) + EPS)
                        * knw_ref[...]).astype(bf16)
    v_ref[...] = kv[:, XATT_W:].astype(bf16)


def _xattn_kernel(q_ref, mem_ref, nw_ref, w_ref, knw_ref, o_ref, k_ref, v_ref):
    @pl.when(pl.program_id(1) == 0)
    def _():
        _mem_kv(mem_ref, nw_ref, w_ref, knw_ref, k_ref, v_ref)

    for hd in range(XATT_HEADS):
        sl = slice(hd * XATT_HEAD_DIM, (hd + 1) * XATT_HEAD_DIM)
        s = _dot_nt(q_ref[:, sl], k_ref[:, sl])
        m = jnp.max(s, axis=-1, keepdims=True)
        p = jnp.exp2(s - m)
        den = jnp.sum(p, axis=-1, keepdims=True)
        o_ref[:, sl] = (_dot(p.astype(bf16), v_ref[:, sl]) / den).astype(bf16)


def _xattn(qx, mem2, mem_norm_w, w_kv, xk_norm_w, batch, seq, mem_len, tq):
    nq = seq // tq
    qmap = lambda b, i: (b * nq + i, 0)
    mmap = lambda b, i: (b, 0)
    const = lambda b, i: (0, 0)
    return pl.pallas_call(
        _xattn_kernel,
        grid=(batch, nq),
        in_specs=[
            pl.BlockSpec((tq, XATT_W), qmap),
            pl.BlockSpec((mem_len, D_MODEL), mmap),
            pl.BlockSpec((1, D_MODEL), const),
            pl.BlockSpec((D_MODEL, 2 * XATT_W), const),
            pl.BlockSpec((1, XATT_HEAD_DIM), const),
        ],
        out_specs=pl.BlockSpec((tq, XATT_W), qmap),
        out_shape=jax.ShapeDtypeStruct((batch * seq, XATT_W), bf16),
        scratch_shapes=[pltpu.VMEM((mem_len, XATT_W), bf16),
                        pltpu.VMEM((mem_len, XATT_W), bf16)],
        compiler_params=pltpu.CompilerParams(
            dimension_semantics=("arbitrary", "arbitrary"), vmem_limit_bytes=VMEM_LIMIT),
        name="xattn",
    )(qx, mem2, mem_norm_w, w_kv, xk_norm_w)


FF_CHUNK = 1024


def _out_mlp_kernel(x_ref, a_ref, s_ref, z_ref, c_ref, wo_ref, snw_ref, nw_ref, wu_ref, wd_ref,
                    o_ref):
    y = s_ref[...].astype(f32) * z_ref[...].astype(f32)
    gw = SSD_WIDTH // SSD_GROUPS
    mix = (_dot(a_ref[...], wo_ref[0:ATT_Q_W, :])
           + _dot(c_ref[...], wo_ref[ATT_Q_W + SSD_WIDTH:, :]))
    for g in range(SSD_GROUPS):
        yg = y[:, g * gw:(g + 1) * gw]
        ms = jnp.mean(yg * yg, axis=-1, keepdims=True)
        yn = (yg * lax.rsqrt(ms + EPS) * snw_ref[:, g * gw:(g + 1) * gw]).astype(bf16)
        mix = mix + _dot(yn, wo_ref[ATT_Q_W + g * gw:ATT_Q_W + (g + 1) * gw, :])
    x1 = x_ref[...] + mix
    h = _rms_rows(x1, nw_ref[...]).astype(bf16)
    acc = x1
    for j in range(D_FF // FF_CHUNK):
        u = _dot(h, wu_ref[:, j * FF_CHUNK:(j + 1) * FF_CHUNK])
        r = jnp.maximum(u, 0.0)
        acc = acc + _dot((r * r).astype(bf16), wd_ref[j * FF_CHUNK:(j + 1) * FF_CHUNK, :])
    o_ref[...] = acc


def _out_mlp(x2, attn, ssd, z, xatt, w_out, ssd_norm_w, norm_w, w_up, w_down, tm):
    tokens = x2.shape[0]
    row = lambda i: (i, 0)
    const = lambda i: (0, 0)
    resident = functools.partial(pl.BlockSpec, index_map=const, pipeline_mode=pl.Buffered(1))
    return pl.pallas_call(
        _out_mlp_kernel,
        grid=(tokens // tm,),
        in_specs=[
            pl.BlockSpec((tm, D_MODEL), row),
            pl.BlockSpec((tm, ATT_Q_W), row),
            pl.BlockSpec((tm, SSD_WIDTH), row),
            pl.BlockSpec((tm, SSD_WIDTH), row),
            pl.BlockSpec((tm, XATT_W), row),
            resident((D_MIX, D_MODEL)),
            pl.BlockSpec((1, SSD_WIDTH), const),
            pl.BlockSpec((1, D_MODEL), const),
            resident((D_MODEL, D_FF)),
            resident((D_FF, D_MODEL)),
        ],
        out_specs=pl.BlockSpec((tm, D_MODEL), row),
        out_shape=jax.ShapeDtypeStruct((tokens, D_MODEL), f32),
        compiler_params=pltpu.CompilerParams(
            dimension_semantics=("arbitrary",), vmem_limit_bytes=VMEM_LIMIT),
        name="out_mlp",
    )(x2, attn, ssd, z, xatt, w_out, ssd_norm_w, norm_w, w_up, w_down)


def _rope_tables(seq):
    half = ROPE_DIM // 2
    inv = ROPE_THETA ** (-jnp.arange(0, ROPE_DIM, 2, dtype=f32) / ROPE_DIM)
    ang = jnp.arange(seq, dtype=f32)[:, None] * inv[None, :]
    cos = jnp.cos(ang)
    sin = jnp.sin(ang)
    ones = jnp.ones((seq, ATT_HEAD_DIM - ROPE_DIM), f32)
    zeros_h = jnp.zeros((seq, half), f32)
    zeros_p = jnp.zeros((seq, ATT_HEAD_DIM - ROPE_DIM), f32)
    cos_head = jnp.concatenate([cos, cos, ones], axis=1)
    lo_head = jnp.concatenate([-sin, zeros_h, zeros_p], axis=1)
    hi_head = jnp.concatenate([zeros_h, sin, zeros_p], axis=1)
    rep = LANES // ATT_HEAD_DIM
    return (jnp.tile(cos_head, (1, rep)), jnp.tile(lo_head, (1, rep)),
            jnp.tile(hi_head, (1, rep)))


def _layer(x, mem, norm_mix_w, w_in, q_norm_w, k_norm_w, attn_sink, conv_w, conv_b,
           dt_bias_f, dt_bias_b, a_log_f, a_log_b, ssd_d, ssd_norm_w, mem_norm_w,
           w_mem_kv, xq_norm_w, xk_norm_w, w_out, norm_mlp_w, w_mlp_up, w_mlp_down, tables):
    batch, seq, _ = x.shape
    mem_len = mem.shape[1]
    tokens = batch * seq
    tm = TOKEN_TILE
    assert x.shape[2] == D_MODEL and mem.shape[2] == D_MODEL
    assert seq % tm == 0 and mem_len % BF16_SUBLANES == 0
    assert seq % (ATT_BLOCK * min(ATT_BLOCKS_PER_STEP, seq // ATT_BLOCK)) == 0
    assert seq % (SSD_CHUNK * min(SSD_CHUNKS_PER_STEP, seq // SSD_CHUNK)) == 0
    assert seq % min(XATT_TQ, seq) == 0

    s = np.cumsum([ATT_Q_W, ATT_KV_W, ATT_KV_W, SSD_WIDTH, SSD_XBC_W, SSD_DT_W, XATT_W])
    piece = lambda lo, hi: w_in[:, lo:hi].astype(bf16)
    w_q, w_k, w_v = piece(0, s[0]), piece(s[0], s[1]), piece(s[1], s[2])
    w_z, w_xbc, w_dt, w_qx = (piece(s[2], s[3]), piece(s[3], s[4]),
                              piece(s[4], s[5]), piece(s[5], s[6]))
    hd = ATT_HEAD_DIM
    dup = lambda w: jnp.concatenate(
        [w[:, g * hd:(g + 1) * hd] for g in range(ATT_KV_HEADS) for _ in range(2)], axis=1)
    w_vdt = jnp.concatenate(
        [w_v, w_dt, jnp.zeros((D_MODEL, DT_PAD_W - SSD_DT_W), bf16)], axis=1)
    weights = (w_q, dup(w_k), w_vdt, w_z, w_xbc, w_qx)

    pad = jnp.zeros((DT_PAD_W - SSD_DT_W,), f32)
    dt_bias_row = jnp.concatenate([dt_bias_f, dt_bias_b, pad]).reshape(1, DT_PAD_W)
    a_row = jnp.concatenate([-jnp.exp(a_log_f), -jnp.exp(a_log_b), pad]).reshape(1, DT_PAD_W)
    conv_w8 = jnp.concatenate([conv_w, jnp.zeros((8 - SSD_CONV, SSD_XBC_W), f32)], axis=0)
    dmat = (ssd_d.astype(f32)[:, None, None] * jnp.eye(SSD_CHUNK, dtype=f32)).astype(bf16)

    x2 = x.reshape(tokens, D_MODEL)
    cos_t, sinlo_t, sinhi_t = tables
    q, kdup, vt, z, xc, qx, rows, rowsl, cols, dtT = _in_proj(
        x2, norm_mix_w.reshape(1, D_MODEL), weights,
        jnp.tile(q_norm_w, ATT_HEADS).reshape(1, ATT_Q_W),
        jnp.tile(k_norm_w, KDUP_W // hd).reshape(1, KDUP_W),
        xq_norm_w.reshape(1, XATT_HEAD_DIM), cos_t, sinlo_t, sinhi_t,
        conv_w8, conv_b.reshape(1, SSD_XBC_W), dt_bias_row, a_row, seq, tm)

    attn = _attention(attn_sink.astype(f32), q, kdup, vt, batch, seq)
    ssd = _ssd(dmat, xc, rows, rowsl, cols, dtT, batch, seq)

    xatt = _xattn(qx, mem.reshape(batch * mem_len, D_MODEL), mem_norm_w.reshape(1, D_MODEL),
                  w_mem_kv.astype(bf16), xk_norm_w.reshape(1, XATT_HEAD_DIM),
                  batch, seq, mem_len, min(XATT_TQ, seq))

    out = _out_mlp(x2, attn, ssd, z, xatt, w_out.astype(bf16),
                   ssd_norm_w.reshape(1, SSD_WIDTH), norm_mlp_w.reshape(1, D_MODEL),
                   w_mlp_up.astype(bf16), w_mlp_down.astype(bf16), tm)
    return out.reshape(batch, seq, D_MODEL)


def kernel(x, mem, norm_mix_w, w_in, q_norm_w, k_norm_w, attn_sink, conv_w, conv_b,
           dt_bias_f, dt_bias_b, a_log_f, a_log_b, ssd_d, ssd_norm_w, mem_norm_w,
           w_mem_kv, xq_norm_w, xk_norm_w, w_out, norm_mlp_w, w_mlp_up, w_mlp_down):
    depth = w_in.shape[0]
    tables = _rope_tables(x.shape[1])
    for i in range(depth):
        x = _layer(x, mem, norm_mix_w[i], w_in[i], q_norm_w[i], k_norm_w[i], attn_sink[i],
                   conv_w[i], conv_b[i], dt_bias_f[i], dt_bias_b[i], a_log_f[i], a_log_b[i],
                   ssd_d[i], ssd_norm_w[i], mem_norm_w[i], w_mem_kv[i], xq_norm_w[i],
                   xk_norm_w[i], w_out[i], norm_mlp_w[i], w_mlp_up[i], w_mlp_down[i], tables)
    return x
```

```python
import functools

import numpy as np
import jax
import jax.numpy as jnp
from jax import lax
from jax.experimental import pallas as pl
from jax.experimental.pallas import tpu as pltpu

D_MODEL = 1024
EPS = 1e-6

ATT_HEADS = 8
ATT_KV_HEADS = 2
ATT_HEAD_DIM = 64
ATT_Q_W = ATT_HEADS * ATT_HEAD_DIM
ATT_KV_W = ATT_KV_HEADS * ATT_HEAD_DIM
WINDOW = 128
ATT_BLOCK = 128
ATT_BLOCKS_PER_STEP = 16
ATT_HEADS_PER_CHAIN = 4
ROPE_THETA = 500000.0
ROPE_DIM = ATT_HEAD_DIM // 4

SSD_HEADS = 16
SSD_HEAD_DIM = 64
SSD_WIDTH = SSD_HEADS * SSD_HEAD_DIM
SSD_GROUPS = 2
SSD_STATE = 128
SSD_CONV = 5
SSD_CHUNK = 128
SSD_CHUNKS_PER_STEP = 16
SSD_XBC_W = SSD_WIDTH + 2 * SSD_GROUPS * SSD_STATE
SSD_DT_W = 2 * SSD_HEADS

XATT_HEADS = 4
XATT_HEAD_DIM = 128
XATT_W = XATT_HEADS * XATT_HEAD_DIM
XATT_TQ = 2048
TOKEN_TILE = 512

D_FF = 4 * D_MODEL
D_MIX = 2 * D_MODEL

LANES = 128
MXU_WIDTH = 256
BF16_SUBLANES = 16
F32_SUBLANES = 8
VMEM_LIMIT = 56 * 1024 * 1024

KDUP_W = 2 * ATT_KV_W
DT_PAD_W = LANES
CONV_HALO = BF16_SUBLANES
C_B = SSD_WIDTH
C_C = SSD_WIDTH + SSD_GROUPS * SSD_STATE

NEG_BIG = -1e30
LOG2E = 1.4426950408889634

bf16 = jnp.bfloat16
f32 = jnp.float32


def _dot(a, b):
    return jnp.dot(a, b, preferred_element_type=f32)


def _dot_nt(a, b):
    return lax.dot_general(a, b, (((1,), (1,)), ((), ())), preferred_element_type=f32)


def _split3(x):
    x1 = x.astype(bf16)
    r1 = x - x1.astype(f32)
    x2 = r1.astype(bf16)
    r2 = r1 - x2.astype(f32)
    return x1, x2, r2.astype(bf16)


def _dot3(x, m01):
    x1, x2, x3 = _split3(x)
    return _dot(x1, m01) + _dot(x2, m01) + _dot(x3, m01)


def _iota(shape, dim):
    return lax.broadcasted_iota(jnp.int32, shape, dim)


def _segment_ones(width, seg):
    r = _iota((width, width), 0) // seg
    c = _iota((width, width), 1) // seg
    return jnp.where(r == c, 1.0, 0.0).astype(bf16)


def _head_mean_sq(t, seg):
    width = min(t.shape[1], MXU_WIDTH)
    ones = _segment_ones(width, seg)
    sq = (t * t).astype(bf16)
    sums = [_dot(sq[:, c0:c0 + width], ones) for c0 in range(0, t.shape[1], width)]
    return jnp.concatenate(sums, axis=1) * (1.0 / seg)


def _rope(t, cos, sin_lo, sin_hi):
    half = ROPE_DIM // 2
    return (t * cos + pltpu.roll(t, LANES - half, 1) * sin_lo
            + pltpu.roll(t, half, 1) * sin_hi)


def _scheduled_after(x, anchor):
    folded = anchor[:, 0:LANES]
    for c0 in range(LANES, anchor.shape[1], LANES):
        folded = folded + anchor[:, c0:c0 + LANES]
    bits = pltpu.bitcast(jnp.broadcast_to(folded, (F32_SUBLANES, LANES)), jnp.uint32)
    sixteen = jnp.full(bits.shape, 16, jnp.uint32)
    zero = lax.shift_right_logical(lax.shift_right_logical(bits, sixteen), sixteen)
    zero = jnp.concatenate([zero.astype(f32)] * 2, axis=0).astype(bf16)
    head = x[0:BF16_SUBLANES, :] + jnp.concatenate([zero] * (x.shape[1] // LANES), axis=1)
    return jnp.concatenate([head, x[BF16_SUBLANES:, :]], axis=0)


def _rms_rows(x, w):
    ms = jnp.mean(x * x, axis=-1, keepdims=True)
    return (x * lax.rsqrt(ms + EPS)) * w


def _in_proj_body(pos_blocks, x_ref, xp_ref, xn_ref, nw_ref,
                  wq_ref, wk_ref, wvdt_ref, wz_ref, wxbc_ref, wqx_ref, qnw_ref, knw_ref,
                  xqnw_ref, cos_ref, sinlo_ref, sinhi_ref, cw_ref, cb_ref, dtb_ref, arow_ref,
                  q_ref, k_ref, v_ref, z_ref, xc_ref, qx_ref,
                  rows_ref, rowsl_ref, cols_ref, dtT_ref, conv_s):
    i = pl.program_id(0)
    tm = x_ref.shape[0]
    si = i % pos_blocks
    nw = nw_ref[...]
    h = _rms_rows(x_ref[...], nw).astype(bf16)

    cos = cos_ref[...]
    sin_lo = sinlo_ref[...]
    sin_hi = sinhi_ref[...]

    pq = _dot(h, wq_ref[...])
    qn = pq * lax.rsqrt(_head_mean_sq(pq, ATT_HEAD_DIM) + EPS) * qnw_ref[...]
    for c in range(ATT_Q_W // LANES):
        sl = slice(c * LANES, (c + 1) * LANES)
        q_ref[:, sl] = (_rope(qn[:, sl], cos, sin_lo, sin_hi)
                        * (ATT_HEAD_DIM ** -0.5 * LOG2E)).astype(bf16)

    pk = _dot(h, wk_ref[...])
    kn = pk * lax.rsqrt(_head_mean_sq(pk, ATT_HEAD_DIM) + EPS) * knw_ref[...]
    for c in range(KDUP_W // LANES):
        sl = slice(c * LANES, (c + 1) * LANES)
        k_ref[:, sl] = _rope(kn[:, sl], cos, sin_lo, sin_hi).astype(bf16)

    pvdt = _dot(h, wvdt_ref[...])
    v_ref[...] = pvdt[:, 0:ATT_KV_W].T.astype(bf16)
    pz = _dot(h, wz_ref[...])
    z_ref[...] = (pz / (1.0 + jnp.exp2(pz * (-LOG2E)))).astype(bf16)

    pqx = _dot(h, wqx_ref[...])
    for c in range(XATT_HEADS):
        sl = slice(c * XATT_HEAD_DIM, (c + 1) * XATT_HEAD_DIM)
        t = pqx[:, sl]
        tn = t * lax.rsqrt(jnp.mean(t * t, axis=-1, keepdims=True) + EPS) * xqnw_ref[...]
        qx_ref[:, sl] = (tn * (XATT_HEAD_DIM ** -0.5 * LOG2E)).astype(bf16)

    hp = _rms_rows(xp_ref[...], nw)
    hn = _rms_rows(xn_ref[...], nw)
    hp = jnp.where(si > 0, hp, jnp.zeros_like(hp)).astype(bf16)
    hn = jnp.where(si < pos_blocks - 1, hn, jnp.zeros_like(hn)).astype(bf16)
    h_ext = jnp.concatenate([hp, h, hn], axis=0)
    conv_s[...] = _dot(h_ext, wxbc_ref[...])
    sub = F32_SUBLANES
    nt = tm // sub
    t0 = CONV_HALO // sub
    u3 = conv_s[...].reshape((tm + 2 * CONV_HALO) // sub, sub, SSD_XBC_W)
    row_in_tile = _iota((1, sub, SSD_XBC_W), 1)
    acc = jnp.broadcast_to(cb_ref[...].reshape(1, 1, SSD_XBC_W), (nt, sub, SSD_XBC_W))
    for k in range(SSD_CONV):
        d = k - SSD_CONV // 2
        wk = cw_ref[k:k + 1, :].reshape(1, 1, SSD_XBC_W)
        if d == 0:
            shifted = u3[t0:t0 + nt]
        elif d > 0:
            rot = pltpu.roll(u3, sub - d, 1)
            shifted = jnp.where(row_in_tile < sub - d, rot[t0:t0 + nt], rot[t0 + 1:t0 + nt + 1])
        else:
            rot = pltpu.roll(u3, -d, 1)
            shifted = jnp.where(row_in_tile >= -d, rot[t0:t0 + nt], rot[t0 - 1:t0 + nt - 1])
        acc = acc + shifted * wk
    xc = acc / (1.0 + jnp.exp2(acc * (-LOG2E)))
    xc_ref[...] = xc.reshape(tm, SSD_XBC_W).astype(bf16)

    Q = SSD_CHUNK
    dt = pvdt[:, ATT_KV_W:ATT_KV_W + DT_PAD_W] + dtb_ref[...]
    dt = jnp.maximum(dt, 0.0) + jnp.log1p(jnp.exp(-jnp.abs(dt)))
    a = dt * arow_ref[...]
    li = _iota((Q, Q), 0)
    lj = _iota((Q, Q), 1)
    incl_le = jnp.where(li <= lj, 1.0, 0.0).astype(bf16)
    incl_ge = jnp.where(li >= lj, 1.0, 0.0).astype(bf16)
    pad_rows = jnp.zeros((Q - 2 * SSD_HEADS, Q), f32)
    for ch in range(tm // Q):
        r = slice(ch * Q, (ch + 1) * Q)
        aT = a[r].T
        dtT = dt[r].T
        pre = _dot3(aT[0:SSD_HEADS], incl_le)
        suf = _dot3(aT[SSD_HEADS:2 * SSD_HEADS], incl_ge)
        rows = jnp.concatenate([pre, suf, pad_rows], axis=0) * LOG2E
        rows_ref[r, :] = rows
        rowsl_ref[r, :] = rows - jnp.log(dtT) * LOG2E
        cols_ref[r, :] = rows.T
        dtT_ref[r, :] = dtT


def _in_proj(x2, norm_w, weights, qnw, knw, xqnw, cos_t, sinlo_t, sinhi_t,
             conv_w8, conv_b, dt_bias_row, a_row, seq, tm):
    tokens = x2.shape[0]
    pos_blocks = seq // tm
    hpt = tm // CONV_HALO
    n_halo = tokens // CONV_HALO
    row = lambda i: (i, 0)
    const = lambda i: (0, 0)
    pos = lambda i: (i % pos_blocks, 0)
    prev = lambda i: (jnp.maximum(i * hpt - 1, 0), 0)
    nxt = lambda i: (jnp.minimum((i + 1) * hpt, n_halo - 1), 0)
    out_widths = [ATT_Q_W, KDUP_W, KDUP_W, SSD_WIDTH, SSD_XBC_W, XATT_W,
                  LANES, LANES, LANES, LANES]
    out_dtypes = [bf16, bf16, bf16, bf16, bf16, bf16, f32, f32, f32, f32]
    out_specs = [pl.BlockSpec((tm, w), row) for w in out_widths]
    out_shape = [jax.ShapeDtypeStruct((tokens, w), d) for w, d in zip(out_widths, out_dtypes)]
    out_specs[2] = pl.BlockSpec((ATT_KV_W, tm), lambda i: (i // pos_blocks, i % pos_blocks))
    out_shape[2] = jax.ShapeDtypeStruct((tokens // seq * ATT_KV_W, seq), bf16)
    return pl.pallas_call(
        functools.partial(_in_proj_body, pos_blocks),
        grid=(tokens // tm,),
        in_specs=[
            pl.BlockSpec((tm, D_MODEL), row),
            pl.BlockSpec((CONV_HALO, D_MODEL), prev),
            pl.BlockSpec((CONV_HALO, D_MODEL), nxt),
            pl.BlockSpec((1, D_MODEL), const),
            *[pl.BlockSpec(w.shape, const, pipeline_mode=pl.Buffered(1)) for w in weights],
            pl.BlockSpec((1, ATT_Q_W), const),
            pl.BlockSpec((1, KDUP_W), const),
            pl.BlockSpec((1, XATT_HEAD_DIM), const),
            pl.BlockSpec((tm, LANES), pos),
            pl.BlockSpec((tm, LANES), pos),
            pl.BlockSpec((tm, LANES), pos),
            pl.BlockSpec((8, SSD_XBC_W), const),
            pl.BlockSpec((1, SSD_XBC_W), const),
            pl.BlockSpec((1, DT_PAD_W), const),
            pl.BlockSpec((1, DT_PAD_W), const),
        ],
        out_specs=out_specs,
        out_shape=out_shape,
        scratch_shapes=[pltpu.VMEM((tm + 2 * CONV_HALO, SSD_XBC_W), f32)],
        compiler_params=pltpu.CompilerParams(
            dimension_semantics=("arbitrary",), vmem_limit_bytes=VMEM_LIMIT),
        name="in_proj",
    )(x2, x2, x2, norm_w, *weights, qnw, knw, xqnw, cos_t, sinlo_t, sinhi_t,
      conv_w8, conv_b, dt_bias_row, a_row)


def _attn_kernel(sink_ref, mfirst_ref, mmid_ref, mlast_ref, q_ref, kp_ref, kc_ref, kn_ref,
                 vp_ref, vc_ref, vn_ref, o_ref):
    blk = ATT_BLOCK
    per = q_ref.shape[0] // blk
    heads_per_kv = ATT_HEADS // ATT_KV_HEADS
    hpc = ATT_HEADS_PER_CHAIN
    cols = hpc * blk
    low_lanes = _iota((blk, LANES), 1) < ATT_HEAD_DIM
    head_of_col = _iota((1, cols), 1) // blk
    masks = [mfirst_ref] + [mmid_ref] * (per - 2) + [mlast_ref]

    kblocks, vblocks = [], []
    for g in range(ATT_KV_HEADS):
        gs = slice(g * LANES, (g + 1) * LANES)
        vs = slice(g * ATT_HEAD_DIM, (g + 1) * ATT_HEAD_DIM)
        kblocks.append([kp_ref[:, gs]] + [kc_ref[i * blk:(i + 1) * blk, gs] for i in range(per)]
                       + [kn_ref[:, gs]])
        vblocks.append([vp_ref[vs, :]] + [vc_ref[vs, i * blk:(i + 1) * blk] for i in range(per)]
                       + [vn_ref[vs, :]])
    ones_rows = jnp.ones((LANES - ATT_HEAD_DIM, 3 * blk), bf16)

    def scores(g, t, h0):
        qrows = slice(t * blk, (t + 1) * blk)
        kcat = jnp.concatenate(kblocks[g][t:t + 3], axis=0)
        qs = []
        sink = jnp.zeros((1, cols), f32)
        for j in range(hpc):
            hd = h0 + j
            pair = q_ref[qrows, (hd // 2) * LANES:(hd // 2 + 1) * LANES]
            keep = low_lanes if hd % 2 == 0 else jnp.logical_not(low_lanes)
            qs.append(jnp.where(keep, pair, jnp.zeros_like(pair)))
            sink = jnp.where(head_of_col == j, sink_ref[hd] * LOG2E, sink)
        qg = jnp.concatenate(qs, axis=0)
        raw = _dot_nt(kcat, qg)
        st = jnp.concatenate(
            [raw[0:blk] + masks[t][0:blk, 0:cols], raw[blk:2 * blk],
             raw[2 * blk:] + masks[t][2 * blk:, 0:cols]], axis=0)
        m = jnp.maximum(jnp.max(st, axis=0, keepdims=True), sink)
        return st, m, sink, raw[3 * blk - 1:3 * blk, :]

    def finish(g, t, h0, st, m, sink, tail_next):
        qrows = slice(t * blk, (t + 1) * blk)
        vt = jnp.concatenate(
            [jnp.concatenate(vblocks[g][t:t + 3], axis=1), ones_rows], axis=0)
        if tail_next is not None:
            vt = _scheduled_after(vt, tail_next)
        p = jnp.exp2(st - m).astype(bf16)
        ov = _dot(vt, p)
        den = ov[ATT_HEAD_DIM:ATT_HEAD_DIM + 1] + jnp.exp2(sink - m)
        ot = ov[0:ATT_HEAD_DIM] / den
        for jp in range(hpc // 2):
            even = ot[:, (2 * jp) * blk:(2 * jp + 1) * blk]
            odd = ot[:, (2 * jp + 1) * blk:(2 * jp + 2) * blk]
            pr = h0 // 2 + jp
            o_ref[qrows, pr * LANES:(pr + 1) * LANES] = (
                jnp.concatenate([even, odd], axis=0).T.astype(bf16))

    chains = [(g, t, g * heads_per_kv + c * hpc) for g in range(ATT_KV_HEADS)
              for t in range(per) for c in range(heads_per_kv // hpc)]
    nxt = scores(*chains[0])
    for idx, ch in enumerate(chains):
        st, m, sink, _ = nxt
        nxt = scores(*chains[idx + 1]) if idx + 1 < len(chains) else None
        finish(*ch, st, m, sink, None if nxt is None else nxt[3])


def _attn_masks():
    blk = ATT_BLOCK
    cols = (ATT_HEADS // ATT_KV_HEADS) * blk
    key = np.arange(3 * blk)[:, None]
    qi = np.arange(cols)[None, :] % blk
    band = (key >= qi) & (key <= qi + 2 * WINDOW)
    variants = [band & (key >= blk), band, band & (key < 2 * blk)]
    return jnp.asarray(np.where(np.stack(variants), 0.0, NEG_BIG), dtype=f32)


def _attention(sink, q, kdup, vt, batch, seq):
    blk = ATT_BLOCK
    nb = seq // blk
    per = min(ATT_BLOCKS_PER_STEP, nb)
    ns = nb // per
    cols = (ATT_HEADS // ATT_KV_HEADS) * blk
    masks = _attn_masks()
    cur = lambda b, s: (b * ns + s, 0)
    prev = lambda b, s: (b * nb + jnp.maximum(per * s - 1, 0), 0)
    nxt = lambda b, s: (b * nb + jnp.minimum(per * s + per, nb - 1), 0)
    vcur = lambda b, s: (b, s)
    vprev = lambda b, s: (b, jnp.maximum(per * s - 1, 0))
    vnxt = lambda b, s: (b, jnp.minimum(per * s + per, nb - 1))
    mfirst = lambda b, s: (jnp.where(s == 0, 0, 1), 0, 0)
    mmid = lambda b, s: (1, 0, 0)
    mlast = lambda b, s: (jnp.where(s == ns - 1, 2, 1), 0, 0)
    return pl.pallas_call(
        _attn_kernel,
        grid=(batch, ns),
        in_specs=[
            pl.BlockSpec(memory_space=pltpu.SMEM),
            pl.BlockSpec((None, 3 * blk, cols), mfirst),
            pl.BlockSpec((None, 3 * blk, cols), mmid),
            pl.BlockSpec((None, 3 * blk, cols), mlast),
            pl.BlockSpec((per * blk, ATT_Q_W), cur),
            pl.BlockSpec((blk, KDUP_W), prev),
            pl.BlockSpec((per * blk, KDUP_W), cur),
            pl.BlockSpec((blk, KDUP_W), nxt),
            pl.BlockSpec((ATT_KV_W, blk), vprev),
            pl.BlockSpec((ATT_KV_W, per * blk), vcur),
            pl.BlockSpec((ATT_KV_W, blk), vnxt),
        ],
        out_specs=pl.BlockSpec((per * blk, ATT_Q_W), cur),
        out_shape=jax.ShapeDtypeStruct((batch * seq, ATT_Q_W), bf16),
        compiler_params=pltpu.CompilerParams(
            dimension_semantics=("arbitrary", "arbitrary"), vmem_limit_bytes=VMEM_LIMIT),
        name="attention",
    )(sink, masks, masks, masks, q, kdup, kdup, kdup, vt, vt, vt)


def _ssd_kernel(xc_ref, rows_ref, rowsl_ref, cols_ref, dtT_ref, dmat_ref, y_ref,
                hf_s, sb_s, hf_run, hb_run):
    ph = pl.program_id(1)
    step = pl.program_id(2)
    nsteps = pl.num_programs(2)
    Q = SSD_CHUNK
    per = xc_ref.shape[0] // Q
    heads_per_group = SSD_HEADS // SSD_GROUPS
    low_lanes = _iota((Q, LANES), 1) < SSD_HEAD_DIM
    low_row = _iota((1, LANES), 1) < SSD_HEAD_DIM
    li = _iota((Q, Q), 0)
    si = _iota((Q, Q), 1)

    def chunk_states(c, rs):
        rows = rows_ref[rs, :]
        dtT = dtT_ref[rs, :]
        pre = rows[0:SSD_HEADS]
        suf = rows[SSD_HEADS:2 * SSD_HEADS]
        pre_end = jnp.broadcast_to(pre[:, Q - 1:Q], (SSD_HEADS, Q))
        suf_end = jnp.broadcast_to(suf[:, 0:1], (SSD_HEADS, Q))
        wf = jnp.exp2(pre_end - pre) * dtT[0:SSD_HEADS]
        wb = jnp.exp2(suf_end - suf) * dtT[SSD_HEADS:2 * SSD_HEADS]
        decf = jnp.exp2(pre_end)

        for g in range(SSD_GROUPS):
            bT = xc_ref[rs, C_B + g * SSD_STATE:C_B + (g + 1) * SSD_STATE].astype(f32).T
            for m in range(heads_per_group // 2):
                h0 = g * heads_per_group + 2 * m
                h1 = h0 + 1
                lhs = jnp.concatenate(
                    [bT * wf[h0:h0 + 1], bT * wb[h0:h0 + 1],
                     bT * wf[h1:h1 + 1], bT * wb[h1:h1 + 1]], axis=0).astype(bf16)
                cs = slice(h0 * SSD_HEAD_DIM, h0 * SSD_HEAD_DIM + LANES)
                res = _dot(lhs, xc_ref[rs, cs])
                s_f = jnp.where(low_lanes, res[0:Q], res[2 * Q:3 * Q])
                s_b = jnp.where(low_lanes, res[Q:2 * Q], res[3 * Q:4 * Q])
                dec = jnp.where(low_row, decf[h0:h0 + 1], decf[h1:h1 + 1])
                prev = hf_run[:, cs]
                hf_s[c, :, cs] = prev.astype(bf16)
                hf_run[:, cs] = dec * prev + s_f
                sb_s[c, :, cs] = s_b.astype(bf16)

    def chunk_outputs(c, rs):
        rows = rows_ref[rs, :]
        suf = rows[SSD_HEADS:2 * SSD_HEADS]
        cols = cols_ref[rs, :]
        rowsl = rowsl_ref[rs, :]
        decb = jnp.exp2(jnp.broadcast_to(suf[:, 0:1], (SSD_HEADS, Q)))
        lower = li >= si
        upper = si >= li

        for g in range(SSD_GROUPS):
            bg = xc_ref[rs, C_B + g * SSD_STATE:C_B + (g + 1) * SSD_STATE]
            cg = xc_ref[rs, C_C + g * SSD_STATE:C_C + (g + 1) * SSD_STATE]
            cbm = _dot_nt(cg, bg).astype(bf16)
            for m in range(heads_per_group // 2):
                h0 = g * heads_per_group + 2 * m
                parts, dec_f, dec_b = [], [], []
                for hd in (h0, h0 + 1):
                    hb = SSD_HEADS + hd
                    colf = jnp.broadcast_to(cols[:, hd:hd + 1], (Q, Q))
                    colb = jnp.broadcast_to(cols[:, hb:hb + 1], (Q, Q))
                    ef = jnp.exp2(jnp.where(lower, colf - rowsl[hd:hd + 1], NEG_BIG)).astype(bf16)
                    eb = jnp.exp2(jnp.where(upper, colb - rowsl[hb:hb + 1], NEG_BIG)).astype(bf16)
                    parts.append(cbm * (ef + eb) + dmat_ref[hd])
                    dec_f.append(jnp.exp2(colf))
                    dec_b.append(jnp.exp2(colb))
                cs = slice(h0 * SSD_HEAD_DIM, h0 * SSD_HEAD_DIM + LANES)
                hb_prev = hb_run[:, cs]
                res = _dot(jnp.concatenate(parts, axis=0), xc_ref[rs, cs])
                off = _dot(cg, jnp.concatenate([hf_s[c, :, cs], hb_prev.astype(bf16)], axis=1))
                y = (jnp.where(low_lanes, res[0:Q], res[Q:2 * Q])
                     + off[:, 0:LANES] * jnp.where(low_lanes, dec_f[0], dec_f[1])
                     + off[:, LANES:] * jnp.where(low_lanes, dec_b[0], dec_b[1]))
                y_ref[rs, cs] = y.astype(bf16)
                dec = jnp.where(low_row, decb[h0:h0 + 1], decb[h0 + 1:h0 + 2])
                hb_run[:, cs] = dec * hb_prev + sb_s[c, :, cs].astype(f32)

    @pl.when(ph == 0)
    def _phase0():
        @pl.when(step == 0)
        def _():
            hf_run[...] = jnp.zeros_like(hf_run)

        for sub in range(per):
            chunk_states(step * per + sub, slice(sub * Q, (sub + 1) * Q))

    @pl.when(ph == 1)
    def _phase1():
        @pl.when(step == 0)
        def _():
            hb_run[...] = jnp.zeros_like(hb_run)

        back_step = nsteps - 1 - step
        for sub in reversed(range(per)):
            chunk_outputs(back_step * per + sub, slice(sub * Q, (sub + 1) * Q))


def _ssd(dmat, xc, rows, rowsl, cols, dtT, batch, seq):
    Q = SSD_CHUNK
    nc = seq // Q
    rb = min(SSD_CHUNKS_PER_STEP, nc) * Q
    ns = seq // rb
    last = ns - 1

    def both_map(b, ph, s):
        return (b * ns + s * (1 - ph) + (last - s) * ph, 0)

    def fwd_map(b, ph, s):
        return (b * ns + s * (1 - ph) + last * ph, 0)

    def back_map(b, ph, s):
        return (b * ns + last - s * ph, 0)

    return pl.pallas_call(
        _ssd_kernel,
        grid=(batch, 2, ns),
        in_specs=[
            pl.BlockSpec((rb, SSD_XBC_W), both_map),
            pl.BlockSpec((rb, LANES), both_map),
            pl.BlockSpec((rb, LANES), back_map),
            pl.BlockSpec((rb, LANES), back_map),
            pl.BlockSpec((rb, LANES), fwd_map),
            pl.BlockSpec((SSD_HEADS, Q, Q), lambda b, ph, s: (0, 0, 0)),
        ],
        out_specs=pl.BlockSpec((rb, SSD_WIDTH), back_map),
        out_shape=jax.ShapeDtypeStruct((batch * seq, SSD_WIDTH), bf16),
        scratch_shapes=[
            pltpu.VMEM((nc, SSD_STATE, SSD_WIDTH), bf16),
            pltpu.VMEM((nc, SSD_STATE, SSD_WIDTH), bf16),
            pltpu.VMEM((SSD_STATE, SSD_WIDTH), f32),
            pltpu.VMEM((SSD_STATE, SSD_WIDTH), f32),
        ],
        compiler_params=pltpu.CompilerParams(
            dimension_semantics=("arbitrary", "arbitrary", "arbitrary"),
            vmem_limit_bytes=VMEM_LIMIT),
        name="ssd",
    )(xc, rows, rowsl, cols, dtT, dmat)


def _mem_kv(mem_ref, nw_ref, w_ref, knw_ref, k_ref, v_ref):
    h = _rms_rows(mem_ref[...], nw_ref[...]).astype(bf16)
    kv = _dot(h, w_ref[...])
    for c in range(XATT_HEADS):
        sl = slice(c * XATT_HEAD_DIM, (c + 1) * XATT_HEAD_DIM)
        t = kv[:, sl]
        k_ref[:, sl] = (t * lax.rsqrt(jnp.mean(t * t, axis=-1, keepdims=True) + EPS)
                        * knw_ref[...]).astype(bf16)
    v_ref[...] = kv[:, XATT_W:].astype(bf16)


def _xattn_kernel(q_ref, mem_ref, nw_ref, w_ref, knw_ref, o_ref, k_ref, v_ref):
    @pl.when(pl.program_id(1) == 0)
    def _():
        _mem_kv(mem_ref, nw_ref, w_ref, knw_ref, k_ref, v_ref)

    for hd in range(XATT_HEADS):
        sl = slice(hd * XATT_HEAD_DIM, (hd + 1) * XATT_HEAD_DIM)
        s = _dot_nt(q_ref[:, sl], k_ref[:, sl])
        m = jnp.max(s, axis=-1, keepdims=True)
        p = jnp.exp2(s - m)
        den = jnp.sum(p, axis=-1, keepdims=True)
        o_ref[:, sl] = (_dot(p.astype(bf16), v_ref[:, sl]) / den).astype(bf16)


def _xattn(qx, mem2, mem_norm_w, w_kv, xk_norm_w, batch, seq, mem_len, tq):
    nq = seq // tq
    qmap = lambda b, i: (b * nq + i, 0)
    mmap = lambda b, i: (b, 0)
    const = lambda b, i: (0, 0)
    return pl.pallas_call(
        _xattn_kernel,
        grid=(batch, nq),
        in_specs=[
            pl.BlockSpec((tq, XATT_W), qmap),
            pl.BlockSpec((mem_len, D_MODEL), mmap),
            pl.BlockSpec((1, D_MODEL), const),
            pl.BlockSpec((D_MODEL, 2 * XATT_W), const),
            pl.BlockSpec((1, XATT_HEAD_DIM), const),
        ],
        out_specs=pl.BlockSpec((tq, XATT_W), qmap),
        out_shape=jax.ShapeDtypeStruct((batch * seq, XATT_W), bf16),
        scratch_shapes=[pltpu.VMEM((mem_len, XATT_W), bf16),
                        pltpu.VMEM((mem_len, XATT_W), bf16)],
        compiler_params=pltpu.CompilerParams(
            dimension_semantics=("arbitrary", "arbitrary"), vmem_limit_bytes=VMEM_LIMIT),
        name="xattn",
    )(qx, mem2, mem_norm_w, w_kv, xk_norm_w)


FF_CHUNK = 1024


def _out_mlp_kernel(x_ref, a_ref, s_ref, z_ref, c_ref, wo_ref, snw_ref, nw_ref, wu_ref, wd_ref,
                    o_ref):
    y = s_ref[...].astype(f32) * z_ref[...].astype(f32)
    gw = SSD_WIDTH // SSD_GROUPS
    mix = (_dot(a_ref[...], wo_ref[0:ATT_Q_W, :])
           + _dot(c_ref[...], wo_ref[ATT_Q_W + SSD_WIDTH:, :]))
    for g in range(SSD_GROUPS):
        yg = y[:, g * gw:(g + 1) * gw]
        ms = jnp.mean(yg * yg, axis=-1, keepdims=True)
        yn = (yg * lax.rsqrt(ms + EPS) * snw_ref[:, g * gw:(g + 1) * gw]).astype(bf16)
        mix = mix + _dot(yn, wo_ref[ATT_Q_W + g * gw:ATT_Q_W + (g + 1) * gw, :])
    x1 = x_ref[...] + mix
    h = _rms_rows(x1, nw_ref[...]).astype(bf16)
    acc = x1
    for j in range(D_FF // FF_CHUNK):
        u = _dot(h, wu_ref[:, j * FF_CHUNK:(j + 1) * FF_CHUNK])
        r = jnp.maximum(u, 0.0)
        acc = acc + _dot((r * r).astype(bf16), wd_ref[j * FF_CHUNK:(j + 1) * FF_CHUNK, :])
    o_ref[...] = acc


def _out_mlp(x2, attn, ssd, z, xatt, w_out, ssd_norm_w, norm_w, w_up, w_down, tm):
    tokens = x2.shape[0]
    row = lambda i: (i, 0)
    const = lambda i: (0, 0)
    resident = functools.partial(pl.BlockSpec, index_map=const, pipeline_mode=pl.Buffered(1))
    return pl.pallas_call(
        _out_mlp_kernel,
        grid=(tokens // tm,),
        in_specs=[
            pl.BlockSpec((tm, D_MODEL), row),
            pl.BlockSpec((tm, ATT_Q_W), row),
            pl.BlockSpec((tm, SSD_WIDTH), row),
            pl.BlockSpec((tm, SSD_WIDTH), row),
            pl.BlockSpec((tm, XATT_W), row),
            resident((D_MIX, D_MODEL)),
            pl.BlockSpec((1, SSD_WIDTH), const),
            pl.BlockSpec((1, D_MODEL), const),
            resident((D_MODEL, D_FF)),
            resident((D_FF, D_MODEL)),
        ],
        out_specs=pl.BlockSpec((tm, D_MODEL), row),
        out_shape=jax.ShapeDtypeStruct((tokens, D_MODEL), f32),
        compiler_params=pltpu.CompilerParams(
            dimension_semantics=("arbitrary",), vmem_limit_bytes=VMEM_LIMIT),
        name="out_mlp",
    )(x2, attn, ssd, z, xatt, w_out, ssd_norm_w, norm_w, w_up, w_down)


def _rope_tables(seq):
    half = ROPE_DIM // 2
    inv = ROPE_THETA ** (-jnp.arange(0, ROPE_DIM, 2, dtype=f32) / ROPE_DIM)
    ang = jnp.arange(seq, dtype=f32)[:, None] * inv[None, :]
    cos = jnp.cos(ang)
    sin = jnp.sin(ang)
    ones = jnp.ones((seq, ATT_HEAD_DIM - ROPE_DIM), f32)
    zeros_h = jnp.zeros((seq, half), f32)
    zeros_p = jnp.zeros((seq, ATT_HEAD_DIM - ROPE_DIM), f32)
    cos_head = jnp.concatenate([cos, cos, ones], axis=1)
    lo_head = jnp.concatenate([-sin, zeros_h, zeros_p], axis=1)
    hi_head = jnp.concatenate([zeros_h, sin, zeros_p], axis=1)
    rep = LANES // ATT_HEAD_DIM
    return (jnp.tile(cos_head, (1, rep)), jnp.tile(lo_head, (1, rep)),
            jnp.tile(hi_head, (1, rep)))


def _layer(x, mem, norm_mix_w, w_in, q_norm_w, k_norm_w, attn_sink, conv_w, conv_b,
           dt_bias_f, dt_bias_b, a_log_f, a_log_b, ssd_d, ssd_norm_w, mem_norm_w,
           w_mem_kv, xq_norm_w, xk_norm_w, w_out, norm_mlp_w, w_mlp_up, w_mlp_down, tables):
    batch, seq, _ = x.shape
    mem_len = mem.shape[1]
    tokens = batch * seq
    tm = TOKEN_TILE
    assert x.shape[2] == D_MODEL and mem.shape[2] == D_MODEL
    assert seq % tm == 0 and mem_len % BF16_SUBLANES == 0
    assert seq % (ATT_BLOCK * min(ATT_BLOCKS_PER_STEP, seq // ATT_BLOCK)) == 0
    assert seq % (SSD_CHUNK * min(SSD_CHUNKS_PER_STEP, seq // SSD_CHUNK)) == 0
    assert seq % min(XATT_TQ, seq) == 0

    s = np.cumsum([ATT_Q_W, ATT_KV_W, ATT_KV_W, SSD_WIDTH, SSD_XBC_W, SSD_DT_W, XATT_W])
    piece = lambda lo, hi: w_in[:, lo:hi].astype(bf16)
    w_q, w_k, w_v = piece(0, s[0]), piece(s[0], s[1]), piece(s[1], s[2])
    w_z, w_xbc, w_dt, w_qx = (piece(s[2], s[3]), piece(s[3], s[4]),
                              piece(s[4], s[5]), piece(s[5], s[6]))
    hd = ATT_HEAD_DIM
    dup = lambda w: jnp.concatenate(
        [w[:, g * hd:(g + 1) * hd] for g in range(ATT_KV_HEADS) for _ in range(2)], axis=1)
    w_vdt = jnp.concatenate(
        [w_v, w_dt, jnp.zeros((D_MODEL, DT_PAD_W - SSD_DT_W), bf16)], axis=1)
    weights = (w_q, dup(w_k), w_vdt, w_z, w_xbc, w_qx)

    pad = jnp.zeros((DT_PAD_W - SSD_DT_W,), f32)
    dt_bias_row = jnp.concatenate([dt_bias_f, dt_bias_b, pad]).reshape(1, DT_PAD_W)
    a_row = jnp.concatenate([-jnp.exp(a_log_f), -jnp.exp(a_log_b), pad]).reshape(1, DT_PAD_W)
    conv_w8 = jnp.concatenate([conv_w, jnp.zeros((8 - SSD_CONV, SSD_XBC_W), f32)], axis=0)
    dmat = (ssd_d.astype(f32)[:, None, None] * jnp.eye(SSD_CHUNK, dtype=f32)).astype(bf16)

    x2 = x.reshape(tokens, D_MODEL)
    cos_t, sinlo_t, sinhi_t = tables
    q, kdup, vt, z, xc, qx, rows, rowsl, cols, dtT = _in_proj(
        x2, norm_mix_w.reshape(1, D_MODEL), weights,
        jnp.tile(q_norm_w, ATT_HEADS).reshape(1, ATT_Q_W),
        jnp.tile(k_norm_w, KDUP_W // hd).reshape(1, KDUP_W),
        xq_norm_w.reshape(1, XATT_HEAD_DIM), cos_t, sinlo_t, sinhi_t,
        conv_w8, conv_b.reshape(1, SSD_XBC_W), dt_bias_row, a_row, seq, tm)

    attn = _attention(attn_sink.astype(f32), q, kdup, vt, batch, seq)
    ssd = _ssd(dmat, xc, rows, rowsl, cols, dtT, batch, seq)

    xatt = _xattn(qx, mem.reshape(batch * mem_len, D_MODEL), mem_norm_w.reshape(1, D_MODEL),
                  w_mem_kv.astype(bf16), xk_norm_w.reshape(1, XATT_HEAD_DIM),
                  batch, seq, mem_len, min(XATT_TQ, seq))

    out = _out_mlp(x2, attn, ssd, z, xatt, w_out.astype(bf16),
                   ssd_norm_w.reshape(1, SSD_WIDTH), norm_mlp_w.reshape(1, D_MODEL),
                   w_mlp_up.astype(bf16), w_mlp_down.astype(bf16), tm)
    return out.reshape(batch, seq, D_MODEL)


def kernel(x, mem, norm_mix_w, w_in, q_norm_w, k_norm_w, attn_sink, conv_w, conv_b,
           dt_bias_f, dt_bias_b, a_log_f, a_log_b, ssd_d, ssd_norm_w, mem_norm_w,
           w_mem_kv, xq_norm_w, xk_norm_w, w_out, norm_mlp_w, w_mlp_up, w_mlp_down):
    depth = w_in.shape[0]
    tables = _rope_tables(x.shape[1])
    for i in range(depth):
        x = _layer(x, mem, norm_mix_w[i], w_in[i], q_norm_w[i], k_norm_w[i], attn_sink[i],
                   conv_w[i], conv_b[i], dt_bias_f[i], dt_bias_b[i], a_log_f[i], a_log_b[i],
                   ssd_d[i], ssd_norm_w[i], mem_norm_w[i], w_mem_kv[i], xq_norm_w[i],
                   xk_norm_w[i], w_out[i], norm_mlp_w[i], w_mlp_up[i], w_mlp_down[i], tables)
    return x
```

```python
import functools

import numpy as np
import jax
import jax.numpy as jnp
from jax import lax
from jax.experimental import pallas as pl
from jax.experimental.pallas import tpu as pltpu

D_MODEL = 1024
EPS = 1e-6

ATT_HEADS = 8
ATT_KV_HEADS = 2
ATT_HEAD_DIM = 64
ATT_Q_W = ATT_HEADS * ATT_HEAD_DIM
ATT_KV_W = ATT_KV_HEADS * ATT_HEAD_DIM
WINDOW = 128
ATT_BLOCK = 128
ATT_BLOCKS_PER_STEP = 16
ATT_HEADS_PER_CHAIN = 4
ROPE_THETA = 500000.0
ROPE_DIM = ATT_HEAD_DIM // 4

SSD_HEADS = 16
SSD_HEAD_DIM = 64
SSD_WIDTH = SSD_HEADS * SSD_HEAD_DIM
SSD_GROUPS = 2
SSD_STATE = 128
SSD_CONV = 5
SSD_CHUNK = 128
SSD_CHUNKS_PER_STEP = 16
SSD_XBC_W = SSD_WIDTH + 2 * SSD_GROUPS * SSD_STATE
SSD_DT_W = 2 * SSD_HEADS

XATT_HEADS = 4
XATT_HEAD_DIM = 128
XATT_W = XATT_HEADS * XATT_HEAD_DIM
XATT_TQ = 2048
TOKEN_TILE = 512

D_FF = 4 * D_MODEL
D_MIX = 2 * D_MODEL

LANES = 128
MXU_WIDTH = 256
BF16_SUBLANES = 16
F32_SUBLANES = 8
VMEM_LIMIT = 56 * 1024 * 1024

KDUP_W = 2 * ATT_KV_W
DT_PAD_W = LANES
CONV_HALO = BF16_SUBLANES
C_B = SSD_WIDTH
C_C = SSD_WIDTH + SSD_GROUPS * SSD_STATE

NEG_BIG = -1e30
LOG2E = 1.4426950408889634

bf16 = jnp.bfloat16
f32 = jnp.float32


def _dot(a, b):
    return jnp.dot(a, b, preferred_element_type=f32)


def _dot_nt(a, b):
    return lax.dot_general(a, b, (((1,), (1,)), ((), ())), preferred_element_type=f32)


def _split3(x):
    x1 = x.astype(bf16)
    r1 = x - x1.astype(f32)
    x2 = r1.astype(bf16)
    r2 = r1 - x2.astype(f32)
    return x1, x2, r2.astype(bf16)


def _dot3(x, m01):
    x1, x2, x3 = _split3(x)
    return _dot(x1, m01) + _dot(x2, m01) + _dot(x3, m01)


def _iota(shape, dim):
    return lax.broadcasted_iota(jnp.int32, shape, dim)


def _segment_ones(width, seg):
    r = _iota((width, width), 0) // seg
    c = _iota((width, width), 1) // seg
    return jnp.where(r == c, 1.0, 0.0).astype(bf16)


def _head_mean_sq(t, seg):
    width = min(t.shape[1], MXU_WIDTH)
    ones = _segment_ones(width, seg)
    sq = (t * t).astype(bf16)
    sums = [_dot(sq[:, c0:c0 + width], ones) for c0 in range(0, t.shape[1], width)]
    return jnp.concatenate(sums, axis=1) * (1.0 / seg)


def _rope(t, cos, sin_lo, sin_hi):
    half = ROPE_DIM // 2
    return (t * cos + pltpu.roll(t, LANES - half, 1) * sin_lo
            + pltpu.roll(t, half, 1) * sin_hi)


def _scheduled_after(x, anchor):
    folded = anchor[:, 0:LANES]
    for c0 in range(LANES, anchor.shape[1], LANES):
        folded = folded + anchor[:, c0:c0 + LANES]
    bits = pltpu.bitcast(jnp.broadcast_to(folded, (F32_SUBLANES, LANES)), jnp.uint32)
    sixteen = jnp.full(bits.shape, 16, jnp.uint32)
    zero = lax.shift_right_logical(lax.shift_right_logical(bits, sixteen), sixteen)
    zero = jnp.concatenate([zero.astype(f32)] * 2, axis=0).astype(bf16)
    head = x[0:BF16_SUBLANES, :] + jnp.concatenate([zero] * (x.shape[1] // LANES), axis=1)
    return jnp.concatenate([head, x[BF16_SUBLANES:, :]], axis=0)


def _rms_rows(x, w):
    ms = jnp.mean(x * x, axis=-1, keepdims=True)
    return (x * lax.rsqrt(ms + EPS)) * w


def _in_proj_body(pos_blocks, x_ref, xp_ref, xn_ref, nw_ref,
                  wq_ref, wk_ref, wvdt_ref, wz_ref, wxbc_ref, wqx_ref, qnw_ref, knw_ref,
                  xqnw_ref, cos_ref, sinlo_ref, sinhi_ref, cw_ref, cb_ref, dtb_ref, arow_ref,
                  q_ref, k_ref, v_ref, z_ref, xc_ref, qx_ref,
                  rows_ref, rowsl_ref, cols_ref, dtT_ref, conv_s):
    i = pl.program_id(0)
    tm = x_ref.shape[0]
    si = i % pos_blocks
    nw = nw_ref[...]
    h = _rms_rows(x_ref[...], nw).astype(bf16)

    cos = cos_ref[...]
    sin_lo = sinlo_ref[...]
    sin_hi = sinhi_ref[...]

    pq = _dot(h, wq_ref[...])
    qn = pq * lax.rsqrt(_head_mean_sq(pq, ATT_HEAD_DIM) + EPS) * qnw_ref[...]
    for c in range(ATT_Q_W // LANES):
        sl = slice(c * LANES, (c + 1) * LANES)
        q_ref[:, sl] = (_rope(qn[:, sl], cos, sin_lo, sin_hi)
                        * (ATT_HEAD_DIM ** -0.5 * LOG2E)).astype(bf16)

    pk = _dot(h, wk_ref[...])
    kn = pk * lax.rsqrt(_head_mean_sq(pk, ATT_HEAD_DIM) + EPS) * knw_ref[...]
    for c in range(KDUP_W // LANES):
        sl = slice(c * LANES, (c + 1) * LANES)
        k_ref[:, sl] = _rope(kn[:, sl], cos, sin_lo, sin_hi).astype(bf16)

    pvdt = _dot(h, wvdt_ref[...])
    v_ref[...] = pvdt[:, 0:ATT_KV_W].T.astype(bf16)
    pz = _dot(h, wz_ref[...])
    z_ref[...] = (pz / (1.0 + jnp.exp2(pz * (-LOG2E)))).astype(bf16)

    pqx = _dot(h, wqx_ref[...])
    for c in range(XATT_HEADS):
        sl = slice(c * XATT_HEAD_DIM, (c + 1) * XATT_HEAD_DIM)
        t = pqx[:, sl]
        tn = t * lax.rsqrt(jnp.mean(t * t, axis=-1, keepdims=True) + EPS) * xqnw_ref[...]
        qx_ref[:, sl] = (tn * (XATT_HEAD_DIM ** -0.5 * LOG2E)).astype(bf16)

    hp = _rms_rows(xp_ref[...], nw)
    hn = _rms_rows(xn_ref[...], nw)
    hp = jnp.where(si > 0, hp, jnp.zeros_like(hp)).astype(bf16)
    hn = jnp.where(si < pos_blocks - 1, hn, jnp.zeros_like(hn)).astype(bf16)
    h_ext = jnp.concatenate([hp, h, hn], axis=0)
    conv_s[...] = _dot(h_ext, wxbc_ref[...])
    sub = F32_SUBLANES
    nt = tm // sub
    t0 = CONV_HALO // sub
    u3 = conv_s[...].reshape((tm + 2 * CONV_HALO) // sub, sub, SSD_XBC_W)
    row_in_tile = _iota((1, sub, SSD_XBC_W), 1)
    acc = jnp.broadcast_to(cb_ref[...].reshape(1, 1, SSD_XBC_W), (nt, sub, SSD_XBC_W))
    for k in range(SSD_CONV):
        d = k - SSD_CONV // 2
        wk = cw_ref[k:k + 1, :].reshape(1, 1, SSD_XBC_W)
        if d == 0:
            shifted = u3[t0:t0 + nt]
        elif d > 0:
            rot = pltpu.roll(u3, sub - d, 1)
            shifted = jnp.where(row_in_tile < sub - d, rot[t0:t0 + nt], rot[t0 + 1:t0 + nt + 1])
        else:
            rot = pltpu.roll(u3, -d, 1)
            shifted = jnp.where(row_in_tile >= -d, rot[t0:t0 + nt], rot[t0 - 1:t0 + nt - 1])
        acc = acc + shifted * wk
    xc = acc / (1.0 + jnp.exp2(acc * (-LOG2E)))
    xc_ref[...] = xc.reshape(tm, SSD_XBC_W).astype(bf16)

    Q = SSD_CHUNK
    dt = pvdt[:, ATT_KV_W:ATT_KV_W + DT_PAD_W] + dtb_ref[...]
    dt = jnp.maximum(dt, 0.0) + jnp.log1p(jnp.exp(-jnp.abs(dt)))
    a = dt * arow_ref[...]
    li = _iota((Q, Q), 0)
    lj = _iota((Q, Q), 1)
    incl_le = jnp.where(li <= lj, 1.0, 0.0).astype(bf16)
    incl_ge = jnp.where(li >= lj, 1.0, 0.0).astype(bf16)
    pad_rows = jnp.zeros((Q - 2 * SSD_HEADS, Q), f32)
    for ch in range(tm // Q):
        r = slice(ch * Q, (ch + 1) * Q)
        aT = a[r].T
        dtT = dt[r].T
        pre = _dot3(aT[0:SSD_HEADS], incl_le)
        suf = _dot3(aT[SSD_HEADS:2 * SSD_HEADS], incl_ge)
        rows = jnp.concatenate([pre, suf, pad_rows], axis=0) * LOG2E
        rows_ref[r, :] = rows
        rowsl_ref[r, :] = rows - jnp.log(dtT) * LOG2E
        cols_ref[r, :] = rows.T
        dtT_ref[r, :] = dtT


def _in_proj(x2, norm_w, weights, qnw, knw, xqnw, cos_t, sinlo_t, sinhi_t,
             conv_w8, conv_b, dt_bias_row, a_row, seq, tm):
    tokens = x2.shape[0]
    pos_blocks = seq // tm
    hpt = tm // CONV_HALO
    n_halo = tokens // CONV_HALO
    row = lambda i: (i, 0)
    const = lambda i: (0, 0)
    pos = lambda i: (i % pos_blocks, 0)
    prev = lambda i: (jnp.maximum(i * hpt - 1, 0), 0)
    nxt = lambda i: (jnp.minimum((i + 1) * hpt, n_halo - 1), 0)
    out_widths = [ATT_Q_W, KDUP_W, KDUP_W, SSD_WIDTH, SSD_XBC_W, XATT_W,
                  LANES, LANES, LANES, LANES]
    out_dtypes = [bf16, bf16, bf16, bf16, bf16, bf16, f32, f32, f32, f32]
    out_specs = [pl.BlockSpec((tm, w), row) for w in out_widths]
    out_shape = [jax.ShapeDtypeStruct((tokens, w), d) for w, d in zip(out_widths, out_dtypes)]
    out_specs[2] = pl.BlockSpec((ATT_KV_W, tm), lambda i: (i // pos_blocks, i % pos_blocks))
    out_shape[2] = jax.ShapeDtypeStruct((tokens // seq * ATT_KV_W, seq), bf16)
    return pl.pallas_call(
        functools.partial(_in_proj_body, pos_blocks),
        grid=(tokens // tm,),
        in_specs=[
            pl.BlockSpec((tm, D_MODEL), row),
            pl.BlockSpec((CONV_HALO, D_MODEL), prev),
            pl.BlockSpec((CONV_HALO, D_MODEL), nxt),
            pl.BlockSpec((1, D_MODEL), const),
            *[pl.BlockSpec(w.shape, const, pipeline_mode=pl.Buffered(1)) for w in weights],
            pl.BlockSpec((1, ATT_Q_W), const),
            pl.BlockSpec((1, KDUP_W), const),
            pl.BlockSpec((1, XATT_HEAD_DIM), const),
            pl.BlockSpec((tm, LANES), pos),
            pl.BlockSpec((tm, LANES), pos),
            pl.BlockSpec((tm, LANES), pos),
            pl.BlockSpec((8, SSD_XBC_W), const),
            pl.BlockSpec((1, SSD_XBC_W), const),
            pl.BlockSpec((1, DT_PAD_W), const),
            pl.BlockSpec((1, DT_PAD_W), const),
        ],
        out_specs=out_specs,
        out_shape=out_shape,
        scratch_shapes=[pltpu.VMEM((tm + 2 * CONV_HALO, SSD_XBC_W), f32)],
        compiler_params=pltpu.CompilerParams(
            dimension_semantics=("arbitrary",), vmem_limit_bytes=VMEM_LIMIT),
        name="in_proj",
    )(x2, x2, x2, norm_w, *weights, qnw, knw, xqnw, cos_t, sinlo_t, sinhi_t,
      conv_w8, conv_b, dt_bias_row, a_row)


def _attn_kernel(sink_ref, mfirst_ref, mmid_ref, mlast_ref, q_ref, kp_ref, kc_ref, kn_ref,
                 vp_ref, vc_ref, vn_ref, o_ref):
    blk = ATT_BLOCK
    per = q_ref.shape[0] // blk
    heads_per_kv = ATT_HEADS // ATT_KV_HEADS
    hpc = ATT_HEADS_PER_CHAIN
    cols = hpc * blk
    low_lanes = _iota((blk, LANES), 1) < ATT_HEAD_DIM
    head_of_col = _iota((1, cols), 1) // blk
    masks = [mfirst_ref] + [mmid_ref] * (per - 2) + [mlast_ref]

    kblocks, vblocks = [], []
    for g in range(ATT_KV_HEADS):
        gs = slice(g * LANES, (g + 1) * LANES)
        vs = slice(g * ATT_HEAD_DIM, (g + 1) * ATT_HEAD_DIM)
        kblocks.append([kp_ref[:, gs]] + [kc_ref[i * blk:(i + 1) * blk, gs] for i in range(per)]
                       + [kn_ref[:, gs]])
        vblocks.append([vp_ref[vs, :]] + [vc_ref[vs, i * blk:(i + 1) * blk] for i in range(per)]
                       + [vn_ref[vs, :]])
    ones_rows = jnp.ones((LANES - ATT_HEAD_DIM, 3 * blk), bf16)

    def scores(g, t, h0):
        qrows = slice(t * blk, (t + 1) * blk)
        kcat = jnp.concatenate(kblocks[g][t:t + 3], axis=0)
        qs = []
        sink = jnp.zeros((1, cols), f32)
        for j in range(hpc):
            hd = h0 + j
            pair = q_ref[qrows, (hd // 2) * LANES:(hd // 2 + 1) * LANES]
            keep = low_lanes if hd % 2 == 0 else jnp.logical_not(low_lanes)
            qs.append(jnp.where(keep, pair, jnp.zeros_like(pair)))
            sink = jnp.where(head_of_col == j, sink_ref[hd] * LOG2E, sink)
        qg = jnp.concatenate(qs, axis=0)
        raw = _dot_nt(kcat, qg)
        st = jnp.concatenate(
            [raw[0:blk] + masks[t][0:blk, 0:cols], raw[blk:2 * blk],
             raw[2 * blk:] + masks[t][2 * blk:, 0:cols]], axis=0)
        m = jnp.maximum(jnp.max(st, axis=0, keepdims=True), sink)
        return st, m, sink, raw[3 * blk - 1:3 * blk, :]

    def finish(g, t, h0, st, m, sink, tail_next):
        qrows = slice(t * blk, (t + 1) * blk)
        vt = jnp.concatenate(
            [jnp.concatenate(vblocks[g][t:t + 3], axis=1), ones_rows], axis=0)
        if tail_next is not None:
            vt = _scheduled_after(vt, tail_next)
        p = jnp.exp2(st - m).astype(bf16)
        ov = _dot(vt, p)
        den = ov[ATT_HEAD_DIM:ATT_HEAD_DIM + 1] + jnp.exp2(sink - m)
        ot = ov[0:ATT_HEAD_DIM] / den
        for jp in range(hpc // 2):
            even = ot[:, (2 * jp) * blk:(2 * jp + 1) * blk]
            odd = ot[:, (2 * jp + 1) * blk:(2 * jp + 2) * blk]
            pr = h0 // 2 + jp
            o_ref[qrows, pr * LANES:(pr + 1) * LANES] = (
                jnp.concatenate([even, odd], axis=0).T.astype(bf16))

    chains = [(g, t, g * heads_per_kv + c * hpc) for g in range(ATT_KV_HEADS)
              for t in range(per) for c in range(heads_per_kv // hpc)]
    nxt = scores(*chains[0])
    for idx, ch in enumerate(chains):
        st, m, sink, _ = nxt
        nxt = scores(*chains[idx + 1]) if idx + 1 < len(chains) else None
        finish(*ch, st, m, sink, None if nxt is None else nxt[3])


def _attn_masks():
    blk = ATT_BLOCK
    cols = (ATT_HEADS // ATT_KV_HEADS) * blk
    key = np.arange(3 * blk)[:, None]
    qi = np.arange(cols)[None, :] % blk
    band = (key >= qi) & (key <= qi + 2 * WINDOW)
    variants = [band & (key >= blk), band, band & (key < 2 * blk)]
    return jnp.asarray(np.where(np.stack(variants), 0.0, NEG_BIG), dtype=f32)


def _attention(sink, q, kdup, vt, batch, seq):
    blk = ATT_BLOCK
    nb = seq // blk
    per = min(ATT_BLOCKS_PER_STEP, nb)
    ns = nb // per
    cols = (ATT_HEADS // ATT_KV_HEADS) * blk
    masks = _attn_masks()
    cur = lambda b, s: (b * ns + s, 0)
    prev = lambda b, s: (b * nb + jnp.maximum(per * s - 1, 0), 0)
    nxt = lambda b, s: (b * nb + jnp.minimum(per * s + per, nb - 1), 0)
    vcur = lambda b, s: (b, s)
    vprev = lambda b, s: (b, jnp.maximum(per * s - 1, 0))
    vnxt = lambda b, s: (b, jnp.minimum(per * s + per, nb - 1))
    mfirst = lambda b, s: (jnp.where(s == 0, 0, 1), 0, 0)
    mmid = lambda b, s: (1, 0, 0)
    mlast = lambda b, s: (jnp.where(s == ns - 1, 2, 1), 0, 0)
    return pl.pallas_call(
        _attn_kernel,
        grid=(batch, ns),
        in_specs=[
            pl.BlockSpec(memory_space=pltpu.SMEM),
            pl.BlockSpec((None, 3 * blk, cols), mfirst),
            pl.BlockSpec((None, 3 * blk, cols), mmid),
            pl.BlockSpec((None, 3 * blk, cols), mlast),
            pl.BlockSpec((per * blk, ATT_Q_W), cur),
            pl.BlockSpec((blk, KDUP_W), prev),
            pl.BlockSpec((per * blk, KDUP_W), cur),
            pl.BlockSpec((blk, KDUP_W), nxt),
            pl.BlockSpec((ATT_KV_W, blk), vprev),
            pl.BlockSpec((ATT_KV_W, per * blk), vcur),
            pl.BlockSpec((ATT_KV_W, blk), vnxt),
        ],
        out_specs=pl.BlockSpec((per * blk, ATT_Q_W), cur),
        out_shape=jax.ShapeDtypeStruct((batch * seq, ATT_Q_W), bf16),
        compiler_params=pltpu.CompilerParams(
            dimension_semantics=("arbitrary", "arbitrary"), vmem_limit_bytes=VMEM_LIMIT),
        name="attention",
    )(sink, masks, masks, masks, q, kdup, kdup, kdup, vt, vt, vt)


def _ssd_kernel(xc_ref, rows_ref, rowsl_ref, cols_ref, dtT_ref, dmat_ref, y_ref,
                hf_s, sb_s, hf_run, hb_run):
    ph = pl.program_id(1)
    step = pl.program_id(2)
    nsteps = pl.num_programs(2)
    Q = SSD_CHUNK
    per = xc_ref.shape[0] // Q
    heads_per_group = SSD_HEADS // SSD_GROUPS
    low_lanes = _iota((Q, LANES), 1) < SSD_HEAD_DIM
    low_row = _iota((1, LANES), 1) < SSD_HEAD_DIM
    li = _iota((Q, Q), 0)
    si = _iota((Q, Q), 1)

    def chunk_states(c, rs):
        rows = rows_ref[rs, :]
        dtT = dtT_ref[rs, :]
        pre = rows[0:SSD_HEADS]
        suf = rows[SSD_HEADS:2 * SSD_HEADS]
        pre_end = jnp.broadcast_to(pre[:, Q - 1:Q], (SSD_HEADS, Q))
        suf_end = jnp.broadcast_to(suf[:, 0:1], (SSD_HEADS, Q))
        wf = jnp.exp2(pre_end - pre) * dtT[0:SSD_HEADS]
        wb = jnp.exp2(suf_end - suf) * dtT[SSD_HEADS:2 * SSD_HEADS]
        decf = jnp.exp2(pre_end)

        for g in range(SSD_GROUPS):
            bT = xc_ref[rs, C_B + g * SSD_STATE:C_B + (g + 1) * SSD_STATE].astype(f32).T
            for m in range(heads_per_group // 2):
                h0 = g * heads_per_group + 2 * m
                h1 = h0 + 1
                lhs = jnp.concatenate(
                    [bT * wf[h0:h0 + 1], bT * wb[h0:h0 + 1],
                     bT * wf[h1:h1 + 1], bT * wb[h1:h1 + 1]], axis=0).astype(bf16)
                cs = slice(h0 * SSD_HEAD_DIM, h0 * SSD_HEAD_DIM + LANES)
                res = _dot(lhs, xc_ref[rs, cs])
                s_f = jnp.where(low_lanes, res[0:Q], res[2 * Q:3 * Q])
                s_b = jnp.where(low_lanes, res[Q:2 * Q], res[3 * Q:4 * Q])
                dec = jnp.where(low_row, decf[h0:h0 + 1], decf[h1:h1 + 1])
                prev = hf_run[:, cs]
                hf_s[c, :, cs] = prev.astype(bf16)
                hf_run[:, cs] = dec * prev + s_f
                sb_s[c, :, cs] = s_b.astype(bf16)

    def chunk_outputs(c, rs):
        rows = rows_ref[rs, :]
        suf = rows[SSD_HEADS:2 * SSD_HEADS]
        cols = cols_ref[rs, :]
        rowsl = rowsl_ref[rs, :]
        decb = jnp.exp2(jnp.broadcast_to(suf[:, 0:1], (SSD_HEADS, Q)))
        lower = li >= si
        upper = si >= li

        for g in range(SSD_GROUPS):
            bg = xc_ref[rs, C_B + g * SSD_STATE:C_B + (g + 1) * SSD_STATE]
            cg = xc_ref[rs, C_C + g * SSD_STATE:C_C + (g + 1) * SSD_STATE]
            cbm = _dot_nt(cg, bg).astype(bf16)
            for m in range(heads_per_group // 2):
                h0 = g * heads_per_group + 2 * m
                parts, dec_f, dec_b = [], [], []
                for hd in (h0, h0 + 1):
                    hb = SSD_HEADS + hd
                    colf = jnp.broadcast_to(cols[:, hd:hd + 1], (Q, Q))
                    colb = jnp.broadcast_to(cols[:, hb:hb + 1], (Q, Q))
                    ef = jnp.exp2(jnp.where(lower, colf - rowsl[hd:hd + 1], NEG_BIG)).astype(bf16)
                    eb = jnp.exp2(jnp.where(upper, colb - rowsl[hb:hb + 1], NEG_BIG)).astype(bf16)
                    parts.append(cbm * (ef + eb) + dmat_ref[hd])
                    dec_f.append(colf)
                    dec_b.append(colb)
                cs = slice(h0 * SSD_HEAD_DIM, h0 * SSD_HEAD_DIM + LANES)
                hb_prev = hb_run[:, cs]
                res = _dot(jnp.concatenate(parts, axis=0), xc_ref[rs, cs])
                off = _dot(cg, jnp.concatenate([hf_s[c, :, cs], hb_prev.astype(bf16)], axis=1))
                y = (jnp.where(low_lanes, res[0:Q], res[Q:2 * Q])
                     + off[:, 0:LANES] * jnp.exp2(jnp.where(low_lanes, dec_f[0], dec_f[1]))
                     + off[:, LANES:] * jnp.exp2(jnp.where(low_lanes, dec_b[0], dec_b[1])))
                y_ref[rs, cs] = y.astype(bf16)
                dec = jnp.where(low_row, decb[h0:h0 + 1], decb[h0 + 1:h0 + 2])
                hb_run[:, cs] = dec * hb_prev + sb_s[c, :, cs].astype(f32)

    @pl.when(ph == 0)
    def _phase0():
        @pl.when(step == 0)
        def _():
            hf_run[...] = jnp.zeros_like(hf_run)

        for sub in range(per):
            chunk_states(step * per + sub, slice(sub * Q, (sub + 1) * Q))

    @pl.when(ph == 1)
    def _phase1():
        @pl.when(step == 0)
        def _():
            hb_run[...] = jnp.zeros_like(hb_run)

        back_step = nsteps - 1 - step
        for sub in reversed(range(per)):
            chunk_outputs(back_step * per + sub, slice(sub * Q, (sub + 1) * Q))


def _ssd(dmat, xc, rows, rowsl, cols, dtT, batch, seq):
    Q = SSD_CHUNK
    nc = seq // Q
    rb = min(SSD_CHUNKS_PER_STEP, nc) * Q
    ns = seq // rb
    last = ns - 1

    def both_map(b, ph, s):
        return (b * ns + s * (1 - ph) + (last - s) * ph, 0)

    def fwd_map(b, ph, s):
        return (b * ns + s * (1 - ph) + last * ph, 0)

    def back_map(b, ph, s):
        return (b * ns + last - s * ph, 0)

    return pl.pallas_call(
        _ssd_kernel,
        grid=(batch, 2, ns),
        in_specs=[
            pl.BlockSpec((rb, SSD_XBC_W), both_map),
            pl.BlockSpec((rb, LANES), both_map),
            pl.BlockSpec((rb, LANES), back_map),
            pl.BlockSpec((rb, LANES), back_map),
            pl.BlockSpec((rb, LANES), fwd_map),
            pl.BlockSpec((SSD_HEADS, Q, Q), lambda b, ph, s: (0, 0, 0)),
        ],
        out_specs=pl.BlockSpec((rb, SSD_WIDTH), back_map),
        out_shape=jax.ShapeDtypeStruct((batch * seq, SSD_WIDTH), bf16),
        scratch_shapes=[
            pltpu.VMEM((nc, SSD_STATE, SSD_WIDTH), bf16),
            pltpu.VMEM((nc, SSD_STATE, SSD_WIDTH), bf16),
            pltpu.VMEM((SSD_STATE, SSD_WIDTH), f32),
            pltpu.VMEM((SSD_STATE, SSD_WIDTH), f32),
        ],
        compiler_params=pltpu.CompilerParams(
            dimension_semantics=("arbitrary", "arbitrary", "arbitrary"),
            vmem_limit_bytes=VMEM_LIMIT),
        name="ssd",
    )(xc, rows, rowsl, cols, dtT, dmat)


def _mem_kv(mem_ref, nw_ref, w_ref, knw_ref, k_ref, v_ref):
    h = _rms_rows(mem_ref[...], nw_ref[...]).astype(bf16)
    kv = _dot(h, w_ref[...])
    for c in range(XATT_HEADS):
        sl = slice(c * XATT_HEAD_DIM, (c + 1) * XATT_HEAD_DIM)
        t = kv[:, sl]
        k_ref[:, sl] = (t * lax.rsqrt(jnp.mean(t * t, axis=-1, keepdims=True) + EPS)
                        * knw_ref[...]).astype(bf16)
    v_ref[...] = kv[:, XATT_W:].astype(bf16)


def _xattn_kernel(q_ref, mem_ref, nw_ref, w_ref, knw_ref, o_ref, k_ref, v_ref):
    @pl.when(pl.program_id(1) == 0)
    def _():
        _mem_kv(mem_ref, nw_ref, w_ref, knw_ref, k_ref, v_ref)

    for hd in range(XATT_HEADS):
        sl = slice(hd * XATT_HEAD_DIM, (hd + 1) * XATT_HEAD_DIM)
        s = _dot_nt(q_ref[:, sl], k_ref[:, sl])
        m = jnp.max(s, axis=-1, keepdims=True)
        p = jnp.exp2(s - m)
        den = jnp.sum(p, axis=-1, keepdims=True)
        o_ref[:, sl] = (_dot(p.astype(bf16), v_ref[:, sl]) / den).astype(bf16)


def _xattn(qx, mem2, mem_norm_w, w_kv, xk_norm_w, batch, seq, mem_len, tq):
    nq = seq // tq
    qmap = lambda b, i: (b * nq + i, 0)
    mmap = lambda b, i: (b, 0)
    const = lambda b, i: (0, 0)
    return pl.pallas_call(
        _xattn_kernel,
        grid=(batch, nq),
        in_specs=[
            pl.BlockSpec((tq, XATT_W), qmap),
            pl.BlockSpec((mem_len, D_MODEL), mmap),
            pl.BlockSpec((1, D_MODEL), const),
            pl.BlockSpec((D_MODEL, 2 * XATT_W), const),
            pl.BlockSpec((1, XATT_HEAD_DIM), const),
        ],
        out_specs=pl.BlockSpec((tq, XATT_W), qmap),
        out_shape=jax.ShapeDtypeStruct((batch * seq, XATT_W), bf16),
        scratch_shapes=[pltpu.VMEM((mem_len, XATT_W), bf16),
                        pltpu.VMEM((mem_len, XATT_W), bf16)],
        compiler_params=pltpu.CompilerParams(
            dimension_semantics=("arbitrary", "arbitrary"), vmem_limit_bytes=VMEM_LIMIT),
        name="xattn",
    )(qx, mem2, mem_norm_w, w_kv, xk_norm_w)


FF_CHUNK = 1024


def _out_mlp_kernel(x_ref, a_ref, s_ref, z_ref, c_ref, wo_ref, snw_ref, nw_ref, wu_ref, wd_ref,
                    o_ref):
    y = s_ref[...].astype(f32) * z_ref[...].astype(f32)
    gw = SSD_WIDTH // SSD_GROUPS
    mix = (_dot(a_ref[...], wo_ref[0:ATT_Q_W, :])
           + _dot(c_ref[...], wo_ref[ATT_Q_W + SSD_WIDTH:, :]))
    for g in range(SSD_GROUPS):
        yg = y[:, g * gw:(g + 1) * gw]
        ms = jnp.mean(yg * yg, axis=-1, keepdims=True)
        yn = (yg * lax.rsqrt(ms + EPS) * snw_ref[:, g * gw:(g + 1) * gw]).astype(bf16)
        mix = mix + _dot(yn, wo_ref[ATT_Q_W + g * gw:ATT_Q_W + (g + 1) * gw, :])
    x1 = x_ref[...] + mix
    h = _rms_rows(x1, nw_ref[...]).astype(bf16)
    acc = x1
    for j in range(D_FF // FF_CHUNK):
        u = _dot(h, wu_ref[:, j * FF_CHUNK:(j + 1) * FF_CHUNK])
        r = jnp.maximum(u, 0.0)
        acc = acc + _dot((r * r).astype(bf16), wd_ref[j * FF_CHUNK:(j + 1) * FF_CHUNK, :])
    o_ref[...] = acc


def _out_mlp(x2, attn, ssd, z, xatt, w_out, ssd_norm_w, norm_w, w_up, w_down, tm):
    tokens = x2.shape[0]
    row = lambda i: (i, 0)
    const = lambda i: (0, 0)
    resident = functools.partial(pl.BlockSpec, index_map=const, pipeline_mode=pl.Buffered(1))
    return pl.pallas_call(
        _out_mlp_kernel,
        grid=(tokens // tm,),
        in_specs=[
            pl.BlockSpec((tm, D_MODEL), row),
            pl.BlockSpec((tm, ATT_Q_W), row),
            pl.BlockSpec((tm, SSD_WIDTH), row),
            pl.BlockSpec((tm, SSD_WIDTH), row),
            pl.BlockSpec((tm, XATT_W), row),
            resident((D_MIX, D_MODEL)),
            pl.BlockSpec((1, SSD_WIDTH), const),
            pl.BlockSpec((1, D_MODEL), const),
            resident((D_MODEL, D_FF)),
            resident((D_FF, D_MODEL)),
        ],
        out_specs=pl.BlockSpec((tm, D_MODEL), row),
        out_shape=jax.ShapeDtypeStruct((tokens, D_MODEL), f32),
        compiler_params=pltpu.CompilerParams(
            dimension_semantics=("arbitrary",), vmem_limit_bytes=VMEM_LIMIT),
        name="out_mlp",
    )(x2, attn, ssd, z, xatt, w_out, ssd_norm_w, norm_w, w_up, w_down)


def _rope_tables(seq):
    half = ROPE_DIM // 2
    inv = ROPE_THETA ** (-jnp.arange(0, ROPE_DIM, 2, dtype=f32) / ROPE_DIM)
    ang = jnp.arange(seq, dtype=f32)[:, None] * inv[None, :]
    cos = jnp.cos(ang)
    sin = jnp.sin(ang)
    ones = jnp.ones((seq, ATT_HEAD_DIM - ROPE_DIM), f32)
    zeros_h = jnp.zeros((seq, half), f32)
    zeros_p = jnp.zeros((seq, ATT_HEAD_DIM - ROPE_DIM), f32)
    cos_head = jnp.concatenate([cos, cos, ones], axis=1)
    lo_head = jnp.concatenate([-sin, zeros_h, zeros_p], axis=1)
    hi_head = jnp.concatenate([zeros_h, sin, zeros_p], axis=1)
    rep = LANES // ATT_HEAD_DIM
    return (jnp.tile(cos_head, (1, rep)), jnp.tile(lo_head, (1, rep)),
            jnp.tile(hi_head, (1, rep)))


def _layer(x, mem, norm_mix_w, w_in, q_norm_w, k_norm_w, attn_sink, conv_w, conv_b,
           dt_bias_f, dt_bias_b, a_log_f, a_log_b, ssd_d, ssd_norm_w, mem_norm_w,
           w_mem_kv, xq_norm_w, xk_norm_w, w_out, norm_mlp_w, w_mlp_up, w_mlp_down, tables):
    batch, seq, _ = x.shape
    mem_len = mem.shape[1]
    tokens = batch * seq
    tm = TOKEN_TILE
    assert x.shape[2] == D_MODEL and mem.shape[2] == D_MODEL
    assert seq % tm == 0 and mem_len % BF16_SUBLANES == 0
    assert seq % (ATT_BLOCK * min(ATT_BLOCKS_PER_STEP, seq // ATT_BLOCK)) == 0
    assert seq % (SSD_CHUNK * min(SSD_CHUNKS_PER_STEP, seq // SSD_CHUNK)) == 0
    assert seq % min(XATT_TQ, seq) == 0

    s = np.cumsum([ATT_Q_W, ATT_KV_W, ATT_KV_W, SSD_WIDTH, SSD_XBC_W, SSD_DT_W, XATT_W])
    piece = lambda lo, hi: w_in[:, lo:hi].astype(bf16)
    w_q, w_k, w_v = piece(0, s[0]), piece(s[0], s[1]), piece(s[1], s[2])
    w_z, w_xbc, w_dt, w_qx = (piece(s[2], s[3]), piece(s[3], s[4]),
                              piece(s[4], s[5]), piece(s[5], s[6]))
    hd = ATT_HEAD_DIM
    dup = lambda w: jnp.concatenate(
        [w[:, g * hd:(g + 1) * hd] for g in range(ATT_KV_HEADS) for _ in range(2)], axis=1)
    w_vdt = jnp.concatenate(
        [w_v, w_dt, jnp.zeros((D_MODEL, DT_PAD_W - SSD_DT_W), bf16)], axis=1)
    weights = (w_q, dup(w_k), w_vdt, w_z, w_xbc, w_qx)

    pad = jnp.zeros((DT_PAD_W - SSD_DT_W,), f32)
    dt_bias_row = jnp.concatenate([dt_bias_f, dt_bias_b, pad]).reshape(1, DT_PAD_W)
    a_row = jnp.concatenate([-jnp.exp(a_log_f), -jnp.exp(a_log_b), pad]).reshape(1, DT_PAD_W)
    conv_w8 = jnp.concatenate([conv_w, jnp.zeros((8 - SSD_CONV, SSD_XBC_W), f32)], axis=0)
    dmat = (ssd_d.astype(f32)[:, None, None] * jnp.eye(SSD_CHUNK, dtype=f32)).astype(bf16)

    x2 = x.reshape(tokens, D_MODEL)
    cos_t, sinlo_t, sinhi_t = tables
    q, kdup, vt, z, xc, qx, rows, rowsl, cols, dtT = _in_proj(
        x2, norm_mix_w.reshape(1, D_MODEL), weights,
        jnp.tile(q_norm_w, ATT_HEADS).reshape(1, ATT_Q_W),
        jnp.tile(k_norm_w, KDUP_W // hd).reshape(1, KDUP_W),
        xq_norm_w.reshape(1, XATT_HEAD_DIM), cos_t, sinlo_t, sinhi_t,
        conv_w8, conv_b.reshape(1, SSD_XBC_W), dt_bias_row, a_row, seq, tm)

    attn = _attention(attn_sink.astype(f32), q, kdup, vt, batch, seq)
    ssd = _ssd(dmat, xc, rows, rowsl, cols, dtT, batch, seq)

    xatt = _xattn(qx, mem.reshape(batch * mem_len, D_MODEL), mem_norm_w.reshape(1, D_MODEL),
                  w_mem_kv.astype(bf16), xk_norm_w.reshape(1, XATT_HEAD_DIM),
                  batch, seq, mem_len, min(XATT_TQ, seq))

    out = _out_mlp(x2, attn, ssd, z, xatt, w_out.astype(bf16),
                   ssd_norm_w.reshape(1, SSD_WIDTH), norm_mlp_w.reshape(1, D_MODEL),
                   w_mlp_up.astype(bf16), w_mlp_down.astype(bf16), tm)
    return out.reshape(batch, seq, D_MODEL)


def kernel(x, mem, norm_mix_w, w_in, q_norm_w, k_norm_w, attn_sink, conv_w, conv_b,
           dt_bias_f, dt_bias_b, a_log_f, a_log_b, ssd_d, ssd_norm_w, mem_norm_w,
           w_mem_kv, xq_norm_w, xk_norm_w, w_out, norm_mlp_w, w_mlp_up, w_mlp_down):
    depth = w_in.shape[0]
    tables = _rope_tables(x.shape[1])
    for i in range(depth):
        x = _layer(x, mem, norm_mix_w[i], w_in[i], q_norm_w[i], k_norm_w[i], attn_sink[i],
                   conv_w[i], conv_b[i], dt_bias_f[i], dt_bias_b[i], a_log_f[i], a_log_b[i],
                   ssd_d[i], ssd_norm_w[i], mem_norm_w[i], w_mem_kv[i], xq_norm_w[i],
                   xk_norm_w[i], w_out[i], norm_mlp_w[i], w_mlp_up[i], w_mlp_down[i], tables)
    return x
```
